```python
import math
import jax
import jax.numpy as jnp
from jax import lax
import numpy as np

D_MODEL = 4096
BATCH = 2
SEQ = 8192
DEPTH = 2

CTX_LEN = 256
GRID_W = 64
HEAD_DIM = D_MODEL // 32
A_HEADS = 8
A_KV_HEADS = 2
A_WINDOW = 128
A_BLOCK = 128
B_HEADS = 8
NA_ROWS = 8
NA_COLS = 16
C_HEADS = 8
C_SUB_DIM = HEAD_DIM // 2
Q_BLOCK = 128
D_HEADS = 8
RET_CHUNK = 128
ROPE_BASE = 10000.0
N_GROUPS = 4
EXPERTS_PER_GROUP = 8
N_EXPERTS = N_GROUPS * EXPERTS_PER_GROUP
TOP_K_IN_GROUP = 2
D_EXPERT = D_MODEL // 8
MOE_BLOCK = 128
EPS = 1e-6
NEG_INF = -1e30

A_Q = A_HEADS * HEAD_DIM
A_KV = A_KV_HEADS * HEAD_DIM
B_W = B_HEADS * HEAD_DIM
C_W = C_HEADS * HEAD_DIM
D_W = D_HEADS * HEAD_DIM
MIX_WIDTH = A_Q + B_W + C_W + D_W
IN_WIDTHS = [A_Q, A_KV, A_KV, B_W, B_W, B_W, C_W, C_W, C_W, D_W, D_W, D_W, D_W]
IN_COLS = sum(IN_WIDTHS)
IN_OFFSETS = [int(v) for v in np.cumsum(IN_WIDTHS)[:-1]]

kernel_name = 'hybrid_headgroup_dit_moe_trunk'


def rms_norm(x, gain=None):
    xf = x.astype(jnp.float32)
    y = xf * lax.rsqrt(jnp.mean(xf * xf, axis=-1, keepdims=True) + EPS)
    if gain is not None:
        y = y * gain.astype(jnp.float32)
    return y.astype(x.dtype)


def modulate(x, gain, shift, scale):
    return rms_norm(x, gain) * (1.0 + scale) + shift


def rope_tables(L, dim):
    t = jnp.arange(L)
    row = (t // GRID_W).astype(jnp.float32)
    col = (t % GRID_W).astype(jnp.float32)
    quarter = dim // 4
    inv = ROPE_BASE ** (-jnp.arange(quarter, dtype=jnp.float32) / quarter)
    ar = row[:, None] * inv[None, :]
    ac = col[:, None] * inv[None, :]
    ang = jnp.concatenate([ar, ar, ac, ac], axis=-1)
    return jnp.cos(ang), jnp.sin(ang)


def apply_rope(x, cos, sin):
    half = x.shape[-1] // 2
    quarter = half // 2
    xr, xcol = x[..., :half], x[..., half:]
    rot = jnp.concatenate([-xr[..., quarter:], xr[..., :quarter], -xcol[..., quarter:], xcol[..., :quarter]], axis=-1)
    return (x.astype(jnp.float32) * cos + rot.astype(jnp.float32) * sin).astype(x.dtype)


def ctx_attention(q, k, v, sink):
    B_, C, H, dh = q.shape
    KV = k.shape[2]
    G = H // KV
    qg = q.reshape(B_, C, KV, G, dh)
    s = jnp.einsum('bqkgd,bckd->bkgqc', qg, k).astype(jnp.float32) * (dh ** -0.5)
    if sink is not None:
        s_sink = jnp.broadcast_to(sink.astype(jnp.float32).reshape(1, KV, G, 1, 1), s.shape[:-1] + (1,))
        s = jnp.concatenate([s, s_sink], axis=-1)
    p = jax.nn.softmax(s, axis=-1)[..., :C].astype(v.dtype)
    return jnp.einsum('bkgqc,bckd->bqkgd', p, v).reshape(B_, C, H * dh)


def band_blocks(t, blk):
    B_, L = t.shape[0], t.shape[1]
    nb = L // blk
    tp = jnp.pad(t, ((0, 0), (blk, blk), (0, 0), (0, 0)))
    tb = tp.reshape(B_, nb + 2, blk, t.shape[2], t.shape[3])
    return jnp.concatenate([tb[:, :-2], tb[:, 1:-1], tb[:, 2:]], axis=2)


def window_mask(nb, blk, window):
    s = jnp.arange(blk)[:, None]
    t = jnp.arange(3 * blk)[None, :]
    rel = t - blk - s
    j = jnp.arange(nb)[:, None, None] * blk - blk + t[None]
    return (jnp.abs(rel) <= window)[None] & (j >= 0) & (j < nb * blk)


def window_gqa_latent(q, k, v, kc, vc, sink):
    B_, L, H, dh = q.shape
    KV = k.shape[2]
    G = H // KV
    nb = L // A_BLOCK
    C = kc.shape[1]
    scale = dh ** -0.5
    qb = q.reshape(B_, nb, A_BLOCK, KV, G, dh)
    kb = band_blocks(k, A_BLOCK)
    vb = band_blocks(v, A_BLOCK)
    s_lat = jnp.einsum('bnqkgd,bntkd->bnkgqt', qb, kb).astype(jnp.float32) * scale
    s_lat = jnp.where(window_mask(nb, A_BLOCK, A_WINDOW)[None, :, None, None], s_lat, NEG_INF)
    s_ctx = jnp.einsum('bnqkgd,bckd->bnkgqc', qb, kc).astype(jnp.float32) * scale
    s_sink = jnp.broadcast_to(sink.astype(jnp.float32).reshape(1, 1, KV, G, 1, 1), s_lat.shape[:-1] + (1,))
    p = jax.nn.softmax(jnp.concatenate([s_lat, s_ctx, s_sink], axis=-1), axis=-1).astype(v.dtype)
    nt = 3 * A_BLOCK
    o = (jnp.einsum('bnkgqt,bntkd->bnqkgd', p[..., :nt], vb)
         + jnp.einsum('bnkgqc,bckd->bnqkgd', p[..., nt:nt + C], vc))
    return o.reshape(B_, L, H * dh)


def neighbourhood_latent(q, k, v, kc, vc, rpb):
    B_, L, H, dh = q.shape
    W = GRID_W
    R = L // W
    KH = min(NA_ROWS, R)
    KW = min(NA_COLS, W)
    scale = dh ** -0.5
    r = jnp.arange(R)
    rs = jnp.clip(r - KH // 2, 0, R - KH)
    row_idx = rs[:, None] + jnp.arange(KH)[None, :]
    qg = q.reshape(B_, R, W, H, dh)
    kg = k.reshape(B_, R, W, H, dh)[:, row_idx].reshape(B_, R, KH * W, H, dh)
    vg = v.reshape(B_, R, W, H, dh)[:, row_idx].reshape(B_, R, KH * W, H, dh)
    cidx = jnp.arange(W)
    cs = jnp.clip(cidx - KW // 2, 0, W - KW)
    colmask = (cidx[None, :] >= cs[:, None]) & (cidx[None, :] < cs[:, None] + KW)
    rel_r = row_idx - r[:, None] + (NA_ROWS - 1)
    rel_c = jnp.clip(cidx[None, :] - cidx[:, None] + (NA_COLS - 1), 0, 2 * NA_COLS - 2)
    bias = rpb.astype(jnp.float32)[:, rel_r[:, None, :, None], rel_c[None, :, None, :]]
    bias = jnp.where(colmask[None, None, :, None, :], bias, NEG_INF)
    bias = jnp.transpose(bias, (1, 0, 2, 3, 4)).reshape(R, H, W, KH * W)
    s_lat = jnp.einsum('brqhd,brkhd->brhqk', qg, kg).astype(jnp.float32) * scale + bias[None]
    s_ctx = jnp.einsum('brqhd,bchd->brhqc', qg, kc).astype(jnp.float32) * scale
    p = jax.nn.softmax(jnp.concatenate([s_lat, s_ctx], axis=-1), axis=-1).astype(v.dtype)
    nk = KH * W
    o = (jnp.einsum('brhqk,brkhd->brqhd', p[..., :nk], vg)
         + jnp.einsum('brhqc,bchd->brqhd', p[..., nk:], vc))
    return o.reshape(B_, L, H * dh)


def diff_lambda(lambda_params, lam_init):
    lp = lambda_params.astype(jnp.float32)
    return jnp.exp(jnp.sum(lp[0] * lp[1])) - jnp.exp(jnp.sum(lp[2] * lp[3])) + lam_init


def diff_attend(q, k, v, lam):
    s = jnp.einsum('bqhsd,bkhsd->bhsqk', q, k).astype(jnp.float32) * (q.shape[-1] ** -0.5)
    p = jax.nn.softmax(s, axis=-1)
    a = p[:, :, 0] - lam * p[:, :, 1]
    return jnp.einsum('bhqk,bkhe->bqhe', a.astype(v.dtype), v)


def diff_finish(o, gain, lam_init):
    y = rms_norm(o, gain) * (1.0 - lam_init)
    return y.reshape(o.shape[0], o.shape[1], o.shape[2] * o.shape[3])


def diff_attention_latent(q, k, v, kc, vc, lam, gain, lam_init):
    B_, L = q.shape[0], q.shape[1]
    nb = L // Q_BLOCK
    k_all = jnp.concatenate([k, kc], axis=1)
    v_all = jnp.concatenate([v, vc], axis=1)
    qb = jnp.moveaxis(q.reshape(B_, nb, Q_BLOCK, q.shape[2], q.shape[3], q.shape[4]), 1, 0)
    ob = lax.map(lambda blk: diff_attend(blk, k_all, v_all, lam), qb)
    o = jnp.moveaxis(ob, 0, 1).reshape(B_, L, ob.shape[3], ob.shape[4])
    return diff_finish(o, gain, lam_init)


def retention_scan(q, k, v, log_gamma, state0, with_out):
    B_, L, H, dk = q.shape
    dv = v.shape[-1]
    n = L // RET_CHUNK
    pos = jnp.arange(RET_CHUNK, dtype=jnp.float32)
    rel = pos[:, None] - pos[None, :]
    intra = jnp.where(rel[None] >= 0, jnp.exp(log_gamma[:, None, None] * jnp.maximum(rel, 0.0)[None]), 0.0)
    q_decay = jnp.exp(log_gamma[None, :] * (pos[:, None] + 1.0))
    k_decay = jnp.exp(log_gamma[None, :] * (RET_CHUNK - 1.0 - pos[:, None]))
    chunk_decay = jnp.exp(log_gamma * RET_CHUNK)[None, :, None, None]

    def chunks(t):
        return jnp.moveaxis(t.reshape(B_, n, RET_CHUNK, H, t.shape[-1]), 1, 0)

    def step(S, inp):
        qb, kb, vb = inp
        S_new = S * chunk_decay + jnp.einsum('bjhd,bjhe->bhde', kb * k_decay[None, :, :, None], vb)
        if not with_out:
            return S_new, None
        s = jnp.einsum('bihd,bjhd->bhij', qb, kb) * intra[None]
        o = (jnp.einsum('bhij,bjhe->bihe', s, vb)
             + jnp.einsum('bihd,bhde->bihe', qb, S) * q_decay[None, :, :, None])
        return S_new, o

    S_fin, o = lax.scan(step, state0, (chunks(q), chunks(k), chunks(v)))
    if with_out:
        o = jnp.moveaxis(o, 0, 1).reshape(B_, L, H, dv)
    return o, S_fin


def retention_mixer(q, k, v, g, qc, kc, vc, gc, log_decay, with_ctx):
    B_, L, H, dk = q.shape
    dv = v.shape[-1]
    f32 = lambda t: t.astype(jnp.float32)
    scale = dk ** -0.5
    qf, kf, vf = f32(q) * scale, f32(k), f32(v)
    qcf, kcf, vcf = f32(qc) * scale, f32(kc), f32(vc)
    state0 = jnp.zeros((B_, H, dk, dv), jnp.float32)
    outs, outs_c = [], []
    for direction in range(2):
        lg = log_decay[direction].astype(jnp.float32)
        if direction == 0:
            rev = lambda t: t
        else:
            rev = lambda t: jnp.flip(t, axis=1)
        oc, s_ctx = retention_scan(rev(qcf), rev(kcf), rev(vcf), lg, state0, with_ctx)
        ol, _ = retention_scan(rev(qf), rev(kf), rev(vf), lg, s_ctx, True)
        outs.append(rev(ol))
        if with_ctx:
            outs_c.append(rev(oc))
    y = rms_norm(outs[0] + outs[1]) * jax.nn.silu(f32(g).reshape(B_, L, H, dv))
    y = y.reshape(B_, L, H * dv).astype(g.dtype)
    yc = None
    if with_ctx:
        C = qc.shape[1]
        yc = rms_norm(outs_c[0] + outs_c[1]) * jax.nn.silu(f32(gc).reshape(B_, C, H, dv))
        yc = yc.reshape(B_, C, H * dv).astype(gc.dtype)
    return y, yc


def hier_moe(h, router_group, router_expert, w_gate, w_up, w_down):
    T, D = h.shape
    hf = h.astype(jnp.float32)
    g_prob = jax.nn.softmax(hf @ router_group.astype(jnp.float32), axis=-1)
    g_val, g_idx = lax.top_k(g_prob, 1)
    e_logits = (hf @ router_expert.astype(jnp.float32)).reshape(T, N_GROUPS, EXPERTS_PER_GROUP)
    sel = jnp.take_along_axis(e_logits, g_idx[:, :, None], axis=1)[:, 0]
    e_val, e_idx = lax.top_k(sel, TOP_K_IN_GROUP)
    weights = g_val * jax.nn.softmax(e_val, axis=-1)
    expert_id = g_idx * EXPERTS_PER_GROUP + e_idx
    A = T * TOP_K_IN_GROUP
    flat_e = expert_id.reshape(A)
    flat_tok = jnp.arange(A) // TOP_K_IN_GROUP
    flat_w = weights.reshape(A)
    order = jnp.argsort(flat_e, stable=True)
    se = flat_e[order]
    counts = jnp.bincount(flat_e, length=N_EXPERTS)
    starts = jnp.cumsum(counts) - counts
    pcounts = (counts + MOE_BLOCK - 1) // MOE_BLOCK * MOE_BLOCK
    pends = jnp.cumsum(pcounts)
    pstarts = pends - pcounts
    dest = pstarts[se] + (jnp.arange(A) - starts[se])
    NB = (A + N_EXPERTS * (MOE_BLOCK - 1)) // MOE_BLOCK
    P = NB * MOE_BLOCK
    row_tok = jnp.full((P,), T, jnp.int32).at[dest].set(flat_tok[order].astype(jnp.int32))
    row_w = jnp.zeros((P,), jnp.float32).at[dest].set(flat_w[order])
    block_e = jnp.clip(jnp.searchsorted(pends, jnp.arange(NB) * MOE_BLOCK, side='right'), 0, N_EXPERTS - 1)
    hpad = jnp.concatenate([h, jnp.zeros((1, D), h.dtype)], axis=0)

    def expert_block(args):
        tok, e = args
        xb = hpad[tok]
        a = xb @ w_gate[e]
        u = xb @ w_up[e]
        return (jax.nn.silu(a) * u) @ w_down[e]

    yb = lax.map(expert_block, (row_tok.reshape(NB, MOE_BLOCK), block_e))
    y = yb.reshape(P, D) * row_w[:, None].astype(h.dtype)
    out = jnp.zeros((T + 1, D), h.dtype).at[row_tok].add(y)
    return out[:T]


def trunk_layer(x, xc, c, c_ctx, layer_idx, with_ctx, rope_a, rope_c,
                w_mod, b_mod, norm_mix, norm_ffn, w_in, w_out,
                qk_norm_a, sink_a, qk_norm_b, rpb_b, qk_norm_c, lambda_c, subln_c,
                ret_log_decay, router_group, router_expert, w_gate, w_up, w_down):
    B_, L, D = x.shape
    C = xc.shape[1]
    mod = jax.nn.silu(c) @ w_mod + b_mod
    mod_c = jax.nn.silu(c_ctx) @ w_mod + b_mod
    sh1, sc1, g1, sh2, sc2, g2 = jnp.split(mod[:, None, :], 6, axis=-1)
    csh1, csc1, cg1, csh2, csc2, cg2 = jnp.split(mod_c, 6, axis=-1)

    h = modulate(x, norm_mix, sh1, sc1)
    hc = modulate(xc, norm_mix, csh1, csc1)
    (aq, ak, av, bq, bk, bv, cq, ck, cv, dq, dk, dv, dg) = jnp.split(h @ w_in, IN_OFFSETS, axis=-1)
    (aqc, akc, avc, bqc, bkc, bvc, cqc, ckc, cvc, dqc, dkc, dvc, dgc) = jnp.split(hc @ w_in, IN_OFFSETS, axis=-1)
    heads = lambda t, n: t.reshape(t.shape[0], t.shape[1], n, -1)
    subheads = lambda t: t.reshape(t.shape[0], t.shape[1], C_HEADS, 2, C_SUB_DIM)

    cos_a, sin_a = rope_a
    qa = apply_rope(rms_norm(heads(aq, A_HEADS), qk_norm_a[0]), cos_a[:, None], sin_a[:, None])
    ka = apply_rope(rms_norm(heads(ak, A_KV_HEADS), qk_norm_a[1]), cos_a[:, None], sin_a[:, None])
    kac = rms_norm(heads(akc, A_KV_HEADS), qk_norm_a[1])
    vac = heads(avc, A_KV_HEADS)
    out_a = window_gqa_latent(qa, ka, heads(av, A_KV_HEADS), kac, vac, sink_a)

    kbc = rms_norm(heads(bkc, B_HEADS), qk_norm_b[1])
    vbc = heads(bvc, B_HEADS)
    out_b = neighbourhood_latent(rms_norm(heads(bq, B_HEADS), qk_norm_b[0]),
                                 rms_norm(heads(bk, B_HEADS), qk_norm_b[1]),
                                 heads(bv, B_HEADS), kbc, vbc, rpb_b)

    lam_init = 0.8 - 0.6 * math.exp(-0.3 * layer_idx)
    lam = diff_lambda(lambda_c, lam_init)
    cos_c, sin_c = rope_c
    qcl = apply_rope(rms_norm(subheads(cq), qk_norm_c[0]), cos_c[:, None, None], sin_c[:, None, None])
    kcl = apply_rope(rms_norm(subheads(ck), qk_norm_c[1]), cos_c[:, None, None], sin_c[:, None, None])
    kcc = rms_norm(subheads(ckc), qk_norm_c[1])
    vcc = heads(cvc, C_HEADS)
    out_c = diff_attention_latent(qcl, kcl, heads(cv, C_HEADS), kcc, vcc, lam, subln_c, lam_init)

    out_d, out_dc = retention_mixer(heads(dq, D_HEADS), heads(dk, D_HEADS), heads(dv, D_HEADS), dg,
                                    heads(dqc, D_HEADS), heads(dkc, D_HEADS), heads(dvc, D_HEADS), dgc,
                                    ret_log_decay, with_ctx)

    x = x + g1 * (jnp.concatenate([out_a, out_b, out_c, out_d], axis=-1) @ w_out)
    if with_ctx:
        out_ac = ctx_attention(rms_norm(heads(aqc, A_HEADS), qk_norm_a[0]), kac, vac, sink_a)
        out_bc = ctx_attention(rms_norm(heads(bqc, B_HEADS), qk_norm_b[0]), kbc, vbc, None)
        out_cc = diff_finish(diff_attend(rms_norm(subheads(cqc), qk_norm_c[0]), kcc, vcc, lam), subln_c, lam_init)
        xc = xc + cg1 * (jnp.concatenate([out_ac, out_bc, out_cc, out_dc], axis=-1) @ w_out)

    h2 = modulate(x, norm_ffn, sh2, sc2).reshape(B_ * L, D)
    if with_ctx:
        h2c = modulate(xc, norm_ffn, csh2, csc2).reshape(B_ * C, D)
        y2 = hier_moe(jnp.concatenate([h2, h2c], axis=0), router_group, router_expert, w_gate, w_up, w_down)
        x = x + g2 * y2[:B_ * L].reshape(B_, L, D)
        xc = xc + cg2 * y2[B_ * L:].reshape(B_, C, D)
    else:
        x = x + g2 * hier_moe(h2, router_group, router_expert, w_gate, w_up, w_down).reshape(B_, L, D)
    return x, xc


def setup_inputs(seed: int = 0) -> dict:
    key = jax.random.key(seed)
    ks = jax.random.split(key, 24)
    D = D_MODEL
    nrm = lambda k, shape, s: jax.random.normal(k, shape, jnp.float32) * s
    base_decay = jnp.asarray(np.log(1.0 - 2.0 ** (-5.0 - np.arange(D_HEADS))).astype(np.float32))
    return {
        'x': nrm(ks[0], (BATCH, SEQ, D), 1.0),
        'c': nrm(ks[1], (BATCH, D), 1.0),
        'ctx': nrm(ks[2], (BATCH, CTX_LEN, D), 1.0),
        'c_ctx': nrm(ks[3], (D,), 1.0),
        'w_mod': nrm(ks[4], (DEPTH, D, 6 * D), 0.5 * D ** -0.5),
        'b_mod': nrm(ks[5], (DEPTH, 6 * D), 0.02),
        'norm_mix': 1.0 + nrm(ks[6], (DEPTH, D), 0.02),
        'norm_ffn': 1.0 + nrm(ks[7], (DEPTH, D), 0.02),
        'w_in': nrm(ks[8], (DEPTH, D, IN_COLS), D ** -0.5),
        'w_out': nrm(ks[9], (DEPTH, MIX_WIDTH, D), MIX_WIDTH ** -0.5),
        'qk_norm_a': 1.0 + nrm(ks[10], (DEPTH, 2, HEAD_DIM), 0.02),
        'sink_a': nrm(ks[11], (DEPTH, A_HEADS), 0.5),
        'qk_norm_b': 1.0 + nrm(ks[12], (DEPTH, 2, HEAD_DIM), 0.02),
        'rpb_b': nrm(ks[13], (DEPTH, B_HEADS, 2 * NA_ROWS - 1, 2 * NA_COLS - 1), 0.1),
        'qk_norm_c': 1.0 + nrm(ks[14], (DEPTH, 2, C_SUB_DIM), 0.02),
        'lambda_c': nrm(ks[15], (DEPTH, 4, C_SUB_DIM), 0.1),
        'subln_c': 1.0 + nrm(ks[16], (DEPTH, HEAD_DIM), 0.02),
        'ret_log_decay': base_decay[None, None, :] * (1.0 + nrm(ks[17], (DEPTH, 2, D_HEADS), 0.05)),
        'router_group': nrm(ks[18], (DEPTH, D, N_GROUPS), D ** -0.5),
        'router_expert': nrm(ks[19], (DEPTH, D, N_EXPERTS), D ** -0.5),
        'w_gate': nrm(ks[20], (DEPTH, N_EXPERTS, D, D_EXPERT), D ** -0.5),
        'w_up': nrm(ks[21], (DEPTH, N_EXPERTS, D, D_EXPERT), D ** -0.5),
        'w_down': nrm(ks[22], (DEPTH, N_EXPERTS, D_EXPERT, D), D_EXPERT ** -0.5),
    }


def reference(x, c, ctx, c_ctx, w_mod, b_mod, norm_mix, norm_ffn, w_in, w_out,
              qk_norm_a, sink_a, qk_norm_b, rpb_b, qk_norm_c, lambda_c, subln_c,
              ret_log_decay, router_group, router_expert, w_gate, w_up, w_down):
    L = x.shape[1]
    rope_a = rope_tables(L, HEAD_DIM)
    rope_c = rope_tables(L, C_SUB_DIM)
    xc = ctx
    for l in range(DEPTH):
        x, xc = trunk_layer(x, xc, c, c_ctx, l, l < DEPTH - 1, rope_a, rope_c,
                            w_mod[l], b_mod[l], norm_mix[l], norm_ffn[l], w_in[l], w_out[l],
                            qk_norm_a[l], sink_a[l], qk_norm_b[l], rpb_b[l], qk_norm_c[l], lambda_c[l],
                            subln_c[l], ret_log_decay[l], router_group[l], router_expert[l],
                            w_gate[l], w_up[l], w_down[l])
    return x
```

```python
import functools
import math

import jax
import jax.numpy as jnp
import numpy as np
from jax import lax
from jax.experimental import pallas as pl
from jax.experimental.pallas import tpu as pltpu

F32 = jnp.float32
BF16 = jnp.bfloat16

HEAD_DIM = 128
GRID_W = 64
A_HEADS, A_KV_HEADS, A_WINDOW = 8, 2, 128
B_HEADS, NA_ROWS, NA_COLS = 8, 8, 16
C_HEADS, C_SUB_DIM = 8, 64
D_HEADS, RET_CHUNK = 8, 128
ROPE_BASE = 10000.0
N_GROUPS, EXPERTS_PER_GROUP, TOP_K = 4, 8, 2
N_EXPERTS = N_GROUPS * EXPERTS_PER_GROUP
EPS = 1e-6
NEG_INF = -1e30

A_Q, A_K, A_V = 0, 8, 10
B_Q, B_K, B_V = 12, 20, 28
C_Q, C_K, C_V = 36, 44, 52
D_Q, D_K, D_V, D_G = 60, 68, 76, 84
ABC_BLOCKS = 60
IN_COLS = 92 * 128

VMEM_LIMIT = 56 * 1024 * 1024


def _cparams(sem, vmem=VMEM_LIMIT):
    return pltpu.CompilerParams(dimension_semantics=sem, vmem_limit_bytes=vmem)


def _dot(a, b):
    return jnp.dot(a, b, preferred_element_type=F32)


def _dot_nt(a, b):
    return lax.dot_general(a, b, (((1,), (1,)), ((), ())), preferred_element_type=F32)


def _split_bf16(x):
    hi = x.astype(BF16)
    lo = (x - hi.astype(F32)).astype(BF16)
    return hi, lo


def _dot3(x, w):
    xh, xl = _split_bf16(x)
    wh, wl = _split_bf16(w)
    return _dot(xh, wh) + _dot(xl, wh) + _dot(xh, wl)


def _mod_kernel(c_ref, w_ref, b_ref, o_ref):
    c = c_ref[...]
    x = c * jax.nn.sigmoid(c)
    o_ref[0] = _dot3(x, w_ref[0]) + b_ref[0]


def mod_vectors(cvec, w_mod, b_mod, tn=512):
    depth, D, N = w_mod.shape
    return pl.pallas_call(
        _mod_kernel,
        out_shape=jax.ShapeDtypeStruct((depth, 8, N), F32),
        grid=(depth, N // tn),
        in_specs=[
            pl.BlockSpec((8, D), lambda l, j: (0, 0)),
            pl.BlockSpec((1, D, tn), lambda l, j: (l, 0, j)),
            pl.BlockSpec((1, 1, tn), lambda l, j: (l, 0, j)),
        ],
        out_specs=pl.BlockSpec((1, 8, tn), lambda l, j: (l, 0, j)),
        compiler_params=_cparams(("arbitrary", "arbitrary")),
        name="mod_vectors",
    )(cvec, w_mod, b_mod.reshape(depth, 1, N))


def _modulated(x, gain, shift, scale):
    ms = jnp.mean(x * x, axis=-1, keepdims=True)
    y = x * lax.rsqrt(ms + EPS) * gain
    return y * (1.0 + scale) + shift


def _in_proj_kernel(x_ref, gain_ref, shift_ref, scale_ref, w_ref, o_ref, h_ref):
    @pl.when(pl.program_id(2) == 0)
    def _():
        h_ref[...] = _modulated(x_ref[0], gain_ref[...], shift_ref[0], scale_ref[0]).astype(BF16)

    o_ref[0] = _dot(h_ref[...], w_ref[...]).astype(o_ref.dtype)


def in_proj(x, gain, shift, scale, w, tm, tn=512):
    B, L, D = x.shape
    N = w.shape[1]
    return pl.pallas_call(
        _in_proj_kernel,
        out_shape=jax.ShapeDtypeStruct((B, L, N), BF16),
        grid=(B, L // tm, N // tn),
        in_specs=[
            pl.BlockSpec((1, tm, D), lambda b, i, j: (b, i, 0)),
            pl.BlockSpec((1, D), lambda b, i, j: (0, 0)),
            pl.BlockSpec((1, 1, D), lambda b, i, j: (b, 0, 0)),
            pl.BlockSpec((1, 1, D), lambda b, i, j: (b, 0, 0)),
            pl.BlockSpec((D, tn), lambda b, i, j: (0, j)),
        ],
        out_specs=pl.BlockSpec((1, tm, tn), lambda b, i, j: (b, i, j)),
        scratch_shapes=[pltpu.VMEM((tm, D), BF16)],
        compiler_params=_cparams(("arbitrary", "arbitrary", "arbitrary")),
        name="in_proj",
    )(x, gain, shift, scale, w)


def _rope_tables(L, dim):
    t = jnp.arange(L)
    row = (t // GRID_W).astype(F32)
    col = (t % GRID_W).astype(F32)
    quarter = dim // 4
    inv = ROPE_BASE ** (-jnp.arange(quarter, dtype=F32) / quarter)
    ar = row[:, None] * inv[None, :]
    ac = col[:, None] * inv[None, :]
    ang = jnp.concatenate([ar, ar, ac, ac], axis=-1)
    sign = jnp.where((jnp.arange(dim) % (dim // 2)) < quarter, -1.0, 1.0).astype(F32)
    reps = HEAD_DIM // dim
    return jnp.tile(jnp.cos(ang), (1, reps)), jnp.tile(jnp.sin(ang) * sign[None, :], (1, reps))


def _qk_prep_kernel(p_ref, cosa_ref, sina_ref, cosc_ref, sinc_ref, ga_ref, gb_ref, gc_ref, o_ref):
    t = p_ref.shape[1]
    lane = lax.broadcasted_iota(jnp.int32, (t, HEAD_DIM), 1)
    low_half = lane < C_SUB_DIM

    def norm_full(x):
        return x * lax.rsqrt(jnp.mean(x * x, axis=-1, keepdims=True) + EPS)

    def norm_halves(x):
        sq = x * x
        s_lo = jnp.sum(jnp.where(low_half, sq, 0.0), axis=-1, keepdims=True)
        s_hi = jnp.sum(jnp.where(low_half, 0.0, sq), axis=-1, keepdims=True)
        ms = jnp.where(low_half, s_lo, s_hi) * (1.0 / C_SUB_DIM)
        return x * lax.rsqrt(ms + EPS)

    def rope(x, cos, sin, dim):
        quarter = dim // 4
        first = (lane % (dim // 2)) < quarter
        rot = jnp.where(first, pltpu.roll(x, HEAD_DIM - quarter, 1), pltpu.roll(x, quarter, 1))
        return x * cos + rot * sin

    cosa, sina = cosa_ref[...], sina_ref[...]
    cosc, sinc = cosc_ref[...], sinc_ref[...]
    scale_ab = HEAD_DIM ** -0.5
    scale_c = C_SUB_DIM ** -0.5
    for blk in range(ABC_BLOCKS):
        sl = slice(blk * HEAD_DIM, (blk + 1) * HEAD_DIM)
        if A_V <= blk < B_Q or B_V <= blk < C_Q or C_V <= blk:
            o_ref[0, :, sl] = p_ref[0, :, sl]
            continue
        x = p_ref[0, :, sl].astype(F32)
        if blk < A_K:
            y = rope(norm_full(x) * ga_ref[0:1, :], cosa, sina, HEAD_DIM) * scale_ab
        elif blk < A_V:
            y = rope(norm_full(x) * ga_ref[1:2, :], cosa, sina, HEAD_DIM)
        elif blk < B_K:
            y = norm_full(x) * gb_ref[0:1, :] * scale_ab
        elif blk < B_V:
            y = norm_full(x) * gb_ref[1:2, :]
        elif blk < C_K:
            y = rope(norm_halves(x) * gc_ref[0:1, :], cosc, sinc, C_SUB_DIM) * scale_c
        else:
            y = rope(norm_halves(x) * gc_ref[1:2, :], cosc, sinc, C_SUB_DIM)
        o_ref[0, :, sl] = y.astype(o_ref.dtype)


def qk_prep(proj, cosa, sina, cosc, sinc, ga, gb, gc, t=256):
    B, L, _ = proj.shape
    W = ABC_BLOCKS * HEAD_DIM
    tab = pl.BlockSpec((t, HEAD_DIM), lambda b, i: (i, 0))
    par = pl.BlockSpec((2, HEAD_DIM), lambda b, i: (0, 0))
    return pl.pallas_call(
        _qk_prep_kernel,
        out_shape=jax.ShapeDtypeStruct((B, L, W), BF16),
        grid=(B, L // t),
        in_specs=[pl.BlockSpec((1, t, W), lambda b, i: (b, i, 0)), tab, tab, tab, tab, par, par, par],
        out_specs=pl.BlockSpec((1, t, W), lambda b, i: (b, i, 0)),
        compiler_params=_cparams(("arbitrary", "arbitrary")),
        name="qk_prep",
    )(proj, cosa, sina, cosc, sinc, ga, gb, gc)


def _attn_a_kernel(sink_ref, q_ref, k_ref, v_ref, kc_ref, vc_ref, o_ref, *, tq):
    L = k_ref.shape[1]
    kv, i = pl.program_id(1), pl.program_id(2)
    win = tq + 2 * A_WINDOW
    start = pl.multiple_of(jnp.clip(i * tq - A_WINDOW, 0, L - win), A_WINDOW)
    k = k_ref[0, pl.ds(start, win), :]
    v = v_ref[0, pl.ds(start, win), :]
    kc, vc = kc_ref[0], vc_ref[0]
    qpos = i * tq + lax.broadcasted_iota(jnp.int32, (tq, win), 0)
    kpos = start + lax.broadcasted_iota(jnp.int32, (tq, win), 1)
    in_window = jnp.abs(kpos - qpos) <= A_WINDOW
    group = A_HEADS // A_KV_HEADS
    for g in range(group):
        sl = slice(g * HEAD_DIM, (g + 1) * HEAD_DIM)
        q = q_ref[0, :, sl]
        s = jnp.where(in_window, _dot_nt(q, k), NEG_INF)
        sc = _dot_nt(q, kc)
        sink = sink_ref[kv * group + g]
        m = jnp.maximum(jnp.maximum(jnp.max(s, axis=-1, keepdims=True), jnp.max(sc, axis=-1, keepdims=True)), sink)
        p = jnp.exp(s - m)
        pc = jnp.exp(sc - m)
        denom = jnp.sum(p, axis=-1, keepdims=True) + jnp.sum(pc, axis=-1, keepdims=True) + jnp.exp(sink - m)
        o = _dot(p.astype(BF16), v) + _dot(pc.astype(BF16), vc)
        o_ref[0, :, sl] = (o / denom).astype(o_ref.dtype)


def attn_a(qk, qkc, sink, tq=512):
    B, L, _ = qk.shape
    C = qkc.shape[1]
    gw = (A_HEADS // A_KV_HEADS) * HEAD_DIM
    return pl.pallas_call(
        functools.partial(_attn_a_kernel, tq=tq),
        out_shape=jax.ShapeDtypeStruct((B, L, A_HEADS * HEAD_DIM), BF16),
        grid=(B, A_KV_HEADS, L // tq),
        in_specs=[
            pl.BlockSpec(memory_space=pltpu.SMEM),
            pl.BlockSpec((1, tq, gw), lambda b, kv, i: (b, i, kv)),
            pl.BlockSpec((1, L, HEAD_DIM), lambda b, kv, i: (b, 0, A_K + kv)),
            pl.BlockSpec((1, L, HEAD_DIM), lambda b, kv, i: (b, 0, A_V + kv)),
            pl.BlockSpec((1, C, HEAD_DIM), lambda b, kv, i: (b, 0, A_K + kv)),
            pl.BlockSpec((1, C, HEAD_DIM), lambda b, kv, i: (b, 0, A_V + kv)),
        ],
        out_specs=pl.BlockSpec((1, tq, gw), lambda b, kv, i: (b, i, kv)),
        compiler_params=_cparams(("arbitrary", "arbitrary", "arbitrary")),
        name="attn_a",
    )(sink, qk, qk, qk, qkc, qkc)


NA_QROWS = 8


def _na_bias_tables(rpb, R):
    W = GRID_W
    H = rpb.shape[0]
    wq = jnp.arange(W)
    cs = jnp.clip(wq - NA_COLS // 2, 0, W - NA_COLS)
    colmask = (wq[None, :] >= cs[:, None]) & (wq[None, :] < cs[:, None] + NA_COLS)
    rel_c = jnp.clip(wq[None, :] - wq[:, None] + (NA_COLS - 1), 0, 2 * NA_COLS - 2)
    per_row = jnp.where(colmask[None, None], rpb.astype(F32)[:, :, rel_c], NEG_INF)
    nq, nk = NA_QROWS, 2 * NA_QROWS
    tables = []
    for r0, ks in ((0, 0), (nq, nq - NA_ROWS // 2), (R - nq, R - nk)):
        qr = r0 + jnp.arange(nq)
        kr = ks + jnp.arange(nk)
        rs = jnp.clip(qr - NA_ROWS // 2, 0, R - NA_ROWS)
        valid = (kr[None, :] >= rs[:, None]) & (kr[None, :] < rs[:, None] + NA_ROWS)
        dr = jnp.clip(kr[None, :] - qr[:, None] + (NA_ROWS - 1), 0, 2 * NA_ROWS - 2)
        t = jnp.where(valid[None, :, :, None, None], per_row[:, dr], NEG_INF)
        tables.append(jnp.transpose(t, (0, 1, 3, 2, 4)).reshape(H, nq * W, nk * W))
    return jnp.stack(tables)


def _attn_b_kernel(q_ref, k_ref, v_ref, kc_ref, vc_ref, bias_ref, o_ref):
    L = k_ref.shape[1]
    R = L // GRID_W
    i = pl.program_id(2)
    nk = 2 * NA_QROWS
    krow = jnp.clip(i * NA_QROWS - NA_ROWS // 2, 0, R - nk)
    start = pl.multiple_of(krow * GRID_W, (NA_ROWS // 2) * GRID_W)
    k = k_ref[0, pl.ds(start, nk * GRID_W), :]
    v = v_ref[0, pl.ds(start, nk * GRID_W), :]
    q = q_ref[0]
    s = _dot_nt(q, k) + bias_ref[0, 0]
    sc = _dot_nt(q, kc_ref[0])
    m = jnp.maximum(jnp.max(s, axis=-1, keepdims=True), jnp.max(sc, axis=-1, keepdims=True))
    p = jnp.exp(s - m)
    pc = jnp.exp(sc - m)
    denom = jnp.sum(p, axis=-1, keepdims=True) + jnp.sum(pc, axis=-1, keepdims=True)
    o = _dot(p.astype(BF16), v) + _dot(pc.astype(BF16), vc_ref[0])
    o_ref[0] = (o / denom).astype(o_ref.dtype)


def attn_b(qk, qkc, bias):
    B, L, _ = qk.shape
    C = qkc.shape[1]
    tq = NA_QROWS * GRID_W
    nblk = L // tq

    def bias_idx(h, b, i):
        return (jnp.where(i == 0, 0, jnp.where(i == nblk - 1, 2, 1)), h, 0, 0)

    return pl.pallas_call(
        _attn_b_kernel,
        out_shape=jax.ShapeDtypeStruct((B, L, B_HEADS * HEAD_DIM), BF16),
        grid=(B_HEADS, B, nblk),
        in_specs=[
            pl.BlockSpec((1, tq, HEAD_DIM), lambda h, b, i: (b, i, B_Q + h)),
            pl.BlockSpec((1, L, HEAD_DIM), lambda h, b, i: (b, 0, B_K + h)),
            pl.BlockSpec((1, L, HEAD_DIM), lambda h, b, i: (b, 0, B_V + h)),
            pl.BlockSpec((1, C, HEAD_DIM), lambda h, b, i: (b, 0, B_K + h)),
            pl.BlockSpec((1, C, HEAD_DIM), lambda h, b, i: (b, 0, B_V + h)),
            pl.BlockSpec((1, 1, tq, 2 * tq), bias_idx),
        ],
        out_specs=pl.BlockSpec((1, tq, HEAD_DIM), lambda h, b, i: (b, i, h)),
        compiler_params=_cparams(("arbitrary", "arbitrary", "arbitrary")),
        name="attn_b",
    )(qk, qk, qk, qkc, qkc, bias)


def _diff_lambda(lambda_params, lam_init):
    lp = lambda_params.astype(F32)
    return jnp.exp(jnp.sum(lp[0] * lp[1])) - jnp.exp(jnp.sum(lp[2] * lp[3])) + lam_init


def _attn_c_kernel(lam_ref, q_ref, k_ref, vt_ref, kc_ref, vct_ref, g_ref, o_ref, m_ref, l_ref, acc_ref, *, tk, coef):
    tq = q_ref.shape[1]
    nk = vt_ref.shape[2]
    q = q_ref[0]
    lane = lax.broadcasted_iota(jnp.int32, q.shape, 1)
    zero = jnp.zeros_like(q)
    qs = (jnp.where(lane < C_SUB_DIM, q, zero), jnp.where(lane < C_SUB_DIM, zero, q))

    m_ref[...] = jnp.full(m_ref.shape, NEG_INF, F32)
    l_ref[...] = jnp.zeros(l_ref.shape, F32)
    acc_ref[...] = jnp.zeros(acc_ref.shape, F32)

    def accumulate(kblk, vtblk):
        for s_idx in range(2):
            s = _dot_nt(kblk, qs[s_idx])
            m_old = m_ref[s_idx]
            m_new = jnp.maximum(m_old, jnp.max(s, axis=0, keepdims=True))
            alpha = jnp.exp(m_old - m_new)
            p = jnp.exp(s - m_new)
            l_ref[s_idx] = alpha * l_ref[s_idx] + jnp.sum(p, axis=0, keepdims=True)
            acc_ref[s_idx] = alpha * acc_ref[s_idx] + _dot(vtblk, p.astype(BF16))
            m_ref[s_idx] = m_new

    def body(j, carry):
        accumulate(k_ref[0, pl.ds(pl.multiple_of(j * tk, tk), tk), :], vt_ref[0, 0, j])
        return carry

    lax.fori_loop(0, nk, body, 0)
    accumulate(kc_ref[0], vct_ref[0, 0, 0])

    o = acc_ref[0] / l_ref[0] - lam_ref[0] * (acc_ref[1] / l_ref[1])
    y = o * lax.rsqrt(jnp.mean(o * o, axis=0, keepdims=True) + EPS) * g_ref[...] * coef
    o_ref[0] = y.T.astype(o_ref.dtype)


def attn_c(qk, vt, qkc, vct, lam, subln, coef, tq=512):
    B, L, _ = qk.shape
    C = qkc.shape[1]
    _, H, nk, dv, tk = vt.shape
    return pl.pallas_call(
        functools.partial(_attn_c_kernel, tk=tk, coef=coef),
        out_shape=jax.ShapeDtypeStruct((B, L, H * dv), BF16),
        grid=(B, H, L // tq),
        in_specs=[
            pl.BlockSpec(memory_space=pltpu.SMEM),
            pl.BlockSpec((1, tq, HEAD_DIM), lambda b, h, i: (b, i, C_Q + h)),
            pl.BlockSpec((1, L, HEAD_DIM), lambda b, h, i: (b, 0, C_K + h)),
            pl.BlockSpec((1, 1, nk, dv, tk), lambda b, h, i: (b, h, 0, 0, 0)),
            pl.BlockSpec((1, C, HEAD_DIM), lambda b, h, i: (b, 0, C_K + h)),
            pl.BlockSpec((1, 1, 1, dv, C), lambda b, h, i: (b, h, 0, 0, 0)),
            pl.BlockSpec((dv, 1), lambda b, h, i: (0, 0)),
        ],
        out_specs=pl.BlockSpec((1, tq, dv), lambda b, h, i: (b, i, h)),
        scratch_shapes=[
            pltpu.VMEM((2, 1, tq), F32),
            pltpu.VMEM((2, 1, tq), F32),
            pltpu.VMEM((2, dv, tq), F32),
        ],
        compiler_params=_cparams(("arbitrary", "arbitrary", "arbitrary")),
        name="attn_c",
    )(lam, qk, qk, vt, qkc, vct, subln.reshape(dv, 1).astype(F32))


def _retention_kernel(ld_ref, q_ref, k_ref, v_ref, g_ref, qc_ref, kc_ref, vc_ref, gc_ref, *rest, with_ctx):
    if with_ctx:
        y_ref, yc_ref, of_ref, ocf_ref = rest
    else:
        y_ref, of_ref = rest
        yc_ref = ocf_ref = None
    c = RET_CHUNK
    L, C = q_ref.shape[1], qc_ref.shape[1]
    h = pl.program_id(1)
    scale = HEAD_DIM ** -0.5
    row = lax.broadcasted_iota(jnp.int32, (c, c), 0).astype(F32)
    col = lax.broadcasted_iota(jnp.int32, (c, c), 1).astype(F32)
    rowp = lax.broadcasted_iota(jnp.int32, (c, HEAD_DIM), 0).astype(F32)

    def decays(lg, reverse):
        if reverse:
            rel, q_exp, k_exp = col - row, c - rowp, rowp
        else:
            rel, q_exp, k_exp = row - col, rowp + 1.0, (c - 1.0) - rowp
        intra = jnp.where(rel >= 0, jnp.exp(lg * jnp.maximum(rel, 0.0)), 0.0)
        return intra, jnp.exp(lg * q_exp), jnp.exp(lg * k_exp), jnp.exp(lg * c)

    def step(S, q, k, v, dec, want_out):
        intra, q_decay, k_decay, chunk_decay = dec
        o = None
        if want_out:
            qs = (q.astype(F32) * scale).astype(BF16)
            s = _dot_nt(qs, k) * intra
            o = _dot(s.astype(BF16), v) + _dot(qs, S.astype(BF16)) * q_decay
        kd_t = (k.astype(F32) * k_decay).T.astype(BF16)
        return S * chunk_decay + _dot(kd_t, v), o

    def gated(o, g):
        g = g.astype(F32)
        y = o * lax.rsqrt(jnp.mean(o * o, axis=-1, keepdims=True) + EPS)
        return y * (g * jax.nn.sigmoid(g))

    def lat(ref, j):
        return ref[0, pl.ds(pl.multiple_of(j * c, c), c), :]

    def ctx(ref, j):
        return ref[0, j * c:(j + 1) * c, :]

    dec = decays(ld_ref[0, h], False)
    S = jnp.zeros((HEAD_DIM, HEAD_DIM), F32)
    for j in range(C // c):
        S, o = step(S, ctx(qc_ref, j), ctx(kc_ref, j), ctx(vc_ref, j), dec, with_ctx)
        if with_ctx:
            ocf_ref[j * c:(j + 1) * c, :] = o

    def fwd(j, S):
        S, o = step(S, lat(q_ref, j), lat(k_ref, j), lat(v_ref, j), dec, True)
        of_ref[pl.ds(pl.multiple_of(j * c, c), c), :] = o
        return S

    lax.fori_loop(0, L // c, fwd, S)

    dec = decays(ld_ref[1, h], True)
    S = jnp.zeros((HEAD_DIM, HEAD_DIM), F32)
    for j in reversed(range(C // c)):
        S, o = step(S, ctx(qc_ref, j), ctx(kc_ref, j), ctx(vc_ref, j), dec, with_ctx)
        if with_ctx:
            yc_ref[0, j * c:(j + 1) * c, :] = gated(o + ocf_ref[j * c:(j + 1) * c, :], ctx(gc_ref, j)).astype(yc_ref.dtype)

    def bwd(t, S):
        j = L // c - 1 - t
        S, o = step(S, lat(q_ref, j), lat(k_ref, j), lat(v_ref, j), dec, True)
        sl = pl.ds(pl.multiple_of(j * c, c), c)
        y_ref[0, sl, :] = gated(o + of_ref[sl, :], lat(g_ref, j)).astype(y_ref.dtype)
        return S

    lax.fori_loop(0, L // c, bwd, S)


def retention(proj, projc, log_decay, with_ctx):
    B, L, _ = proj.shape
    C = projc.shape[1]

    def lat(blk):
        return pl.BlockSpec((1, L, HEAD_DIM), lambda b, h: (b, 0, blk + h))

    def ctx(blk):
        return pl.BlockSpec((1, C, HEAD_DIM), lambda b, h: (b, 0, blk + h))

    out_shape = [jax.ShapeDtypeStruct((B, L, D_HEADS * HEAD_DIM), BF16)]
    out_specs = [pl.BlockSpec((1, L, HEAD_DIM), lambda b, h: (b, 0, h))]
    scratch = [pltpu.VMEM((L, HEAD_DIM), F32)]
    if with_ctx:
        out_shape.append(jax.ShapeDtypeStruct((B, C, D_HEADS * HEAD_DIM), BF16))
        out_specs.append(pl.BlockSpec((1, C, HEAD_DIM), lambda b, h: (b, 0, h)))
        scratch.append(pltpu.VMEM((C, HEAD_DIM), F32))
    outs = pl.pallas_call(
        functools.partial(_retention_kernel, with_ctx=with_ctx),
        out_shape=out_shape,
        grid=(B, D_HEADS),
        in_specs=[pl.BlockSpec(memory_space=pltpu.SMEM),
                  lat(D_Q), lat(D_K), lat(D_V), lat(D_G), ctx(D_Q), ctx(D_K), ctx(D_V), ctx(D_G)],
        out_specs=out_specs,
        scratch_shapes=scratch,
        compiler_params=_cparams(("arbitrary", "arbitrary")),
        name="retention_ctx" if with_ctx else "retention",
    )(log_decay.astype(F32), proj, proj, proj, proj, projc, projc, projc, projc)
    return (outs[0], outs[1]) if with_ctx else (outs[0], None)


def _ctx_attn_kernel(sink_ref, lam_ref, qk_ref, g_ref, o_ref, *, coef):
    def blk(idx):
        return qk_ref[0, :, idx * HEAD_DIM:(idx + 1) * HEAD_DIM]

    def softmax_out(q, k, v, sink=None):
        s = _dot_nt(q, k)
        m = jnp.max(s, axis=-1, keepdims=True)
        if sink is not None:
            m = jnp.maximum(m, sink)
        p = jnp.exp(s - m)
        denom = jnp.sum(p, axis=-1, keepdims=True)
        if sink is not None:
            denom = denom + jnp.exp(sink - m)
        return _dot(p.astype(BF16), v) / denom

    group = A_HEADS // A_KV_HEADS
    for h in range(A_HEADS):
        o = softmax_out(blk(A_Q + h), blk(A_K + h // group), blk(A_V + h // group), sink_ref[h])
        o_ref[0, :, h * HEAD_DIM:(h + 1) * HEAD_DIM] = o.astype(o_ref.dtype)
    for h in range(B_HEADS):
        o = softmax_out(blk(B_Q + h), blk(B_K + h), blk(B_V + h))
        o_ref[0, :, (A_HEADS + h) * HEAD_DIM:(A_HEADS + h + 1) * HEAD_DIM] = o.astype(o_ref.dtype)
    lane = lax.broadcasted_iota(jnp.int32, (qk_ref.shape[1], HEAD_DIM), 1)
    for h in range(C_HEADS):
        q, k, v = blk(C_Q + h), blk(C_K + h), blk(C_V + h)
        zero = jnp.zeros_like(q)
        o1 = softmax_out(jnp.where(lane < C_SUB_DIM, q, zero), k, v)
        o2 = softmax_out(jnp.where(lane < C_SUB_DIM, zero, q), k, v)
        o = o1 - lam_ref[0] * o2
        y = o * lax.rsqrt(jnp.mean(o * o, axis=-1, keepdims=True) + EPS) * g_ref[...] * coef
        base = A_HEADS + B_HEADS + h
        o_ref[0, :, base * HEAD_DIM:(base + 1) * HEAD_DIM] = y.astype(o_ref.dtype)


def ctx_attn(qkc, sink, lam, subln, coef):
    B, C, W = qkc.shape
    n_out = (A_HEADS + B_HEADS + C_HEADS) * HEAD_DIM
    return pl.pallas_call(
        functools.partial(_ctx_attn_kernel, coef=coef),
        out_shape=jax.ShapeDtypeStruct((B, C, n_out), BF16),
        grid=(B,),
        in_specs=[
            pl.BlockSpec(memory_space=pltpu.SMEM),
            pl.BlockSpec(memory_space=pltpu.SMEM),
            pl.BlockSpec((1, C, W), lambda b: (b, 0, 0)),
            pl.BlockSpec((1, HEAD_DIM), lambda b: (0, 0)),
        ],
        out_specs=pl.BlockSpec((1, C, n_out), lambda b: (b, 0, 0)),
        compiler_params=_cparams(("arbitrary",)),
        name="ctx_attn",
    )(sink, lam, qkc, subln.reshape(1, HEAD_DIM).astype(F32))


def _out_proj_kernel(*refs, n_parts):
    parts = refs[:n_parts]
    x_ref, gate_ref, w_ref, o_ref = refs[n_parts:]
    mix = jnp.concatenate([p[0] for p in parts], axis=-1)
    o_ref[0] = x_ref[0] + gate_ref[0] * _dot(mix, w_ref[...])


def out_proj(parts, x, gate, w, tm, tn=512):
    B, L, D = x.shape
    part_specs = [pl.BlockSpec((1, tm, p.shape[2]), lambda b, i, j: (b, i, 0)) for p in parts]
    return pl.pallas_call(
        functools.partial(_out_proj_kernel, n_parts=len(parts)),
        out_shape=jax.ShapeDtypeStruct((B, L, D), F32),
        grid=(B, L // tm, D // tn),
        in_specs=part_specs + [
            pl.BlockSpec((1, tm, tn), lambda b, i, j: (b, i, j)),
            pl.BlockSpec((1, 1, tn), lambda b, i, j: (b, 0, j)),
            pl.BlockSpec((w.shape[0], tn), lambda b, i, j: (0, j)),
        ],
        out_specs=pl.BlockSpec((1, tm, tn), lambda b, i, j: (b, i, j)),
        compiler_params=_cparams(("arbitrary", "arbitrary", "arbitrary")),
        name="out_proj",
    )(*parts, x, gate, w)


def _ffn_prep_kernel(*refs, n_lat_tiles):
    if n_lat_tiles is None:
        x_ref, gain_ref, shift_ref, scale_ref, r_ref, h_ref, id_ref, w_ref = refs
        x = x_ref[...]
    else:
        xl_ref, xc_ref, gain_ref, shift_ref, scale_ref, r_ref, h_ref, id_ref, w_ref = refs
        x = jnp.where(pl.program_id(0) < n_lat_tiles, xl_ref[...], xc_ref[...])
    h = _modulated(x, gain_ref[...], shift_ref[0], scale_ref[0])
    h_ref[...] = h

    logits = _dot3(h, r_ref[...])
    lane = lax.broadcasted_iota(jnp.int32, logits.shape, 1)
    big = jnp.int32(logits.shape[1])

    def first_argmax(vals):
        top = jnp.max(vals, axis=-1, keepdims=True)
        return top, jnp.min(jnp.where(vals == top, lane, big), axis=-1, keepdims=True)

    is_group = lane < N_GROUPS
    g_top, g_idx = first_argmax(jnp.where(is_group, logits, NEG_INF))
    g_val = 1.0 / jnp.sum(jnp.where(is_group, jnp.exp(logits - g_top), 0.0), axis=-1, keepdims=True)
    lo = N_GROUPS + EXPERTS_PER_GROUP * g_idx
    e_logits = jnp.where((lane >= lo) & (lane < lo + EXPERTS_PER_GROUP), logits, NEG_INF)
    v1, i1 = first_argmax(e_logits)
    v2, i2 = first_argmax(jnp.where(lane == i1, NEG_INF, e_logits))
    e21 = jnp.exp(v2 - v1)
    w1 = g_val / (1.0 + e21)
    w2 = g_val * e21 / (1.0 + e21)
    id_ref[...] = jnp.where(lane == 0, i1 - N_GROUPS, jnp.where(lane == 1, i2 - N_GROUPS, 0))
    w_ref[...] = jnp.where(lane == 0, w1, jnp.where(lane == 1, w2, 0.0))


def ffn_prep(x, xc, gain, shift, scale, r_cat, t=256):
    B, L, D = x.shape
    n_lat = B * L // t
    lat_per_seg = L // t
    if xc is None:
        T = B * L
        xs = [x.reshape(B * L, D)]
        x_specs = [pl.BlockSpec((t, D), lambda i: (i, 0))]
        seg = lambda i: (i // lat_per_seg, 0, 0)
        n_lat_tiles = None
    else:
        C = xc.shape[1]
        T = B * (L + C)
        xs = [x.reshape(B * L, D), xc.reshape(B * C, D)]
        x_specs = [pl.BlockSpec((t, D), lambda i: (jnp.minimum(i, n_lat - 1), 0)),
                   pl.BlockSpec((t, D), lambda i: (jnp.maximum(i - n_lat, 0), 0))]
        seg = lambda i: (jnp.minimum(i // lat_per_seg, B), 0, 0)
        n_lat_tiles = n_lat
    return pl.pallas_call(
        functools.partial(_ffn_prep_kernel, n_lat_tiles=n_lat_tiles),
        out_shape=[jax.ShapeDtypeStruct((T, D), F32),
                   jax.ShapeDtypeStruct((T, 128), jnp.int32),
                   jax.ShapeDtypeStruct((T, 128), F32)],
        grid=(T // t,),
        in_specs=x_specs + [
            pl.BlockSpec((1, D), lambda i: (0, 0)),
            pl.BlockSpec((1, 1, D), seg),
            pl.BlockSpec((1, 1, D), seg),
            pl.BlockSpec((D, 128), lambda i: (0, 0)),
        ],
        out_specs=[pl.BlockSpec((t, D), lambda i: (i, 0)),
                   pl.BlockSpec((t, 128), lambda i: (i, 0)),
                   pl.BlockSpec((t, 128), lambda i: (i, 0))],
        compiler_params=_cparams(("arbitrary",)),
        name="ffn_prep",
    )(*xs, gain, shift, scale, r_cat)


def _route(ids, tm):
    A = ids.shape[0] * TOP_K
    flat_e = ids.reshape(A)
    onehot = (flat_e[:, None] == jnp.arange(N_EXPERTS, dtype=jnp.int32)[None, :]).astype(jnp.int32)
    csum = jnp.cumsum(onehot, axis=0)
    rank = jnp.sum(csum * onehot, axis=1) - 1
    counts = csum[-1]
    pcounts = (counts + tm - 1) // tm * tm
    pends = jnp.cumsum(pcounts)
    pstarts = pends - pcounts
    dest = jnp.sum(onehot * pstarts[None, :], axis=1) + rank
    NB = (A + N_EXPERTS * (tm - 1)) // tm
    row_tok = jnp.zeros((NB * tm,), jnp.int32).at[dest].set(jnp.arange(A, dtype=jnp.int32) // TOP_K)
    block_e = jnp.clip(jnp.searchsorted(pends, jnp.arange(NB, dtype=jnp.int32) * tm, side='right'), 0, N_EXPERTS - 1)
    return dest.astype(jnp.int32), row_tok, block_e.astype(jnp.int32), (pends[-1:] // tm).astype(jnp.int32)


def _row_copy(src, src_row, dst, dst_row, sem):
    return pltpu.make_async_copy(src.at[pl.ds(src_row, 1), :], dst.at[pl.ds(dst_row, 1), :], sem)


def _dispatch_kernel(tok_ref, h_ref, o_ref, sem, *, rows):
    i = pl.program_id(0)
    base = i * rows

    def issue(r, carry):
        _row_copy(h_ref, tok_ref[base + r], o_ref, base + r, sem).start()
        return carry

    def drain(r, carry):
        _row_copy(h_ref, 0, o_ref, 0, sem).wait()
        return carry

    lax.fori_loop(0, rows, issue, 0)

    @pl.when(i > 0)
    def _():
        lax.fori_loop(0, rows, drain, 0)

    @pl.when(i == pl.num_programs(0) - 1)
    def _():
        lax.fori_loop(0, rows, drain, 0)


def dispatch(row_tok, h, rows=256):
    P = row_tok.shape[0]
    return pl.pallas_call(
        functools.partial(_dispatch_kernel, rows=rows),
        out_shape=jax.ShapeDtypeStruct((P, h.shape[1]), F32),
        grid_spec=pltpu.PrefetchScalarGridSpec(
            num_scalar_prefetch=1,
            grid=(P // rows,),
            in_specs=[pl.BlockSpec(memory_space=pl.ANY)],
            out_specs=pl.BlockSpec(memory_space=pl.ANY),
            scratch_shapes=[pltpu.SemaphoreType.DMA(())],
        ),
        compiler_params=_cparams(("arbitrary",)),
        name="moe_dispatch",
    )(row_tok, h)


def _expert_kernel(be_ref, nused_ref, x_ref, wg_ref, wu_ref, wd_ref, o_ref):
    @pl.when(pl.program_id(0) < nused_ref[0])
    def _():
        x = x_ref[...].astype(BF16)
        a = _dot(x, wg_ref[0])
        u = _dot(x, wu_ref[0])
        hmid = (a * jax.nn.sigmoid(a) * u).astype(BF16)
        o_ref[...] = _dot(hmid, wd_ref[0])

    @pl.when(pl.program_id(0) >= nused_ref[0])
    def _():
        o_ref[...] = jnp.zeros(o_ref.shape, o_ref.dtype)


def experts(block_e, n_used, xs, w_gate, w_up, w_down, tm):
    P, _ = xs.shape
    _, D, De = w_gate.shape
    last = lambda i, be, nu: jnp.minimum(i, nu[0] - 1)
    return pl.pallas_call(
        _expert_kernel,
        out_shape=jax.ShapeDtypeStruct((P, D), F32),
        grid_spec=pltpu.PrefetchScalarGridSpec(
            num_scalar_prefetch=2,
            grid=(P // tm,),
            in_specs=[
                pl.BlockSpec((tm, D), lambda i, be, nu: (last(i, be, nu), 0)),
                pl.BlockSpec((1, D, De), lambda i, be, nu: (be[last(i, be, nu)], 0, 0)),
                pl.BlockSpec((1, D, De), lambda i, be, nu: (be[last(i, be, nu)], 0, 0)),
                pl.BlockSpec((1, De, D), lambda i, be, nu: (be[last(i, be, nu)], 0, 0)),
            ],
            out_specs=pl.BlockSpec((tm, D), lambda i, be, nu: (i, 0)),
        ),
        compiler_params=_cparams(("arbitrary",)),
        name="moe_experts",
    )(block_e, n_used, xs, w_gate, w_up, w_down)


def _combine_kernel(dest_ref, x_ref, gate_ref, w_ref, y_ref, o_ref, buf, sem, *, tok_offset):
    t, D = x_ref.shape
    base = (tok_offset + pl.program_id(0) * t) * TOP_K

    def issue(r, carry):
        for k in range(TOP_K):
            _row_copy(y_ref, dest_ref[base + r * TOP_K + k], buf.at[k], r, sem).start()
        return carry

    def drain(r, carry):
        _row_copy(y_ref, 0, buf.at[0], 0, sem).wait()
        return carry

    lax.fori_loop(0, t, issue, 0)
    lax.fori_loop(0, t * TOP_K, drain, 0)
    w = w_ref[...]
    y = w[:, 0:1] * buf[0] + w[:, 1:2] * buf[1]
    o_ref[...] = x_ref[...] + gate_ref[0] * y


def combine(dest, x, gate, rw, ys, tok_offset, t=256):
    B, L, D = x.shape
    per_seg = L // t
    woff = tok_offset // t
    out = pl.pallas_call(
        functools.partial(_combine_kernel, tok_offset=tok_offset),
        out_shape=jax.ShapeDtypeStruct((B * L, D), F32),
        grid_spec=pltpu.PrefetchScalarGridSpec(
            num_scalar_prefetch=1,
            grid=(B * L // t,),
            in_specs=[
                pl.BlockSpec((t, D), lambda i, d: (i, 0)),
                pl.BlockSpec((1, 1, D), lambda i, d: (i // per_seg, 0, 0)),
                pl.BlockSpec((t, 128), lambda i, d: (woff + i, 0)),
                pl.BlockSpec(memory_space=pl.ANY),
            ],
            out_specs=pl.BlockSpec((t, D), lambda i, d: (i, 0)),
            scratch_shapes=[pltpu.VMEM((TOP_K, t, D), F32), pltpu.SemaphoreType.DMA(())],
        ),
        compiler_params=_cparams(("arbitrary",)),
        name="moe_combine",
    )(dest, x.reshape(B * L, D), gate, rw, ys)
    return out.reshape(B, L, D)


def _transpose_values(v, H, tk):
    B, N, _ = v.shape
    dv = v.shape[2] // H
    return jnp.transpose(v.reshape(B, N // tk, tk, H, dv), (0, 3, 1, 4, 2))


MOE_TM = 256
ATTN_C_TK = 512


def kernel(x, c, ctx, c_ctx, w_mod, b_mod, norm_mix, norm_ffn, w_in, w_out, qk_norm_a, sink_a, qk_norm_b, rpb_b,
           qk_norm_c, lambda_c, subln_c, ret_log_decay, router_group, router_expert, w_gate, w_up, w_down):
    B, L, D = x.shape
    C = ctx.shape[1]
    depth = w_mod.shape[0]

    cvec = jnp.zeros((8, D), F32).at[:B].set(c).at[B].set(c_ctx)
    mod = mod_vectors(cvec, w_mod, b_mod).reshape(depth, 8, 6, D)

    cosa, sina = _rope_tables(L, HEAD_DIM)
    cosc, sinc = _rope_tables(L, C_SUB_DIM)
    one_tab = jnp.ones((C, HEAD_DIM), F32)
    zero_tab = jnp.zeros((C, HEAD_DIM), F32)

    xc = ctx
    for l in range(depth):
        with_ctx = l < depth - 1
        lat = lambda j: mod[l, :B, j][:, None, :]
        cx = lambda j: jnp.broadcast_to(mod[l, B, j][None, None, :], (B, 1, D))
        w_in_l = w_in[l].astype(BF16)
        w_out_l = w_out[l].astype(BF16)
        gain_mix = norm_mix[l][None, :]

        proj = in_proj(x, gain_mix, lat(0), lat(1), w_in_l, tm=512)
        projc = in_proj(xc, gain_mix, cx(0), cx(1), w_in_l, tm=C)
        gain_c = jnp.tile(qk_norm_c[l], (1, HEAD_DIM // C_SUB_DIM))
        qk = qk_prep(proj, cosa, sina, cosc, sinc, qk_norm_a[l], qk_norm_b[l], gain_c)
        qkc = qk_prep(projc, one_tab, zero_tab, one_tab, zero_tab, qk_norm_a[l], qk_norm_b[l], gain_c)

        lam_init = 0.8 - 0.6 * math.exp(-0.3 * l)
        lam = _diff_lambda(lambda_c[l], lam_init).reshape(1)
        out_a = attn_a(qk, qkc, sink_a[l])
        out_b = attn_b(qk, qkc, _na_bias_tables(rpb_b[l], L // GRID_W))
        vt = _transpose_values(qk[:, :, C_V * HEAD_DIM:], C_HEADS, ATTN_C_TK)
        vct = _transpose_values(qkc[:, :, C_V * HEAD_DIM:], C_HEADS, C)
        out_c = attn_c(qk, vt, qkc, vct, lam, subln_c[l], 1.0 - lam_init)
        out_d, out_dc = retention(proj, projc, ret_log_decay[l], with_ctx)

        x = out_proj([out_a, out_b, out_c, out_d], x, lat(2), w_out_l, tm=1024)
        if with_ctx:
            out_abc_c = ctx_attn(qkc, sink_a[l], lam, subln_c[l], 1.0 - lam_init)
            xc = out_proj([out_abc_c, out_dc], xc, cx(2), w_out_l, tm=C)

        r_cat = jnp.zeros((D, 128), F32).at[:, :N_GROUPS].set(router_group[l])
        r_cat = r_cat.at[:, N_GROUPS:N_GROUPS + N_EXPERTS].set(router_expert[l])
        if with_ctx:
            shift = jnp.concatenate([lat(3), cx(3)[:1]], axis=0)
            scale = jnp.concatenate([lat(4), cx(4)[:1]], axis=0)
        else:
            shift, scale = lat(3), lat(4)
        h2, rid, rw = ffn_prep(x, xc if with_ctx else None, norm_ffn[l][None, :], shift, scale, r_cat)
        dest, row_tok, block_e, n_used = _route(rid[:, :TOP_K], MOE_TM)
        xs = dispatch(row_tok, h2)
        ys = experts(block_e, n_used, xs, w_gate[l].astype(BF16), w_up[l].astype(BF16), w_down[l].astype(BF16), MOE_TM)
        x = combine(dest, x, lat(5), rw, ys, 0)
        if with_ctx:
            xc = combine(dest, xc, cx(5), rw, ys, B * L)
    return x
```

```python
import functools
import math

import jax
import jax.numpy as jnp
import numpy as np
from jax import lax
from jax.experimental import pallas as pl
from jax.experimental.pallas import tpu as pltpu

F32 = jnp.float32
BF16 = jnp.bfloat16

HEAD_DIM = 128
GRID_W = 64
A_HEADS, A_KV_HEADS, A_WINDOW = 8, 2, 128
B_HEADS, NA_ROWS, NA_COLS = 8, 8, 16
C_HEADS, C_SUB_DIM = 8, 64
D_HEADS, RET_CHUNK = 8, 128
ROPE_BASE = 10000.0
N_GROUPS, EXPERTS_PER_GROUP, TOP_K = 4, 8, 2
N_EXPERTS = N_GROUPS * EXPERTS_PER_GROUP
EPS = 1e-6
NEG_INF = -1e30

A_Q, A_K, A_V = 0, 8, 10
B_Q, B_K, B_V = 12, 20, 28
C_Q, C_K, C_V = 36, 44, 52
D_Q, D_K, D_V, D_G = 60, 68, 76, 84
ABC_BLOCKS = 60
IN_COLS = 92 * 128

VMEM_LIMIT = 56 * 1024 * 1024


def _cparams(sem, vmem=VMEM_LIMIT):
    return pltpu.CompilerParams(dimension_semantics=sem, vmem_limit_bytes=vmem)


def _dot(a, b):
    return jnp.dot(a, b, preferred_element_type=F32)


def _dot_nt(a, b):
    return lax.dot_general(a, b, (((1,), (1,)), ((), ())), preferred_element_type=F32)


def _split_bf16(x):
    hi = x.astype(BF16)
    lo = (x - hi.astype(F32)).astype(BF16)
    return hi, lo


def _dot3(x, w):
    xh, xl = _split_bf16(x)
    wh, wl = _split_bf16(w)
    return _dot(xh, wh) + _dot(xl, wh) + _dot(xh, wl)


def _mod_kernel(c_ref, w_ref, b_ref, o_ref):
    c = c_ref[...]
    x = c * jax.nn.sigmoid(c)
    o_ref[0] = _dot3(x, w_ref[0]) + b_ref[0]


def mod_vectors(cvec, w_mod, b_mod, tn=512):
    depth, D, N = w_mod.shape
    return pl.pallas_call(
        _mod_kernel,
        out_shape=jax.ShapeDtypeStruct((depth, 8, N), F32),
        grid=(depth, N // tn),
        in_specs=[
            pl.BlockSpec((8, D), lambda l, j: (0, 0)),
            pl.BlockSpec((1, D, tn), lambda l, j: (l, 0, j)),
            pl.BlockSpec((1, 1, tn), lambda l, j: (l, 0, j)),
        ],
        out_specs=pl.BlockSpec((1, 8, tn), lambda l, j: (l, 0, j)),
        compiler_params=_cparams(("arbitrary", "arbitrary")),
        name="mod_vectors",
    )(cvec, w_mod, b_mod.reshape(depth, 1, N))


def _modulated(x, gain, shift, scale):
    ms = jnp.mean(x * x, axis=-1, keepdims=True)
    y = x * lax.rsqrt(ms + EPS) * gain
    return y * (1.0 + scale) + shift


def _in_proj_kernel(x_ref, gain_ref, shift_ref, scale_ref, w_ref, o_ref, h_ref):
    @pl.when(pl.program_id(2) == 0)
    def _():
        h_ref[...] = _modulated(x_ref[0], gain_ref[...], shift_ref[0], scale_ref[0]).astype(BF16)

    o_ref[0] = _dot(h_ref[...], w_ref[...]).astype(o_ref.dtype)


def in_proj(x, gain, shift, scale, w, tm, tn=512):
    B, L, D = x.shape
    N = w.shape[1]
    return pl.pallas_call(
        _in_proj_kernel,
        out_shape=jax.ShapeDtypeStruct((B, L, N), BF16),
        grid=(B, L // tm, N // tn),
        in_specs=[
            pl.BlockSpec((1, tm, D), lambda b, i, j: (b, i, 0)),
            pl.BlockSpec((1, D), lambda b, i, j: (0, 0)),
            pl.BlockSpec((1, 1, D), lambda b, i, j: (b, 0, 0)),
            pl.BlockSpec((1, 1, D), lambda b, i, j: (b, 0, 0)),
            pl.BlockSpec((D, tn), lambda b, i, j: (0, j)),
        ],
        out_specs=pl.BlockSpec((1, tm, tn), lambda b, i, j: (b, i, j)),
        scratch_shapes=[pltpu.VMEM((tm, D), BF16)],
        compiler_params=_cparams(("arbitrary", "arbitrary", "arbitrary")),
        name="in_proj",
    )(x, gain, shift, scale, w)


def _rope_tables(L, dim):
    t = jnp.arange(L)
    row = (t // GRID_W).astype(F32)
    col = (t % GRID_W).astype(F32)
    quarter = dim // 4
    inv = ROPE_BASE ** (-jnp.arange(quarter, dtype=F32) / quarter)
    ar = row[:, None] * inv[None, :]
    ac = col[:, None] * inv[None, :]
    ang = jnp.concatenate([ar, ar, ac, ac], axis=-1)
    sign = jnp.where((jnp.arange(dim) % (dim // 2)) < quarter, -1.0, 1.0).astype(F32)
    reps = HEAD_DIM // dim
    return jnp.tile(jnp.cos(ang), (1, reps)), jnp.tile(jnp.sin(ang) * sign[None, :], (1, reps))


def _qk_prep_kernel(p_ref, cosa_ref, sina_ref, cosc_ref, sinc_ref, ga_ref, gb_ref, gc_ref, o_ref):
    t = p_ref.shape[1]
    lane = lax.broadcasted_iota(jnp.int32, (t, HEAD_DIM), 1)
    low_half = lane < C_SUB_DIM

    def norm_full(x):
        return x * lax.rsqrt(jnp.mean(x * x, axis=-1, keepdims=True) + EPS)

    def norm_halves(x):
        sq = x * x
        s_lo = jnp.sum(jnp.where(low_half, sq, 0.0), axis=-1, keepdims=True)
        s_hi = jnp.sum(jnp.where(low_half, 0.0, sq), axis=-1, keepdims=True)
        ms = jnp.where(low_half, s_lo, s_hi) * (1.0 / C_SUB_DIM)
        return x * lax.rsqrt(ms + EPS)

    def rope(x, cos, sin, dim):
        quarter = dim // 4
        first = (lane % (dim // 2)) < quarter
        rot = jnp.where(first, pltpu.roll(x, HEAD_DIM - quarter, 1), pltpu.roll(x, quarter, 1))
        return x * cos + rot * sin

    cosa, sina = cosa_ref[...], sina_ref[...]
    cosc, sinc = cosc_ref[...], sinc_ref[...]
    scale_ab = HEAD_DIM ** -0.5
    scale_c = C_SUB_DIM ** -0.5 * math.log2(math.e)
    for blk in range(ABC_BLOCKS):
        sl = slice(blk * HEAD_DIM, (blk + 1) * HEAD_DIM)
        if A_V <= blk < B_Q or B_V <= blk < C_Q or C_V <= blk:
            o_ref[0, :, sl] = p_ref[0, :, sl]
            continue
        x = p_ref[0, :, sl].astype(F32)
        if blk < A_K:
            y = rope(norm_full(x) * ga_ref[0:1, :], cosa, sina, HEAD_DIM) * scale_ab
        elif blk < A_V:
            y = rope(norm_full(x) * ga_ref[1:2, :], cosa, sina, HEAD_DIM)
        elif blk < B_K:
            y = norm_full(x) * gb_ref[0:1, :] * scale_ab
        elif blk < B_V:
            y = norm_full(x) * gb_ref[1:2, :]
        elif blk < C_K:
            y = rope(norm_halves(x) * gc_ref[0:1, :], cosc, sinc, C_SUB_DIM) * scale_c
        else:
            y = rope(norm_halves(x) * gc_ref[1:2, :], cosc, sinc, C_SUB_DIM)
        o_ref[0, :, sl] = y.astype(o_ref.dtype)


def qk_prep(proj, cosa, sina, cosc, sinc, ga, gb, gc, t=256):
    B, L, _ = proj.shape
    W = ABC_BLOCKS * HEAD_DIM
    tab = pl.BlockSpec((t, HEAD_DIM), lambda b, i: (i, 0))
    par = pl.BlockSpec((2, HEAD_DIM), lambda b, i: (0, 0))
    return pl.pallas_call(
        _qk_prep_kernel,
        out_shape=jax.ShapeDtypeStruct((B, L, W), BF16),
        grid=(B, L // t),
        in_specs=[pl.BlockSpec((1, t, W), lambda b, i: (b, i, 0)), tab, tab, tab, tab, par, par, par],
        out_specs=pl.BlockSpec((1, t, W), lambda b, i: (b, i, 0)),
        compiler_params=_cparams(("arbitrary", "arbitrary")),
        name="qk_prep",
    )(proj, cosa, sina, cosc, sinc, ga, gb, gc)


def _attn_a_kernel(sink_ref, q_ref, k_ref, v_ref, kc_ref, vc_ref, o_ref, *, tq):
    L = k_ref.shape[1]
    kv, i = pl.program_id(1), pl.program_id(2)
    win = tq + 2 * A_WINDOW
    start = pl.multiple_of(jnp.clip(i * tq - A_WINDOW, 0, L - win), A_WINDOW)
    k = k_ref[0, pl.ds(start, win), :]
    v = v_ref[0, pl.ds(start, win), :]
    kc, vc = kc_ref[0], vc_ref[0]
    qpos = i * tq + lax.broadcasted_iota(jnp.int32, (tq, win), 0)
    kpos = start + lax.broadcasted_iota(jnp.int32, (tq, win), 1)
    in_window = jnp.abs(kpos - qpos) <= A_WINDOW
    group = A_HEADS // A_KV_HEADS
    for g in range(group):
        sl = slice(g * HEAD_DIM, (g + 1) * HEAD_DIM)
        q = q_ref[0, :, sl]
        s = jnp.where(in_window, _dot_nt(q, k), NEG_INF)
        sc = _dot_nt(q, kc)
        sink = sink_ref[kv * group + g]
        m = jnp.maximum(jnp.maximum(jnp.max(s, axis=-1, keepdims=True), jnp.max(sc, axis=-1, keepdims=True)), sink)
        p = jnp.exp(s - m)
        pc = jnp.exp(sc - m)
        denom = jnp.sum(p, axis=-1, keepdims=True) + jnp.sum(pc, axis=-1, keepdims=True) + jnp.exp(sink - m)
        o = _dot(p.astype(BF16), v) + _dot(pc.astype(BF16), vc)
        o_ref[0, :, sl] = (o / denom).astype(o_ref.dtype)


def attn_a(qk, qkc, sink, tq=512):
    B, L, _ = qk.shape
    C = qkc.shape[1]
    gw = (A_HEADS // A_KV_HEADS) * HEAD_DIM
    return pl.pallas_call(
        functools.partial(_attn_a_kernel, tq=tq),
        out_shape=jax.ShapeDtypeStruct((B, L, A_HEADS * HEAD_DIM), BF16),
        grid=(B, A_KV_HEADS, L // tq),
        in_specs=[
            pl.BlockSpec(memory_space=pltpu.SMEM),
            pl.BlockSpec((1, tq, gw), lambda b, kv, i: (b, i, kv)),
            pl.BlockSpec((1, L, HEAD_DIM), lambda b, kv, i: (b, 0, A_K + kv)),
            pl.BlockSpec((1, L, HEAD_DIM), lambda b, kv, i: (b, 0, A_V + kv)),
            pl.BlockSpec((1, C, HEAD_DIM), lambda b, kv, i: (b, 0, A_K + kv)),
            pl.BlockSpec((1, C, HEAD_DIM), lambda b, kv, i: (b, 0, A_V + kv)),
        ],
        out_specs=pl.BlockSpec((1, tq, gw), lambda b, kv, i: (b, i, kv)),
        compiler_params=_cparams(("arbitrary", "arbitrary", "arbitrary")),
        name="attn_a",
    )(sink, qk, qk, qk, qkc, qkc)


NA_QROWS = 8


def _na_bias_tables(rpb, R):
    W = GRID_W
    H = rpb.shape[0]
    wq = jnp.arange(W)
    cs = jnp.clip(wq - NA_COLS // 2, 0, W - NA_COLS)
    colmask = (wq[None, :] >= cs[:, None]) & (wq[None, :] < cs[:, None] + NA_COLS)
    rel_c = jnp.clip(wq[None, :] - wq[:, None] + (NA_COLS - 1), 0, 2 * NA_COLS - 2)
    per_row = jnp.where(colmask[None, None], rpb.astype(F32)[:, :, rel_c], NEG_INF)
    nq, nk = NA_QROWS, 2 * NA_QROWS
    tables = []
    for r0, ks in ((0, 0), (nq, nq - NA_ROWS // 2), (R - nq, R - nk)):
        qr = r0 + jnp.arange(nq)
        kr = ks + jnp.arange(nk)
        rs = jnp.clip(qr - NA_ROWS // 2, 0, R - NA_ROWS)
        valid = (kr[None, :] >= rs[:, None]) & (kr[None, :] < rs[:, None] + NA_ROWS)
        dr = jnp.clip(kr[None, :] - qr[:, None] + (NA_ROWS - 1), 0, 2 * NA_ROWS - 2)
        t = jnp.where(valid[None, :, :, None, None], per_row[:, dr], NEG_INF)
        tables.append(jnp.transpose(t, (0, 1, 3, 2, 4)).reshape(H, nq * W, nk * W))
    return jnp.stack(tables)


def _attn_b_kernel(q_ref, k_ref, v_ref, kc_ref, vc_ref, bias_ref, o_ref):
    L = k_ref.shape[1]
    R = L // GRID_W
    i = pl.program_id(2)
    nk = 2 * NA_QROWS
    krow = jnp.clip(i * NA_QROWS - NA_ROWS // 2, 0, R - nk)
    start = pl.multiple_of(krow * GRID_W, (NA_ROWS // 2) * GRID_W)
    k = k_ref[0, pl.ds(start, nk * GRID_W), :]
    v = v_ref[0, pl.ds(start, nk * GRID_W), :]
    q = q_ref[0]
    s = _dot_nt(q, k) + bias_ref[0, 0]
    sc = _dot_nt(q, kc_ref[0])
    m = jnp.maximum(jnp.max(s, axis=-1, keepdims=True), jnp.max(sc, axis=-1, keepdims=True))
    p = jnp.exp(s - m)
    pc = jnp.exp(sc - m)
    denom = jnp.sum(p, axis=-1, keepdims=True) + jnp.sum(pc, axis=-1, keepdims=True)
    o = _dot(p.astype(BF16), v) + _dot(pc.astype(BF16), vc_ref[0])
    o_ref[0] = (o / denom).astype(o_ref.dtype)


def attn_b(qk, qkc, bias):
    B, L, _ = qk.shape
    C = qkc.shape[1]
    tq = NA_QROWS * GRID_W
    nblk = L // tq

    def bias_idx(h, b, i):
        return (jnp.where(i == 0, 0, jnp.where(i == nblk - 1, 2, 1)), h, 0, 0)

    return pl.pallas_call(
        _attn_b_kernel,
        out_shape=jax.ShapeDtypeStruct((B, L, B_HEADS * HEAD_DIM), BF16),
        grid=(B_HEADS, B, nblk),
        in_specs=[
            pl.BlockSpec((1, tq, HEAD_DIM), lambda h, b, i: (b, i, B_Q + h)),
            pl.BlockSpec((1, L, HEAD_DIM), lambda h, b, i: (b, 0, B_K + h)),
            pl.BlockSpec((1, L, HEAD_DIM), lambda h, b, i: (b, 0, B_V + h)),
            pl.BlockSpec((1, C, HEAD_DIM), lambda h, b, i: (b, 0, B_K + h)),
            pl.BlockSpec((1, C, HEAD_DIM), lambda h, b, i: (b, 0, B_V + h)),
            pl.BlockSpec((1, 1, tq, 2 * tq), bias_idx),
        ],
        out_specs=pl.BlockSpec((1, tq, HEAD_DIM), lambda h, b, i: (b, i, h)),
        compiler_params=_cparams(("arbitrary", "arbitrary", "arbitrary")),
        name="attn_b",
    )(qk, qk, qk, qkc, qkc, bias)


def _diff_lambda(lambda_params, lam_init):
    lp = lambda_params.astype(F32)
    return jnp.exp(jnp.sum(lp[0] * lp[1])) - jnp.exp(jnp.sum(lp[2] * lp[3])) + lam_init


ATTN_C_SAFE_SCORE_BOUND = 40.0


def _attn_c_kernel(lam_ref, q_ref, k_ref, vt_ref, kc_ref, vct_ref, g_ref, o_ref, *scratch, tk, coef, online):
    if online:
        m_ref, l_ref, acc_ref = scratch
        m_ref[...] = jnp.full(m_ref.shape, NEG_INF, F32)
    else:
        l_ref, acc_ref = scratch
    tq = q_ref.shape[1]
    nk = vt_ref.shape[2]
    q = q_ref[0]
    lane = lax.broadcasted_iota(jnp.int32, q.shape, 1)
    zero = jnp.zeros_like(q)
    qs = (jnp.where(lane < C_SUB_DIM, q, zero), jnp.where(lane < C_SUB_DIM, zero, q))

    l_ref[...] = jnp.zeros(l_ref.shape, F32)
    acc_ref[...] = jnp.zeros(acc_ref.shape, F32)

    def accumulate(kblk, vtblk):
        for s_idx in range(2):
            s = _dot_nt(kblk, qs[s_idx])
            if online:
                m_old = m_ref[s_idx]
                m_new = jnp.maximum(m_old, jnp.max(s, axis=0, keepdims=True))
                alpha = jnp.exp2(m_old - m_new)
                p = jnp.exp2(s - m_new)
                l_ref[s_idx] = alpha * l_ref[s_idx] + jnp.sum(p, axis=0, keepdims=True)
                acc_ref[s_idx] = alpha * acc_ref[s_idx] + _dot(vtblk, p.astype(BF16))
                m_ref[s_idx] = m_new
            else:
                p = jnp.exp2(s)
                l_ref[s_idx] += jnp.sum(p, axis=0, keepdims=True)
                acc_ref[s_idx] += _dot(vtblk, p.astype(BF16))

    def body(j, carry):
        accumulate(k_ref[0, pl.ds(pl.multiple_of(j * tk, tk), tk), :], vt_ref[0, 0, j])
        return carry

    lax.fori_loop(0, nk, body, 0, unroll=4)
    accumulate(kc_ref[0], vct_ref[0, 0, 0])

    o = acc_ref[0] / l_ref[0] - lam_ref[0] * (acc_ref[1] / l_ref[1])
    y = o * lax.rsqrt(jnp.mean(o * o, axis=0, keepdims=True) + EPS) * g_ref[...] * coef
    o_ref[0] = y.T.astype(o_ref.dtype)


def attn_c(qk, vt, qkc, vct, lam, subln, coef, online, tq=512):
    B, L, _ = qk.shape
    C = qkc.shape[1]
    _, H, nk, dv, tk = vt.shape
    stats = [pltpu.VMEM((2, 1, tq), F32)] * (2 if online else 1)
    return pl.pallas_call(
        functools.partial(_attn_c_kernel, tk=tk, coef=coef, online=online),
        out_shape=jax.ShapeDtypeStruct((B, L, H * dv), BF16),
        grid=(B, H, L // tq),
        in_specs=[
            pl.BlockSpec(memory_space=pltpu.SMEM),
            pl.BlockSpec((1, tq, HEAD_DIM), lambda b, h, i: (b, i, C_Q + h)),
            pl.BlockSpec((1, L, HEAD_DIM), lambda b, h, i: (b, 0, C_K + h)),
            pl.BlockSpec((1, 1, nk, dv, tk), lambda b, h, i: (b, h, 0, 0, 0)),
            pl.BlockSpec((1, C, HEAD_DIM), lambda b, h, i: (b, 0, C_K + h)),
            pl.BlockSpec((1, 1, 1, dv, C), lambda b, h, i: (b, h, 0, 0, 0)),
            pl.BlockSpec((dv, 1), lambda b, h, i: (0, 0)),
        ],
        out_specs=pl.BlockSpec((1, tq, dv), lambda b, h, i: (b, i, h)),
        scratch_shapes=stats + [pltpu.VMEM((2, dv, tq), F32)],
        compiler_params=_cparams(("arbitrary", "arbitrary", "arbitrary")),
        name="attn_c_online" if online else "attn_c",
    )(lam, qk, qk, vt, qkc, vct, subln.reshape(dv, 1).astype(F32))


def _retention_kernel(ld_ref, q_ref, k_ref, v_ref, g_ref, qc_ref, kc_ref, vc_ref, gc_ref, *rest, with_ctx):
    if with_ctx:
        y_ref, yc_ref, of_ref, ob_ref, ocf_ref = rest
    else:
        y_ref, of_ref, ob_ref = rest
        yc_ref = ocf_ref = None
    c = RET_CHUNK
    L, C = q_ref.shape[1], qc_ref.shape[1]
    h = pl.program_id(1)
    scale = HEAD_DIM ** -0.5
    row = lax.broadcasted_iota(jnp.int32, (c, c), 0).astype(F32)
    col = lax.broadcasted_iota(jnp.int32, (c, c), 1).astype(F32)
    rowp = lax.broadcasted_iota(jnp.int32, (c, HEAD_DIM), 0).astype(F32)

    def decays(lg, reverse):
        if reverse:
            rel, q_exp, k_exp = col - row, c - rowp, rowp
        else:
            rel, q_exp, k_exp = row - col, rowp + 1.0, (c - 1.0) - rowp
        intra = jnp.where(rel >= 0, jnp.exp(lg * jnp.maximum(rel, 0.0)), 0.0)
        return intra, jnp.exp(lg * q_exp), jnp.exp(lg * k_exp), jnp.exp(lg * c)

    def step(S, q, k, v, dec, want_out):
        intra, q_decay, k_decay, chunk_decay = dec
        o = None
        if want_out:
            qs = (q.astype(F32) * scale).astype(BF16)
            s = _dot_nt(qs, k) * intra
            o = _dot(s.astype(BF16), v) + _dot(qs, S.astype(BF16)) * q_decay
        kd_t = (k.astype(F32) * k_decay).T.astype(BF16)
        return S * chunk_decay + _dot(kd_t, v), o

    def gated(o, g):
        g = g.astype(F32)
        y = o * lax.rsqrt(jnp.mean(o * o, axis=-1, keepdims=True) + EPS)
        return y * (g * jax.nn.sigmoid(g))

    def lat(ref, j):
        return ref[0, pl.ds(pl.multiple_of(j * c, c), c), :]

    def ctx(ref, j):
        return ref[0, j * c:(j + 1) * c, :]

    dec_f = decays(ld_ref[0, h], False)
    dec_b = decays(ld_ref[1, h], True)
    S_f = jnp.zeros((HEAD_DIM, HEAD_DIM), F32)
    for j in range(C // c):
        S_f, o = step(S_f, ctx(qc_ref, j), ctx(kc_ref, j), ctx(vc_ref, j), dec_f, with_ctx)
        if with_ctx:
            ocf_ref[j * c:(j + 1) * c, :] = o
    S_b = jnp.zeros((HEAD_DIM, HEAD_DIM), F32)
    for j in reversed(range(C // c)):
        S_b, o = step(S_b, ctx(qc_ref, j), ctx(kc_ref, j), ctx(vc_ref, j), dec_b, with_ctx)
        if with_ctx:
            yc_ref[0, j * c:(j + 1) * c, :] = gated(o + ocf_ref[j * c:(j + 1) * c, :], ctx(gc_ref, j)).astype(yc_ref.dtype)

    n = L // c

    def both(t, carry):
        S_f, S_b = carry
        S_f, o_f = step(S_f, lat(q_ref, t), lat(k_ref, t), lat(v_ref, t), dec_f, True)
        of_ref[pl.ds(pl.multiple_of(t * c, c), c), :] = o_f
        j = n - 1 - t
        S_b, o_b = step(S_b, lat(q_ref, j), lat(k_ref, j), lat(v_ref, j), dec_b, True)
        ob_ref[pl.ds(pl.multiple_of(j * c, c), c), :] = o_b
        return S_f, S_b

    lax.fori_loop(0, n, both, (S_f, S_b), unroll=2)

    def finish(j, carry):
        sl = pl.ds(pl.multiple_of(j * c, c), c)
        y_ref[0, sl, :] = gated(of_ref[sl, :] + ob_ref[sl, :], lat(g_ref, j)).astype(y_ref.dtype)
        return carry

    lax.fori_loop(0, n, finish, 0, unroll=2)


def retention(proj, projc, log_decay, with_ctx):
    B, L, _ = proj.shape
    C = projc.shape[1]

    def lat(blk):
        return pl.BlockSpec((1, L, HEAD_DIM), lambda b, h: (b, 0, blk + h))

    def ctx(blk):
        return pl.BlockSpec((1, C, HEAD_DIM), lambda b, h: (b, 0, blk + h))

    out_shape = [jax.ShapeDtypeStruct((B, L, D_HEADS * HEAD_DIM), BF16)]
    out_specs = [pl.BlockSpec((1, L, HEAD_DIM), lambda b, h: (b, 0, h))]
    scratch = [pltpu.VMEM((L, HEAD_DIM), F32), pltpu.VMEM((L, HEAD_DIM), F32)]
    if with_ctx:
        out_shape.append(jax.ShapeDtypeStruct((B, C, D_HEADS * HEAD_DIM), BF16))
        out_specs.append(pl.BlockSpec((1, C, HEAD_DIM), lambda b, h: (b, 0, h)))
        scratch.append(pltpu.VMEM((C, HEAD_DIM), F32))
    outs = pl.pallas_call(
        functools.partial(_retention_kernel, with_ctx=with_ctx),
        out_shape=out_shape,
        grid=(B, D_HEADS),
        in_specs=[pl.BlockSpec(memory_space=pltpu.SMEM),
                  lat(D_Q), lat(D_K), lat(D_V), lat(D_G), ctx(D_Q), ctx(D_K), ctx(D_V), ctx(D_G)],
        out_specs=out_specs,
        scratch_shapes=scratch,
        compiler_params=_cparams(("arbitrary", "arbitrary")),
        name="retention_ctx" if with_ctx else "retention",
    )(log_decay.astype(F32), proj, proj, proj, proj, projc, projc, projc, projc)
    return (outs[0], outs[1]) if with_ctx else (outs[0], None)


def _ctx_attn_kernel(sink_ref, lam_ref, qk_ref, g_ref, o_ref, *, coef):
    def blk(idx):
        return qk_ref[0, :, idx * HEAD_DIM:(idx + 1) * HEAD_DIM]

    def softmax_out(q, k, v, sink=None, exp=jnp.exp):
        s = _dot_nt(q, k)
        m = jnp.max(s, axis=-1, keepdims=True)
        if sink is not None:
            m = jnp.maximum(m, sink)
        p = exp(s - m)
        denom = jnp.sum(p, axis=-1, keepdims=True)
        if sink is not None:
            denom = denom + jnp.exp(sink - m)
        return _dot(p.astype(BF16), v) / denom

    group = A_HEADS // A_KV_HEADS
    for h in range(A_HEADS):
        o = softmax_out(blk(A_Q + h), blk(A_K + h // group), blk(A_V + h // group), sink_ref[h])
        o_ref[0, :, h * HEAD_DIM:(h + 1) * HEAD_DIM] = o.astype(o_ref.dtype)
    for h in range(B_HEADS):
        o = softmax_out(blk(B_Q + h), blk(B_K + h), blk(B_V + h))
        o_ref[0, :, (A_HEADS + h) * HEAD_DIM:(A_HEADS + h + 1) * HEAD_DIM] = o.astype(o_ref.dtype)
    lane = lax.broadcasted_iota(jnp.int32, (qk_ref.shape[1], HEAD_DIM), 1)
    for h in range(C_HEADS):
        q, k, v = blk(C_Q + h), blk(C_K + h), blk(C_V + h)
        zero = jnp.zeros_like(q)
        o1 = softmax_out(jnp.where(lane < C_SUB_DIM, q, zero), k, v, exp=jnp.exp2)
        o2 = softmax_out(jnp.where(lane < C_SUB_DIM, zero, q), k, v, exp=jnp.exp2)
        o = o1 - lam_ref[0] * o2
        y = o * lax.rsqrt(jnp.mean(o * o, axis=-1, keepdims=True) + EPS) * g_ref[...] * coef
        base = A_HEADS + B_HEADS + h
        o_ref[0, :, base * HEAD_DIM:(base + 1) * HEAD_DIM] = y.astype(o_ref.dtype)


def ctx_attn(qkc, sink, lam, subln, coef):
    B, C, W = qkc.shape
    n_out = (A_HEADS + B_HEADS + C_HEADS) * HEAD_DIM
    return pl.pallas_call(
        functools.partial(_ctx_attn_kernel, coef=coef),
        out_shape=jax.ShapeDtypeStruct((B, C, n_out), BF16),
        grid=(B,),
        in_specs=[
            pl.BlockSpec(memory_space=pltpu.SMEM),
            pl.BlockSpec(memory_space=pltpu.SMEM),
            pl.BlockSpec((1, C, W), lambda b: (b, 0, 0)),
            pl.BlockSpec((1, HEAD_DIM), lambda b: (0, 0)),
        ],
        out_specs=pl.BlockSpec((1, C, n_out), lambda b: (b, 0, 0)),
        compiler_params=_cparams(("arbitrary",)),
        name="ctx_attn",
    )(sink, lam, qkc, subln.reshape(1, HEAD_DIM).astype(F32))


def _out_proj_kernel(*refs, n_parts):
    parts = refs[:n_parts]
    x_ref, gate_ref, w_ref, o_ref = refs[n_parts:]
    mix = jnp.concatenate([p[0] for p in parts], axis=-1)
    o_ref[0] = x_ref[0] + gate_ref[0] * _dot(mix, w_ref[...])


def out_proj(parts, x, gate, w, tm, tn=512):
    B, L, D = x.shape
    part_specs = [pl.BlockSpec((1, tm, p.shape[2]), lambda b, i, j: (b, i, 0)) for p in parts]
    return pl.pallas_call(
        functools.partial(_out_proj_kernel, n_parts=len(parts)),
        out_shape=jax.ShapeDtypeStruct((B, L, D), F32),
        grid=(B, L // tm, D // tn),
        in_specs=part_specs + [
            pl.BlockSpec((1, tm, tn), lambda b, i, j: (b, i, j)),
            pl.BlockSpec((1, 1, tn), lambda b, i, j: (b, 0, j)),
            pl.BlockSpec((w.shape[0], tn), lambda b, i, j: (0, j)),
        ],
        out_specs=pl.BlockSpec((1, tm, tn), lambda b, i, j: (b, i, j)),
        compiler_params=_cparams(("arbitrary", "arbitrary", "arbitrary")),
        name="out_proj",
    )(*parts, x, gate, w)


def _ffn_prep_kernel(*refs, n_lat_tiles):
    if n_lat_tiles is None:
        x_ref, gain_ref, shift_ref, scale_ref, r_ref, h_ref, id_ref, w_ref = refs
        x = x_ref[...]
    else:
        xl_ref, xc_ref, gain_ref, shift_ref, scale_ref, r_ref, h_ref, id_ref, w_ref = refs
        x = jnp.where(pl.program_id(0) < n_lat_tiles, xl_ref[...], xc_ref[...])
    h = _modulated(x, gain_ref[...], shift_ref[0], scale_ref[0])
    h_ref[...] = h

    logits = _dot3(h, r_ref[...])
    lane = lax.broadcasted_iota(jnp.int32, logits.shape, 1)
    big = jnp.int32(logits.shape[1])

    def first_argmax(vals):
        top = jnp.max(vals, axis=-1, keepdims=True)
        return top, jnp.min(jnp.where(vals == top, lane, big), axis=-1, keepdims=True)

    is_group = lane < N_GROUPS
    g_top, g_idx = first_argmax(jnp.where(is_group, logits, NEG_INF))
    g_val = 1.0 / jnp.sum(jnp.where(is_group, jnp.exp(logits - g_top), 0.0), axis=-1, keepdims=True)
    lo = N_GROUPS + EXPERTS_PER_GROUP * g_idx
    e_logits = jnp.where((lane >= lo) & (lane < lo + EXPERTS_PER_GROUP), logits, NEG_INF)
    v1, i1 = first_argmax(e_logits)
    v2, i2 = first_argmax(jnp.where(lane == i1, NEG_INF, e_logits))
    e21 = jnp.exp(v2 - v1)
    w1 = g_val / (1.0 + e21)
    w2 = g_val * e21 / (1.0 + e21)
    id_ref[...] = jnp.where(lane == 0, i1 - N_GROUPS, jnp.where(lane == 1, i2 - N_GROUPS, 0))
    w_ref[...] = jnp.where(lane == 0, w1, jnp.where(lane == 1, w2, 0.0))


def ffn_prep(x, xc, gain, shift, scale, r_cat, t=256):
    B, L, D = x.shape
    n_lat = B * L // t
    lat_per_seg = L // t
    if xc is None:
        T = B * L
        xs = [x.reshape(B * L, D)]
        x_specs = [pl.BlockSpec((t, D), lambda i: (i, 0))]
        seg = lambda i: (i // lat_per_seg, 0, 0)
        n_lat_tiles = None
    else:
        C = xc.shape[1]
        T = B * (L + C)
        xs = [x.reshape(B * L, D), xc.reshape(B * C, D)]
        x_specs = [pl.BlockSpec((t, D), lambda i: (jnp.minimum(i, n_lat - 1), 0)),
                   pl.BlockSpec((t, D), lambda i: (jnp.maximum(i - n_lat, 0), 0))]
        seg = lambda i: (jnp.minimum(i // lat_per_seg, B), 0, 0)
        n_lat_tiles = n_lat
    return pl.pallas_call(
        functools.partial(_ffn_prep_kernel, n_lat_tiles=n_lat_tiles),
        out_shape=[jax.ShapeDtypeStruct((T, D), F32),
                   jax.ShapeDtypeStruct((T, 128), jnp.int32),
                   jax.ShapeDtypeStruct((T, 128), F32)],
        grid=(T // t,),
        in_specs=x_specs + [
            pl.BlockSpec((1, D), lambda i: (0, 0)),
            pl.BlockSpec((1, 1, D), seg),
            pl.BlockSpec((1, 1, D), seg),
            pl.BlockSpec((D, 128), lambda i: (0, 0)),
        ],
        out_specs=[pl.BlockSpec((t, D), lambda i: (i, 0)),
                   pl.BlockSpec((t, 128), lambda i: (i, 0)),
                   pl.BlockSpec((t, 128), lambda i: (i, 0))],
        compiler_params=_cparams(("arbitrary",)),
        name="ffn_prep",
    )(*xs, gain, shift, scale, r_cat)


def _route(ids, tm):
    A = ids.shape[0] * TOP_K
    flat_e = ids.reshape(A)
    onehot = (flat_e[:, None] == jnp.arange(N_EXPERTS, dtype=jnp.int32)[None, :]).astype(jnp.int32)
    csum = jnp.cumsum(onehot, axis=0)
    rank = jnp.sum(csum * onehot, axis=1) - 1
    counts = csum[-1]
    pcounts = (counts + tm - 1) // tm * tm
    pends = jnp.cumsum(pcounts)
    pstarts = pends - pcounts
    dest = jnp.sum(onehot * pstarts[None, :], axis=1) + rank
    NB = (A + N_EXPERTS * (tm - 1)) // tm
    row_tok = jnp.zeros((NB * tm,), jnp.int32).at[dest].set(jnp.arange(A, dtype=jnp.int32) // TOP_K)
    block_e = jnp.clip(jnp.searchsorted(pends, jnp.arange(NB, dtype=jnp.int32) * tm, side='right'), 0, N_EXPERTS - 1)
    return dest.astype(jnp.int32), row_tok, block_e.astype(jnp.int32), (pends[-1:] // tm).astype(jnp.int32)


def _row_copy(src, src_row, dst, dst_row, sem):
    return pltpu.make_async_copy(src.at[pl.ds(src_row, 1), :], dst.at[pl.ds(dst_row, 1), :], sem)


def _dispatch_kernel(tok_ref, h_ref, o_ref, sem):
    rows = o_ref.shape[0]
    base = pl.program_id(0) * rows

    def issue(r, carry):
        _row_copy(h_ref, tok_ref[base + r], o_ref, r, sem).start()
        return carry

    def drain(r, carry):
        _row_copy(h_ref, 0, o_ref, 0, sem).wait()
        return carry

    lax.fori_loop(0, rows, issue, 0)
    lax.fori_loop(0, rows, drain, 0)


def dispatch(row_tok, h, rows=256):
    P = row_tok.shape[0]
    D = h.shape[1]
    return pl.pallas_call(
        _dispatch_kernel,
        out_shape=jax.ShapeDtypeStruct((P, D), F32),
        grid_spec=pltpu.PrefetchScalarGridSpec(
            num_scalar_prefetch=1,
            grid=(P // rows,),
            in_specs=[pl.BlockSpec(memory_space=pl.ANY)],
            out_specs=pl.BlockSpec((rows, D), lambda i, tok: (i, 0)),
            scratch_shapes=[pltpu.SemaphoreType.DMA(())],
        ),
        compiler_params=_cparams(("arbitrary",)),
        name="moe_dispatch",
    )(row_tok, h)


def _expert_kernel(be_ref, nused_ref, x_ref, wg_ref, wu_ref, wd_ref, o_ref):
    @pl.when(pl.program_id(0) < nused_ref[0])
    def _():
        x = x_ref[...].astype(BF16)
        a = _dot(x, wg_ref[0])
        u = _dot(x, wu_ref[0])
        hmid = (a * jax.nn.sigmoid(a) * u).astype(BF16)
        o_ref[...] = _dot(hmid, wd_ref[0])

    @pl.when(pl.program_id(0) >= nused_ref[0])
    def _():
        o_ref[...] = jnp.zeros(o_ref.shape, o_ref.dtype)


def experts(block_e, n_used, xs, w_gate, w_up, w_down, tm):
    P, _ = xs.shape
    _, D, De = w_gate.shape
    last = lambda i, be, nu: jnp.minimum(i, nu[0] - 1)
    return pl.pallas_call(
        _expert_kernel,
        out_shape=jax.ShapeDtypeStruct((P, D), F32),
        grid_spec=pltpu.PrefetchScalarGridSpec(
            num_scalar_prefetch=2,
            grid=(P // tm,),
            in_specs=[
                pl.BlockSpec((tm, D), lambda i, be, nu: (last(i, be, nu), 0)),
                pl.BlockSpec((1, D, De), lambda i, be, nu: (be[last(i, be, nu)], 0, 0)),
                pl.BlockSpec((1, D, De), lambda i, be, nu: (be[last(i, be, nu)], 0, 0)),
                pl.BlockSpec((1, De, D), lambda i, be, nu: (be[last(i, be, nu)], 0, 0)),
            ],
            out_specs=pl.BlockSpec((tm, D), lambda i, be, nu: (i, 0)),
        ),
        compiler_params=_cparams(("arbitrary",)),
        name="moe_experts",
    )(block_e, n_used, xs, w_gate, w_up, w_down)


def _combine_kernel(dest_ref, x_ref, gate_ref, w_ref, y_ref, o_ref, buf, sem, *, tok_offset):
    t, D = x_ref.shape
    base = (tok_offset + pl.program_id(0) * t) * TOP_K

    def issue(r, carry):
        for k in range(TOP_K):
            _row_copy(y_ref, dest_ref[base + r * TOP_K + k], buf.at[k], r, sem).start()
        return carry

    def drain(r, carry):
        _row_copy(y_ref, 0, buf.at[0], 0, sem).wait()
        return carry

    lax.fori_loop(0, t, issue, 0)
    lax.fori_loop(0, t * TOP_K, drain, 0)
    w = w_ref[...]
    y = w[:, 0:1] * buf[0] + w[:, 1:2] * buf[1]
    o_ref[...] = x_ref[...] + gate_ref[0] * y


def combine(dest, x, gate, rw, ys, tok_offset, t=256):
    B, L, D = x.shape
    per_seg = L // t
    woff = tok_offset // t
    out = pl.pallas_call(
        functools.partial(_combine_kernel, tok_offset=tok_offset),
        out_shape=jax.ShapeDtypeStruct((B * L, D), F32),
        grid_spec=pltpu.PrefetchScalarGridSpec(
            num_scalar_prefetch=1,
            grid=(B * L // t,),
            in_specs=[
                pl.BlockSpec((t, D), lambda i, d: (i, 0)),
                pl.BlockSpec((1, 1, D), lambda i, d: (i // per_seg, 0, 0)),
                pl.BlockSpec((t, 128), lambda i, d: (woff + i, 0)),
                pl.BlockSpec(memory_space=pl.ANY),
            ],
            out_specs=pl.BlockSpec((t, D), lambda i, d: (i, 0)),
            scratch_shapes=[pltpu.VMEM((TOP_K, t, D), F32), pltpu.SemaphoreType.DMA(())],
        ),
        compiler_params=_cparams(("arbitrary",)),
        name="moe_combine",
    )(dest, x.reshape(B * L, D), gate, rw, ys)
    return out.reshape(B, L, D)


def _transpose_values(v, H, tk):
    B, N, _ = v.shape
    dv = v.shape[2] // H
    return jnp.transpose(v.reshape(B, N // tk, tk, H, dv), (0, 3, 1, 4, 2))


MOE_TM = 256
ATTN_C_TK = 512


def kernel(x, c, ctx, c_ctx, w_mod, b_mod, norm_mix, norm_ffn, w_in, w_out, qk_norm_a, sink_a, qk_norm_b, rpb_b,
           qk_norm_c, lambda_c, subln_c, ret_log_decay, router_group, router_expert, w_gate, w_up, w_down):
    B, L, D = x.shape
    C = ctx.shape[1]
    depth = w_mod.shape[0]

    cvec = jnp.zeros((8, D), F32).at[:B].set(c).at[B].set(c_ctx)
    mod = mod_vectors(cvec, w_mod, b_mod).reshape(depth, 8, 6, D)

    cosa, sina = _rope_tables(L, HEAD_DIM)
    cosc, sinc = _rope_tables(L, C_SUB_DIM)
    one_tab = jnp.ones((C, HEAD_DIM), F32)
    zero_tab = jnp.zeros((C, HEAD_DIM), F32)

    xc = ctx
    for l in range(depth):
        with_ctx = l < depth - 1
        lat = lambda j: mod[l, :B, j][:, None, :]
        cx = lambda j: jnp.broadcast_to(mod[l, B, j][None, None, :], (B, 1, D))
        w_in_l = w_in[l].astype(BF16)
        w_out_l = w_out[l].astype(BF16)
        gain_mix = norm_mix[l][None, :]

        proj = in_proj(x, gain_mix, lat(0), lat(1), w_in_l, tm=512)
        projc = in_proj(xc, gain_mix, cx(0), cx(1), w_in_l, tm=C)
        gain_c = jnp.tile(qk_norm_c[l], (1, HEAD_DIM // C_SUB_DIM))
        qk = qk_prep(proj, cosa, sina, cosc, sinc, qk_norm_a[l], qk_norm_b[l], gain_c)
        qkc = qk_prep(projc, one_tab, zero_tab, one_tab, zero_tab, qk_norm_a[l], qk_norm_b[l], gain_c)

        lam_init = 0.8 - 0.6 * math.exp(-0.3 * l)
        lam = _diff_lambda(lambda_c[l], lam_init).reshape(1)
        out_a = attn_a(qk, qkc, sink_a[l])
        out_b = attn_b(qk, qkc, _na_bias_tables(rpb_b[l], L // GRID_W))
        vt = _transpose_values(qk[:, :, C_V * HEAD_DIM:], C_HEADS, ATTN_C_TK)
        vct = _transpose_values(qkc[:, :, C_V * HEAD_DIM:], C_HEADS, C)
        score_bound = (C_SUB_DIM ** 0.5) * jnp.max(jnp.abs(qk_norm_c[l, 0])) * jnp.max(jnp.abs(qk_norm_c[l, 1]))
        attn_c_l = functools.partial(attn_c, qk, vt, qkc, vct, lam, subln_c[l], 1.0 - lam_init)
        out_c = lax.cond(score_bound <= ATTN_C_SAFE_SCORE_BOUND,
                         lambda: attn_c_l(online=False), lambda: attn_c_l(online=True))
        out_d, out_dc = retention(proj, projc, ret_log_decay[l], with_ctx)

        x = out_proj([out_a, out_b, out_c, out_d], x, lat(2), w_out_l, tm=1024)
        if with_ctx:
            out_abc_c = ctx_attn(qkc, sink_a[l], lam, subln_c[l], 1.0 - lam_init)
            xc = out_proj([out_abc_c, out_dc], xc, cx(2), w_out_l, tm=C)

        r_cat = jnp.zeros((D, 128), F32).at[:, :N_GROUPS].set(router_group[l])
        r_cat = r_cat.at[:, N_GROUPS:N_GROUPS + N_EXPERTS].set(router_expert[l])
        if with_ctx:
            shift = jnp.concatenate([lat(3), cx(3)[:1]], axis=0)
            scale = jnp.concatenate([lat(4), cx(4)[:1]], axis=0)
        else:
            shift, scale = lat(3), lat(4)
        h2, rid, rw = ffn_prep(x, xc if with_ctx else None, norm_ffn[l][None, :], shift, scale, r_cat)
        dest, row_tok, block_e, n_used = _route(rid[:, :TOP_K], MOE_TM)
        xs = dispatch(row_tok, h2)
        ys = experts(block_e, n_used, xs, w_gate[l].astype(BF16), w_up[l].astype(BF16), w_down[l].astype(BF16), MOE_TM)
        x = combine(dest, x, lat(5), rw, ys, 0)
        if with_ctx:
            xc = combine(dest, xc, cx(5), rw, ys, B * L)
    return x
```

```python
import functools
import math

import jax
import jax.numpy as jnp
import numpy as np
from jax import lax
from jax.experimental import pallas as pl
from jax.experimental.pallas import tpu as pltpu

F32 = jnp.float32
BF16 = jnp.bfloat16

HEAD_DIM = 128
GRID_W = 64
A_HEADS, A_KV_HEADS, A_WINDOW = 8, 2, 128
B_HEADS, NA_ROWS, NA_COLS = 8, 8, 16
C_HEADS, C_SUB_DIM = 8, 64
D_HEADS, RET_CHUNK = 8, 128
ROPE_BASE = 10000.0
N_GROUPS, EXPERTS_PER_GROUP, TOP_K = 4, 8, 2
N_EXPERTS = N_GROUPS * EXPERTS_PER_GROUP
EPS = 1e-6
NEG_INF = -1e30

A_Q, A_K, A_V = 0, 8, 10
B_Q, B_K, B_V = 12, 20, 28
C_Q, C_K, C_V = 36, 44, 52
D_Q, D_K, D_V, D_G = 60, 68, 76, 84
ABC_BLOCKS = 60
IN_COLS = 92 * 128

VMEM_LIMIT = 56 * 1024 * 1024


def _cparams(sem, vmem=VMEM_LIMIT):
    return pltpu.CompilerParams(dimension_semantics=sem, vmem_limit_bytes=vmem)


def _dot(a, b):
    return jnp.dot(a, b, preferred_element_type=F32)


def _dot_nt(a, b):
    return lax.dot_general(a, b, (((1,), (1,)), ((), ())), preferred_element_type=F32)


def _split_bf16(x):
    hi = x.astype(BF16)
    lo = (x - hi.astype(F32)).astype(BF16)
    return hi, lo


def _dot3(x, w):
    xh, xl = _split_bf16(x)
    wh, wl = _split_bf16(w)
    return _dot(xh, wh) + _dot(xl, wh) + _dot(xh, wl)


def _mod_kernel(c_ref, w_ref, b_ref, o_ref):
    c = c_ref[...]
    x = c * jax.nn.sigmoid(c)
    o_ref[0] = _dot3(x, w_ref[0]) + b_ref[0]


def mod_vectors(cvec, w_mod, b_mod, tn=512):
    depth, D, N = w_mod.shape
    return pl.pallas_call(
        _mod_kernel,
        out_shape=jax.ShapeDtypeStruct((depth, 8, N), F32),
        grid=(depth, N // tn),
        in_specs=[
            pl.BlockSpec((8, D), lambda l, j: (0, 0)),
            pl.BlockSpec((1, D, tn), lambda l, j: (l, 0, j)),
            pl.BlockSpec((1, 1, tn), lambda l, j: (l, 0, j)),
        ],
        out_specs=pl.BlockSpec((1, 8, tn), lambda l, j: (l, 0, j)),
        compiler_params=_cparams(("arbitrary", "arbitrary")),
        name="mod_vectors",
    )(cvec, w_mod, b_mod.reshape(depth, 1, N))


def _modulated(x, gain, shift, scale):
    ms = jnp.mean(x * x, axis=-1, keepdims=True)
    y = x * lax.rsqrt(ms + EPS) * gain
    return y * (1.0 + scale) + shift


def _in_proj_kernel(x_ref, gain_ref, shift_ref, scale_ref, w_ref, o_ref, h_ref):
    @pl.when(pl.program_id(2) == 0)
    def _():
        h_ref[...] = _modulated(x_ref[0], gain_ref[...], shift_ref[0], scale_ref[0]).astype(BF16)

    o_ref[0] = _dot(h_ref[...], w_ref[...]).astype(o_ref.dtype)


def in_proj(x, gain, shift, scale, w, tm, tn=512):
    B, L, D = x.shape
    N = w.shape[1]
    return pl.pallas_call(
        _in_proj_kernel,
        out_shape=jax.ShapeDtypeStruct((B, L, N), BF16),
        grid=(B, L // tm, N // tn),
        in_specs=[
            pl.BlockSpec((1, tm, D), lambda b, i, j: (b, i, 0)),
            pl.BlockSpec((1, D), lambda b, i, j: (0, 0)),
            pl.BlockSpec((1, 1, D), lambda b, i, j: (b, 0, 0)),
            pl.BlockSpec((1, 1, D), lambda b, i, j: (b, 0, 0)),
            pl.BlockSpec((D, tn), lambda b, i, j: (0, j)),
        ],
        out_specs=pl.BlockSpec((1, tm, tn), lambda b, i, j: (b, i, j)),
        scratch_shapes=[pltpu.VMEM((tm, D), BF16)],
        compiler_params=_cparams(("arbitrary", "arbitrary", "arbitrary")),
        name="in_proj",
    )(x, gain, shift, scale, w)


def _rope_tables(L, dim):
    t = jnp.arange(L)
    row = (t // GRID_W).astype(F32)
    col = (t % GRID_W).astype(F32)
    quarter = dim // 4
    inv = ROPE_BASE ** (-jnp.arange(quarter, dtype=F32) / quarter)
    ar = row[:, None] * inv[None, :]
    ac = col[:, None] * inv[None, :]
    ang = jnp.concatenate([ar, ar, ac, ac], axis=-1)
    sign = jnp.where((jnp.arange(dim) % (dim // 2)) < quarter, -1.0, 1.0).astype(F32)
    reps = HEAD_DIM // dim
    return jnp.tile(jnp.cos(ang), (1, reps)), jnp.tile(jnp.sin(ang) * sign[None, :], (1, reps))


def _qk_prep_kernel(p_ref, cosa_ref, sina_ref, cosc_ref, sinc_ref, ga_ref, gb_ref, gc_ref, o_ref):
    t = p_ref.shape[1]
    lane = lax.broadcasted_iota(jnp.int32, (t, HEAD_DIM), 1)

    ri = lax.broadcasted_iota(jnp.int32, (HEAD_DIM, HEAD_DIM), 0)
    ci = lax.broadcasted_iota(jnp.int32, (HEAD_DIM, HEAD_DIM), 1)
    ones_full = jnp.ones((HEAD_DIM, HEAD_DIM), BF16)
    ones_halves = jnp.where((ri < C_SUB_DIM) == (ci < C_SUB_DIM), 1.0, 0.0).astype(BF16)

    def head_sums(x, ones):
        hi, lo = _split_bf16(x * x)
        return _dot(hi, ones) + _dot(lo, ones)

    def norm_full(x):
        return x * lax.rsqrt(head_sums(x, ones_full) * (1.0 / HEAD_DIM) + EPS)

    def norm_halves(x):
        return x * lax.rsqrt(head_sums(x, ones_halves) * (1.0 / C_SUB_DIM) + EPS)

    def rope(x, cos, sin, dim):
        quarter = dim // 4
        first = (lane % (dim // 2)) < quarter
        rot = jnp.where(first, pltpu.roll(x, HEAD_DIM - quarter, 1), pltpu.roll(x, quarter, 1))
        return x * cos + rot * sin

    cosa, sina = cosa_ref[...], sina_ref[...]
    cosc, sinc = cosc_ref[...], sinc_ref[...]
    scale_ab = HEAD_DIM ** -0.5
    scale_c = C_SUB_DIM ** -0.5 * math.log2(math.e)
    for blk in range(ABC_BLOCKS):
        sl = slice(blk * HEAD_DIM, (blk + 1) * HEAD_DIM)
        if A_V <= blk < B_Q or B_V <= blk < C_Q or C_V <= blk:
            o_ref[0, :, sl] = p_ref[0, :, sl]
            continue
        x = p_ref[0, :, sl].astype(F32)
        if blk < A_K:
            y = rope(norm_full(x) * ga_ref[0:1, :], cosa, sina, HEAD_DIM) * scale_ab
        elif blk < A_V:
            y = rope(norm_full(x) * ga_ref[1:2, :], cosa, sina, HEAD_DIM)
        elif blk < B_K:
            y = norm_full(x) * gb_ref[0:1, :] * scale_ab
        elif blk < B_V:
            y = norm_full(x) * gb_ref[1:2, :]
        elif blk < C_K:
            y = rope(norm_halves(x) * gc_ref[0:1, :], cosc, sinc, C_SUB_DIM) * scale_c
        else:
            y = rope(norm_halves(x) * gc_ref[1:2, :], cosc, sinc, C_SUB_DIM)
        o_ref[0, :, sl] = y.astype(o_ref.dtype)


def qk_prep(proj, cosa, sina, cosc, sinc, ga, gb, gc, t=256):
    B, L, _ = proj.shape
    W = ABC_BLOCKS * HEAD_DIM
    tab = pl.BlockSpec((t, HEAD_DIM), lambda b, i: (i, 0))
    par = pl.BlockSpec((2, HEAD_DIM), lambda b, i: (0, 0))
    return pl.pallas_call(
        _qk_prep_kernel,
        out_shape=jax.ShapeDtypeStruct((B, L, W), BF16),
        grid=(B, L // t),
        in_specs=[pl.BlockSpec((1, t, W), lambda b, i: (b, i, 0)), tab, tab, tab, tab, par, par, par],
        out_specs=pl.BlockSpec((1, t, W), lambda b, i: (b, i, 0)),
        compiler_params=_cparams(("arbitrary", "arbitrary")),
        name="qk_prep",
    )(proj, cosa, sina, cosc, sinc, ga, gb, gc)


def _attn_a_kernel(sink_ref, q_ref, k_ref, v_ref, kc_ref, vc_ref, o_ref, *, tq):
    L = k_ref.shape[1]
    kv, i = pl.program_id(1), pl.program_id(2)
    win = tq + 2 * A_WINDOW
    start = pl.multiple_of(jnp.clip(i * tq - A_WINDOW, 0, L - win), A_WINDOW)
    k = k_ref[0, pl.ds(start, win), :]
    v = v_ref[0, pl.ds(start, win), :]
    kc, vc = kc_ref[0], vc_ref[0]
    qpos = i * tq + lax.broadcasted_iota(jnp.int32, (tq, win), 0)
    kpos = start + lax.broadcasted_iota(jnp.int32, (tq, win), 1)
    in_window = jnp.abs(kpos - qpos) <= A_WINDOW
    group = A_HEADS // A_KV_HEADS
    for g in range(group):
        sl = slice(g * HEAD_DIM, (g + 1) * HEAD_DIM)
        q = q_ref[0, :, sl]
        s = jnp.where(in_window, _dot_nt(q, k), NEG_INF)
        sc = _dot_nt(q, kc)
        sink = sink_ref[kv * group + g]
        m = jnp.maximum(jnp.maximum(jnp.max(s, axis=-1, keepdims=True), jnp.max(sc, axis=-1, keepdims=True)), sink)
        p = jnp.exp(s - m)
        pc = jnp.exp(sc - m)
        denom = jnp.sum(p, axis=-1, keepdims=True) + jnp.sum(pc, axis=-1, keepdims=True) + jnp.exp(sink - m)
        o = _dot(p.astype(BF16), v) + _dot(pc.astype(BF16), vc)
        o_ref[0, :, sl] = (o / denom).astype(o_ref.dtype)


def attn_a(qk, qkc, sink, tq=512):
    B, L, _ = qk.shape
    C = qkc.shape[1]
    gw = (A_HEADS // A_KV_HEADS) * HEAD_DIM
    return pl.pallas_call(
        functools.partial(_attn_a_kernel, tq=tq),
        out_shape=jax.ShapeDtypeStruct((B, L, A_HEADS * HEAD_DIM), BF16),
        grid=(B, A_KV_HEADS, L // tq),
        in_specs=[
            pl.BlockSpec(memory_space=pltpu.SMEM),
            pl.BlockSpec((1, tq, gw), lambda b, kv, i: (b, i, kv)),
            pl.BlockSpec((1, L, HEAD_DIM), lambda b, kv, i: (b, 0, A_K + kv)),
            pl.BlockSpec((1, L, HEAD_DIM), lambda b, kv, i: (b, 0, A_V + kv)),
            pl.BlockSpec((1, C, HEAD_DIM), lambda b, kv, i: (b, 0, A_K + kv)),
            pl.BlockSpec((1, C, HEAD_DIM), lambda b, kv, i: (b, 0, A_V + kv)),
        ],
        out_specs=pl.BlockSpec((1, tq, gw), lambda b, kv, i: (b, i, kv)),
        compiler_params=_cparams(("arbitrary", "arbitrary", "arbitrary")),
        name="attn_a",
    )(sink, qk, qk, qk, qkc, qkc)


NA_QROWS = 8


def _na_bias_tables(rpb, R):
    W = GRID_W
    H = rpb.shape[0]
    wq = jnp.arange(W)
    cs = jnp.clip(wq - NA_COLS // 2, 0, W - NA_COLS)
    colmask = (wq[None, :] >= cs[:, None]) & (wq[None, :] < cs[:, None] + NA_COLS)
    rel_c = jnp.clip(wq[None, :] - wq[:, None] + (NA_COLS - 1), 0, 2 * NA_COLS - 2)
    per_row = jnp.where(colmask[None, None], rpb.astype(F32)[:, :, rel_c], NEG_INF)
    nq, nk = NA_QROWS, 2 * NA_QROWS
    tables = []
    for r0, ks in ((0, 0), (nq, nq - NA_ROWS // 2), (R - nq, R - nk)):
        qr = r0 + jnp.arange(nq)
        kr = ks + jnp.arange(nk)
        rs = jnp.clip(qr - NA_ROWS // 2, 0, R - NA_ROWS)
        valid = (kr[None, :] >= rs[:, None]) & (kr[None, :] < rs[:, None] + NA_ROWS)
        dr = jnp.clip(kr[None, :] - qr[:, None] + (NA_ROWS - 1), 0, 2 * NA_ROWS - 2)
        t = jnp.where(valid[None, :, :, None, None], per_row[:, dr], NEG_INF)
        tables.append(jnp.transpose(t, (0, 1, 3, 2, 4)).reshape(H, nq * W, nk * W))
    return jnp.stack(tables)


def _attn_b_kernel(q_ref, k_ref, v_ref, kc_ref, vc_ref, bias_ref, o_ref):
    L = k_ref.shape[1]
    R = L // GRID_W
    i = pl.program_id(2)
    nk = 2 * NA_QROWS
    krow = jnp.clip(i * NA_QROWS - NA_ROWS // 2, 0, R - nk)
    start = pl.multiple_of(krow * GRID_W, (NA_ROWS // 2) * GRID_W)
    k = k_ref[0, pl.ds(start, nk * GRID_W), :]
    v = v_ref[0, pl.ds(start, nk * GRID_W), :]
    q = q_ref[0]
    s = _dot_nt(q, k) + bias_ref[0, 0]
    sc = _dot_nt(q, kc_ref[0])
    m = jnp.maximum(jnp.max(s, axis=-1, keepdims=True), jnp.max(sc, axis=-1, keepdims=True))
    p = jnp.exp(s - m)
    pc = jnp.exp(sc - m)
    denom = jnp.sum(p, axis=-1, keepdims=True) + jnp.sum(pc, axis=-1, keepdims=True)
    o = _dot(p.astype(BF16), v) + _dot(pc.astype(BF16), vc_ref[0])
    o_ref[0] = (o / denom).astype(o_ref.dtype)


def attn_b(qk, qkc, bias):
    B, L, _ = qk.shape
    C = qkc.shape[1]
    tq = NA_QROWS * GRID_W
    nblk = L // tq

    def bias_idx(h, b, i):
        return (jnp.where(i == 0, 0, jnp.where(i == nblk - 1, 2, 1)), h, 0, 0)

    return pl.pallas_call(
        _attn_b_kernel,
        out_shape=jax.ShapeDtypeStruct((B, L, B_HEADS * HEAD_DIM), BF16),
        grid=(B_HEADS, B, nblk),
        in_specs=[
            pl.BlockSpec((1, tq, HEAD_DIM), lambda h, b, i: (b, i, B_Q + h)),
            pl.BlockSpec((1, L, HEAD_DIM), lambda h, b, i: (b, 0, B_K + h)),
            pl.BlockSpec((1, L, HEAD_DIM), lambda h, b, i: (b, 0, B_V + h)),
            pl.BlockSpec((1, C, HEAD_DIM), lambda h, b, i: (b, 0, B_K + h)),
            pl.BlockSpec((1, C, HEAD_DIM), lambda h, b, i: (b, 0, B_V + h)),
            pl.BlockSpec((1, 1, tq, 2 * tq), bias_idx),
        ],
        out_specs=pl.BlockSpec((1, tq, HEAD_DIM), lambda h, b, i: (b, i, h)),
        compiler_params=_cparams(("arbitrary", "arbitrary", "arbitrary")),
        name="attn_b",
    )(qk, qk, qk, qkc, qkc, bias)


def _diff_lambda(lambda_params, lam_init):
    lp = lambda_params.astype(F32)
    return jnp.exp(jnp.sum(lp[0] * lp[1])) - jnp.exp(jnp.sum(lp[2] * lp[3])) + lam_init


ATTN_C_SAFE_SCORE_BOUND = 40.0


def _attn_c_kernel(lam_ref, q_ref, k_ref, v_ref, kc_ref, vc_ref, g_ref, o_ref, vt_ref, vct_ref, *scratch,
                   tk, coef, online):
    if online:
        m_ref, l_ref, acc_ref = scratch
        m_ref[...] = jnp.full(m_ref.shape, NEG_INF, F32)
    else:
        l_ref, acc_ref = scratch
    nk = vt_ref.shape[0]

    @pl.when(pl.program_id(2) == 0)
    def _():
        for j in range(nk):
            vt_ref[j] = v_ref[0, j * tk:(j + 1) * tk, :].astype(F32).T.astype(BF16)
        vct_ref[...] = vc_ref[0].astype(F32).T.astype(BF16)

    q = q_ref[0]
    lane = lax.broadcasted_iota(jnp.int32, q.shape, 1)
    zero = jnp.zeros_like(q)
    qs = (jnp.where(lane < C_SUB_DIM, q, zero), jnp.where(lane < C_SUB_DIM, zero, q))

    l_ref[...] = jnp.zeros(l_ref.shape, F32)
    acc_ref[...] = jnp.zeros(acc_ref.shape, F32)

    def accumulate(kblk, vtblk):
        for s_idx in range(2):
            s = _dot_nt(kblk, qs[s_idx])
            if online:
                m_old = m_ref[s_idx]
                m_new = jnp.maximum(m_old, jnp.max(s, axis=0, keepdims=True))
                alpha = jnp.exp2(m_old - m_new)
                p = jnp.exp2(s - m_new)
                l_ref[s_idx] = alpha * l_ref[s_idx] + jnp.sum(p, axis=0, keepdims=True)
                acc_ref[s_idx] = alpha * acc_ref[s_idx] + _dot(vtblk, p.astype(BF16))
                m_ref[s_idx] = m_new
            else:
                p = jnp.exp2(s)
                l_ref[s_idx] += jnp.sum(p, axis=0, keepdims=True)
                acc_ref[s_idx] += _dot(vtblk, p.astype(BF16))

    def body(j, carry):
        accumulate(k_ref[0, pl.ds(pl.multiple_of(j * tk, tk), tk), :], vt_ref[j])
        return carry

    lax.fori_loop(0, nk, body, 0, unroll=4)
    accumulate(kc_ref[0], vct_ref[...])

    o = acc_ref[0] / l_ref[0] - lam_ref[0] * (acc_ref[1] / l_ref[1])
    y = o * lax.rsqrt(jnp.mean(o * o, axis=0, keepdims=True) + EPS) * g_ref[...] * coef
    o_ref[0] = y.T.astype(o_ref.dtype)


def attn_c(qk, qkc, lam, subln, coef, online, tq=1024, tk=512):
    B, L, _ = qk.shape
    C = qkc.shape[1]
    H, dv = C_HEADS, HEAD_DIM
    stats = [pltpu.VMEM((2, 1, tq), F32)] * (2 if online else 1)
    return pl.pallas_call(
        functools.partial(_attn_c_kernel, tk=tk, coef=coef, online=online),
        out_shape=jax.ShapeDtypeStruct((B, L, H * dv), BF16),
        grid=(B, H, L // tq),
        in_specs=[
            pl.BlockSpec(memory_space=pltpu.SMEM),
            pl.BlockSpec((1, tq, HEAD_DIM), lambda b, h, i: (b, i, C_Q + h)),
            pl.BlockSpec((1, L, HEAD_DIM), lambda b, h, i: (b, 0, C_K + h)),
            pl.BlockSpec((1, L, HEAD_DIM), lambda b, h, i: (b, 0, C_V + h)),
            pl.BlockSpec((1, C, HEAD_DIM), lambda b, h, i: (b, 0, C_K + h)),
            pl.BlockSpec((1, C, HEAD_DIM), lambda b, h, i: (b, 0, C_V + h)),
            pl.BlockSpec((dv, 1), lambda b, h, i: (0, 0)),
        ],
        out_specs=pl.BlockSpec((1, tq, dv), lambda b, h, i: (b, i, h)),
        scratch_shapes=[pltpu.VMEM((L // tk, dv, tk), BF16), pltpu.VMEM((dv, C), BF16)]
        + stats + [pltpu.VMEM((2, dv, tq), F32)],
        compiler_params=_cparams(("arbitrary", "arbitrary", "arbitrary")),
        name="attn_c_online" if online else "attn_c",
    )(lam, qk, qk, qk, qkc, qkc, subln.reshape(dv, 1).astype(F32))


def _retention_kernel(ld_ref, q_ref, k_ref, v_ref, g_ref, qc_ref, kc_ref, vc_ref, gc_ref, *rest, with_ctx):
    if with_ctx:
        y_ref, yc_ref, of_ref, ob_ref, ocf_ref = rest
    else:
        y_ref, of_ref, ob_ref = rest
        yc_ref = ocf_ref = None
    c = RET_CHUNK
    L, C = q_ref.shape[1], qc_ref.shape[1]
    h = pl.program_id(1)
    scale = HEAD_DIM ** -0.5
    row = lax.broadcasted_iota(jnp.int32, (c, c), 0).astype(F32)
    col = lax.broadcasted_iota(jnp.int32, (c, c), 1).astype(F32)
    rowp = lax.broadcasted_iota(jnp.int32, (c, HEAD_DIM), 0).astype(F32)

    def decays(lg, reverse):
        if reverse:
            rel, q_exp, k_exp = col - row, c - rowp, rowp
        else:
            rel, q_exp, k_exp = row - col, rowp + 1.0, (c - 1.0) - rowp
        intra = jnp.where(rel >= 0, jnp.exp(lg * jnp.maximum(rel, 0.0)), 0.0)
        return intra, jnp.exp(lg * q_exp), jnp.exp(lg * k_exp), jnp.exp(lg * c)

    def step(S, q, k, v, dec, want_out):
        intra, q_decay, k_decay, chunk_decay = dec
        o = None
        if want_out:
            qs = (q.astype(F32) * scale).astype(BF16)
            s = _dot_nt(qs, k) * intra
            o = _dot(s.astype(BF16), v) + _dot(qs, S.astype(BF16)) * q_decay
        kd_t = (k.astype(F32) * k_decay).T.astype(BF16)
        return S * chunk_decay + _dot(kd_t, v), o

    def gated(o, g):
        g = g.astype(F32)
        y = o * lax.rsqrt(jnp.mean(o * o, axis=-1, keepdims=True) + EPS)
        return y * (g * jax.nn.sigmoid(g))

    def lat(ref, j):
        return ref[0, pl.ds(pl.multiple_of(j * c, c), c), :]

    def ctx(ref, j):
        return ref[0, j * c:(j + 1) * c, :]

    dec_f = decays(ld_ref[0, h], False)
    dec_b = decays(ld_ref[1, h], True)
    S_f = jnp.zeros((HEAD_DIM, HEAD_DIM), F32)
    for j in range(C // c):
        S_f, o = step(S_f, ctx(qc_ref, j), ctx(kc_ref, j), ctx(vc_ref, j), dec_f, with_ctx)
        if with_ctx:
            ocf_ref[j * c:(j + 1) * c, :] = o
    S_b = jnp.zeros((HEAD_DIM, HEAD_DIM), F32)
    for j in reversed(range(C // c)):
        S_b, o = step(S_b, ctx(qc_ref, j), ctx(kc_ref, j), ctx(vc_ref, j), dec_b, with_ctx)
        if with_ctx:
            yc_ref[0, j * c:(j + 1) * c, :] = gated(o + ocf_ref[j * c:(j + 1) * c, :], ctx(gc_ref, j)).astype(yc_ref.dtype)

    n = L // c

    def both(t, carry):
        S_f, S_b = carry
        S_f, o_f = step(S_f, lat(q_ref, t), lat(k_ref, t), lat(v_ref, t), dec_f, True)
        of_ref[pl.ds(pl.multiple_of(t * c, c), c), :] = o_f
        j = n - 1 - t
        S_b, o_b = step(S_b, lat(q_ref, j), lat(k_ref, j), lat(v_ref, j), dec_b, True)
        ob_ref[pl.ds(pl.multiple_of(j * c, c), c), :] = o_b
        return S_f, S_b

    lax.fori_loop(0, n, both, (S_f, S_b), unroll=2)

    def finish(j, carry):
        sl = pl.ds(pl.multiple_of(j * c, c), c)
        y_ref[0, sl, :] = gated(of_ref[sl, :] + ob_ref[sl, :], lat(g_ref, j)).astype(y_ref.dtype)
        return carry

    lax.fori_loop(0, n, finish, 0, unroll=2)


def retention(proj, projc, log_decay, with_ctx):
    B, L, _ = proj.shape
    C = projc.shape[1]

    def lat(blk):
        return pl.BlockSpec((1, L, HEAD_DIM), lambda b, h: (b, 0, blk + h))

    def ctx(blk):
        return pl.BlockSpec((1, C, HEAD_DIM), lambda b, h: (b, 0, blk + h))

    out_shape = [jax.ShapeDtypeStruct((B, L, D_HEADS * HEAD_DIM), BF16)]
    out_specs = [pl.BlockSpec((1, L, HEAD_DIM), lambda b, h: (b, 0, h))]
    scratch = [pltpu.VMEM((L, HEAD_DIM), F32), pltpu.VMEM((L, HEAD_DIM), F32)]
    if with_ctx:
        out_shape.append(jax.ShapeDtypeStruct((B, C, D_HEADS * HEAD_DIM), BF16))
        out_specs.append(pl.BlockSpec((1, C, HEAD_DIM), lambda b, h: (b, 0, h)))
        scratch.append(pltpu.VMEM((C, HEAD_DIM), F32))
    outs = pl.pallas_call(
        functools.partial(_retention_kernel, with_ctx=with_ctx),
        out_shape=out_shape,
        grid=(B, D_HEADS),
        in_specs=[pl.BlockSpec(memory_space=pltpu.SMEM),
                  lat(D_Q), lat(D_K), lat(D_V), lat(D_G), ctx(D_Q), ctx(D_K), ctx(D_V), ctx(D_G)],
        out_specs=out_specs,
        scratch_shapes=scratch,
        compiler_params=_cparams(("arbitrary", "arbitrary")),
        name="retention_ctx" if with_ctx else "retention",
    )(log_decay.astype(F32), proj, proj, proj, proj, projc, projc, projc, projc)
    return (outs[0], outs[1]) if with_ctx else (outs[0], None)


def _ctx_attn_kernel(sink_ref, lam_ref, qk_ref, g_ref, o_ref, *, coef):
    def blk(idx):
        return qk_ref[0, :, idx * HEAD_DIM:(idx + 1) * HEAD_DIM]

    def softmax_out(q, k, v, sink=None, exp=jnp.exp):
        s = _dot_nt(q, k)
        m = jnp.max(s, axis=-1, keepdims=True)
        if sink is not None:
            m = jnp.maximum(m, sink)
        p = exp(s - m)
        denom = jnp.sum(p, axis=-1, keepdims=True)
        if sink is not None:
            denom = denom + jnp.exp(sink - m)
        return _dot(p.astype(BF16), v) / denom

    group = A_HEADS // A_KV_HEADS
    for h in range(A_HEADS):
        o = softmax_out(blk(A_Q + h), blk(A_K + h // group), blk(A_V + h // group), sink_ref[h])
        o_ref[0, :, h * HEAD_DIM:(h + 1) * HEAD_DIM] = o.astype(o_ref.dtype)
    for h in range(B_HEADS):
        o = softmax_out(blk(B_Q + h), blk(B_K + h), blk(B_V + h))
        o_ref[0, :, (A_HEADS + h) * HEAD_DIM:(A_HEADS + h + 1) * HEAD_DIM] = o.astype(o_ref.dtype)
    lane = lax.broadcasted_iota(jnp.int32, (qk_ref.shape[1], HEAD_DIM), 1)
    for h in range(C_HEADS):
        q, k, v = blk(C_Q + h), blk(C_K + h), blk(C_V + h)
        zero = jnp.zeros_like(q)
        o1 = softmax_out(jnp.where(lane < C_SUB_DIM, q, zero), k, v, exp=jnp.exp2)
        o2 = softmax_out(jnp.where(lane < C_SUB_DIM, zero, q), k, v, exp=jnp.exp2)
        o = o1 - lam_ref[0] * o2
        y = o * lax.rsqrt(jnp.mean(o * o, axis=-1, keepdims=True) + EPS) * g_ref[...] * coef
        base = A_HEADS + B_HEADS + h
        o_ref[0, :, base * HEAD_DIM:(base + 1) * HEAD_DIM] = y.astype(o_ref.dtype)


def ctx_attn(qkc, sink, lam, subln, coef):
    B, C, W = qkc.shape
    n_out = (A_HEADS + B_HEADS + C_HEADS) * HEAD_DIM
    return pl.pallas_call(
        functools.partial(_ctx_attn_kernel, coef=coef),
        out_shape=jax.ShapeDtypeStruct((B, C, n_out), BF16),
        grid=(B,),
        in_specs=[
            pl.BlockSpec(memory_space=pltpu.SMEM),
            pl.BlockSpec(memory_space=pltpu.SMEM),
            pl.BlockSpec((1, C, W), lambda b: (b, 0, 0)),
            pl.BlockSpec((1, HEAD_DIM), lambda b: (0, 0)),
        ],
        out_specs=pl.BlockSpec((1, C, n_out), lambda b: (b, 0, 0)),
        compiler_params=_cparams(("arbitrary",)),
        name="ctx_attn",
    )(sink, lam, qkc, subln.reshape(1, HEAD_DIM).astype(F32))


def _out_proj_kernel(*refs, n_parts):
    parts = refs[:n_parts]
    x_ref, gate_ref, w_ref, o_ref = refs[n_parts:]
    mix = jnp.concatenate([p[0] for p in parts], axis=-1)
    o_ref[0] = x_ref[0] + gate_ref[0] * _dot(mix, w_ref[...])


def out_proj(parts, x, gate, w, tm, tn=512):
    B, L, D = x.shape
    part_specs = [pl.BlockSpec((1, tm, p.shape[2]), lambda b, i, j: (b, i, 0)) for p in parts]
    return pl.pallas_call(
        functools.partial(_out_proj_kernel, n_parts=len(parts)),
        out_shape=jax.ShapeDtypeStruct((B, L, D), F32),
        grid=(B, L // tm, D // tn),
        in_specs=part_specs + [
            pl.BlockSpec((1, tm, tn), lambda b, i, j: (b, i, j)),
            pl.BlockSpec((1, 1, tn), lambda b, i, j: (b, 0, j)),
            pl.BlockSpec((w.shape[0], tn), lambda b, i, j: (0, j)),
        ],
        out_specs=pl.BlockSpec((1, tm, tn), lambda b, i, j: (b, i, j)),
        compiler_params=_cparams(("arbitrary", "arbitrary", "arbitrary")),
        name="out_proj",
    )(*parts, x, gate, w)


def _ffn_prep_kernel(*refs, n_lat_tiles):
    if n_lat_tiles is None:
        x_ref, gain_ref, shift_ref, scale_ref, r_ref, h_ref, id_ref, w_ref = refs
        x = x_ref[...]
    else:
        xl_ref, xc_ref, gain_ref, shift_ref, scale_ref, r_ref, h_ref, id_ref, w_ref = refs
        x = jnp.where(pl.program_id(0) < n_lat_tiles, xl_ref[...], xc_ref[...])
    h = _modulated(x, gain_ref[...], shift_ref[0], scale_ref[0])
    h_ref[...] = h

    logits = _dot3(h, r_ref[...])
    lane = lax.broadcasted_iota(jnp.int32, logits.shape, 1)
    big = jnp.int32(logits.shape[1])

    def first_argmax(vals):
        top = jnp.max(vals, axis=-1, keepdims=True)
        return top, jnp.min(jnp.where(vals == top, lane, big), axis=-1, keepdims=True)

    is_group = lane < N_GROUPS
    g_top, g_idx = first_argmax(jnp.where(is_group, logits, NEG_INF))
    g_val = 1.0 / jnp.sum(jnp.where(is_group, jnp.exp(logits - g_top), 0.0), axis=-1, keepdims=True)
    lo = N_GROUPS + EXPERTS_PER_GROUP * g_idx
    e_logits = jnp.where((lane >= lo) & (lane < lo + EXPERTS_PER_GROUP), logits, NEG_INF)
    v1, i1 = first_argmax(e_logits)
    v2, i2 = first_argmax(jnp.where(lane == i1, NEG_INF, e_logits))
    e21 = jnp.exp(v2 - v1)
    w1 = g_val / (1.0 + e21)
    w2 = g_val * e21 / (1.0 + e21)
    id_ref[...] = jnp.where(lane == 0, i1 - N_GROUPS, jnp.where(lane == 1, i2 - N_GROUPS, 0))
    w_ref[...] = jnp.where(lane == 0, w1, jnp.where(lane == 1, w2, 0.0))


def ffn_prep(x, xc, gain, shift, scale, r_cat, t=256):
    B, L, D = x.shape
    n_lat = B * L // t
    lat_per_seg = L // t
    if xc is None:
        T = B * L
        xs = [x.reshape(B * L, D)]
        x_specs = [pl.BlockSpec((t, D), lambda i: (i, 0))]
        seg = lambda i: (i // lat_per_seg, 0, 0)
        n_lat_tiles = None
    else:
        C = xc.shape[1]
        T = B * (L + C)
        xs = [x.reshape(B * L, D), xc.reshape(B * C, D)]
        x_specs = [pl.BlockSpec((t, D), lambda i: (jnp.minimum(i, n_lat - 1), 0)),
                   pl.BlockSpec((t, D), lambda i: (jnp.maximum(i - n_lat, 0), 0))]
        seg = lambda i: (jnp.minimum(i // lat_per_seg, B), 0, 0)
        n_lat_tiles = n_lat
    return pl.pallas_call(
        functools.partial(_ffn_prep_kernel, n_lat_tiles=n_lat_tiles),
        out_shape=[jax.ShapeDtypeStruct((T, D), F32),
                   jax.ShapeDtypeStruct((T, 128), jnp.int32),
                   jax.ShapeDtypeStruct((T, 128), F32)],
        grid=(T // t,),
        in_specs=x_specs + [
            pl.BlockSpec((1, D), lambda i: (0, 0)),
            pl.BlockSpec((1, 1, D), seg),
            pl.BlockSpec((1, 1, D), seg),
            pl.BlockSpec((D, 128), lambda i: (0, 0)),
        ],
        out_specs=[pl.BlockSpec((t, D), lambda i: (i, 0)),
                   pl.BlockSpec((t, 128), lambda i: (i, 0)),
                   pl.BlockSpec((t, 128), lambda i: (i, 0))],
        compiler_params=_cparams(("arbitrary",)),
        name="ffn_prep",
    )(*xs, gain, shift, scale, r_cat)


def _route(ids, tm):
    A = ids.shape[0] * TOP_K
    flat_e = ids.reshape(A)
    onehot = (flat_e[:, None] == jnp.arange(N_EXPERTS, dtype=jnp.int32)[None, :]).astype(jnp.int32)
    csum = jnp.cumsum(onehot, axis=0)
    rank = jnp.sum(csum * onehot, axis=1) - 1
    counts = csum[-1]
    pcounts = (counts + tm - 1) // tm * tm
    pends = jnp.cumsum(pcounts)
    pstarts = pends - pcounts
    dest = jnp.sum(onehot * pstarts[None, :], axis=1) + rank
    NB = (A + N_EXPERTS * (tm - 1)) // tm
    row_tok = jnp.zeros((NB * tm,), jnp.int32).at[dest].set(jnp.arange(A, dtype=jnp.int32) // TOP_K)
    block_e = jnp.clip(jnp.searchsorted(pends, jnp.arange(NB, dtype=jnp.int32) * tm, side='right'), 0, N_EXPERTS - 1)
    return dest.astype(jnp.int32), row_tok, block_e.astype(jnp.int32), (pends[-1:] // tm).astype(jnp.int32)


def _row_copy(src, src_row, dst, dst_row, sem):
    return pltpu.make_async_copy(src.at[pl.ds(src_row, 1), :], dst.at[pl.ds(dst_row, 1), :], sem)


def _expert_kernel(be_ref, nused_ref, tok_ref, h_ref, wg_ref, wu_ref, wd_ref, o_ref, xbuf, sem):
    i = pl.program_id(0)
    tm = o_ref.shape[0]
    n_used = nused_ref[0]

    def row(r, blk, slot):
        return _row_copy(h_ref, tok_ref[blk * tm + r], xbuf.at[slot], r, sem.at[slot])

    def gather(blk, slot):
        def issue(r, carry):
            row(r, blk, slot).start()
            return carry
        lax.fori_loop(0, tm, issue, 0, unroll=8)

    @pl.when(i == 0)
    def _():
        gather(0, 0)

    @pl.when(i + 1 < n_used)
    def _():
        gather(i + 1, (i + 1) % 2)

    @pl.when(i < n_used)
    def _():
        slot = i % 2

        def drain(r, carry):
            row(r, i, slot).wait()
            return carry
        lax.fori_loop(0, tm, drain, 0, unroll=8)
        x = xbuf[slot].astype(BF16)
        a = _dot(x, wg_ref[0])
        u = _dot(x, wu_ref[0])
        hmid = (a * jax.nn.sigmoid(a) * u).astype(BF16)
        o_ref[...] = _dot(hmid, wd_ref[0])

    @pl.when(i >= n_used)
    def _():
        o_ref[...] = jnp.zeros(o_ref.shape, o_ref.dtype)


def experts(block_e, n_used, row_tok, h, w_gate, w_up, w_down, tm):
    P = row_tok.shape[0]
    _, D, De = w_gate.shape
    w_idx = lambda i, be, nu, tok: (be[jnp.minimum(i, nu[0] - 1)], 0, 0)
    return pl.pallas_call(
        _expert_kernel,
        out_shape=jax.ShapeDtypeStruct((P, D), F32),
        grid_spec=pltpu.PrefetchScalarGridSpec(
            num_scalar_prefetch=3,
            grid=(P // tm,),
            in_specs=[
                pl.BlockSpec(memory_space=pl.ANY),
                pl.BlockSpec((1, D, De), w_idx),
                pl.BlockSpec((1, D, De), w_idx),
                pl.BlockSpec((1, De, D), w_idx),
            ],
            out_specs=pl.BlockSpec((tm, D), lambda i, be, nu, tok: (i, 0)),
            scratch_shapes=[pltpu.VMEM((2, tm, D), F32), pltpu.SemaphoreType.DMA((2,))],
        ),
        compiler_params=_cparams(("arbitrary",)),
        name="moe_experts",
    )(block_e, n_used, row_tok, h, w_gate, w_up, w_down)


def _combine_kernel(dest_ref, x_ref, gate_ref, w_ref, y_ref, o_ref, buf, sem, *, tok_offset):
    t, D = x_ref.shape
    base = (tok_offset + pl.program_id(0) * t) * TOP_K

    def issue(r, carry):
        for k in range(TOP_K):
            _row_copy(y_ref, dest_ref[base + r * TOP_K + k], buf.at[k], r, sem).start()
        return carry

    def drain(r, carry):
        _row_copy(y_ref, 0, buf.at[0], 0, sem).wait()
        return carry

    lax.fori_loop(0, t, issue, 0)
    lax.fori_loop(0, t * TOP_K, drain, 0)
    w = w_ref[...]
    y = w[:, 0:1] * buf[0] + w[:, 1:2] * buf[1]
    o_ref[...] = x_ref[...] + gate_ref[0] * y


def combine(dest, x, gate, rw, ys, tok_offset, t=256):
    B, L, D = x.shape
    per_seg = L // t
    woff = tok_offset // t
    out = pl.pallas_call(
        functools.partial(_combine_kernel, tok_offset=tok_offset),
        out_shape=jax.ShapeDtypeStruct((B * L, D), F32),
        grid_spec=pltpu.PrefetchScalarGridSpec(
            num_scalar_prefetch=1,
            grid=(B * L // t,),
            in_specs=[
                pl.BlockSpec((t, D), lambda i, d: (i, 0)),
                pl.BlockSpec((1, 1, D), lambda i, d: (i // per_seg, 0, 0)),
                pl.BlockSpec((t, 128), lambda i, d: (woff + i, 0)),
                pl.BlockSpec(memory_space=pl.ANY),
            ],
            out_specs=pl.BlockSpec((t, D), lambda i, d: (i, 0)),
            scratch_shapes=[pltpu.VMEM((TOP_K, t, D), F32), pltpu.SemaphoreType.DMA(())],
        ),
        compiler_params=_cparams(("arbitrary",)),
        name="moe_combine",
    )(dest, x.reshape(B * L, D), gate, rw, ys)
    return out.reshape(B, L, D)


MOE_TM = 256


def kernel(x, c, ctx, c_ctx, w_mod, b_mod, norm_mix, norm_ffn, w_in, w_out, qk_norm_a, sink_a, qk_norm_b, rpb_b,
           qk_norm_c, lambda_c, subln_c, ret_log_decay, router_group, router_expert, w_gate, w_up, w_down):
    B, L, D = x.shape
    C = ctx.shape[1]
    depth = w_mod.shape[0]

    cvec = jnp.zeros((8, D), F32).at[:B].set(c).at[B].set(c_ctx)
    mod = mod_vectors(cvec, w_mod, b_mod).reshape(depth, 8, 6, D)

    cosa, sina = _rope_tables(L, HEAD_DIM)
    cosc, sinc = _rope_tables(L, C_SUB_DIM)
    one_tab = jnp.ones((C, HEAD_DIM), F32)
    zero_tab = jnp.zeros((C, HEAD_DIM), F32)

    xc = ctx
    for l in range(depth):
        with_ctx = l < depth - 1
        lat = lambda j: mod[l, :B, j][:, None, :]
        cx = lambda j: jnp.broadcast_to(mod[l, B, j][None, None, :], (B, 1, D))
        w_in_l = w_in[l].astype(BF16)
        w_out_l = w_out[l].astype(BF16)
        gain_mix = norm_mix[l][None, :]

        proj = in_proj(x, gain_mix, lat(0), lat(1), w_in_l, tm=512)
        projc = in_proj(xc, gain_mix, cx(0), cx(1), w_in_l, tm=C)
        gain_c = jnp.tile(qk_norm_c[l], (1, HEAD_DIM // C_SUB_DIM))
        qk = qk_prep(proj, cosa, sina, cosc, sinc, qk_norm_a[l], qk_norm_b[l], gain_c)
        qkc = qk_prep(projc, one_tab, zero_tab, one_tab, zero_tab, qk_norm_a[l], qk_norm_b[l], gain_c)

        lam_init = 0.8 - 0.6 * math.exp(-0.3 * l)
        lam = _diff_lambda(lambda_c[l], lam_init).reshape(1)
        out_a = attn_a(qk, qkc, sink_a[l])
        out_b = attn_b(qk, qkc, _na_bias_tables(rpb_b[l], L // GRID_W))
        score_bound = (C_SUB_DIM ** 0.5) * jnp.max(jnp.abs(qk_norm_c[l, 0])) * jnp.max(jnp.abs(qk_norm_c[l, 1]))
        attn_c_l = functools.partial(attn_c, qk, qkc, lam, subln_c[l], 1.0 - lam_init)
        out_c = lax.cond(score_bound <= ATTN_C_SAFE_SCORE_BOUND,
                         lambda: attn_c_l(online=False), lambda: attn_c_l(online=True))
        out_d, out_dc = retention(proj, projc, ret_log_decay[l], with_ctx)

        x = out_proj([out_a, out_b, out_c, out_d], x, lat(2), w_out_l, tm=1024)
        if with_ctx:
            out_abc_c = ctx_attn(qkc, sink_a[l], lam, subln_c[l], 1.0 - lam_init)
            xc = out_proj([out_abc_c, out_dc], xc, cx(2), w_out_l, tm=C)

        r_cat = jnp.zeros((D, 128), F32).at[:, :N_GROUPS].set(router_group[l])
        r_cat = r_cat.at[:, N_GROUPS:N_GROUPS + N_EXPERTS].set(router_expert[l])
        if with_ctx:
            shift = jnp.concatenate([lat(3), cx(3)[:1]], axis=0)
            scale = jnp.concatenate([lat(4), cx(4)[:1]], axis=0)
        else:
            shift, scale = lat(3), lat(4)
        h2, rid, rw = ffn_prep(x, xc if with_ctx else None, norm_ffn[l][None, :], shift, scale, r_cat)
        dest, row_tok, block_e, n_used = _route(rid[:, :TOP_K], MOE_TM)
        ys = experts(block_e, n_used, row_tok, h2,
                     w_gate[l].astype(BF16), w_up[l].astype(BF16), w_down[l].astype(BF16), MOE_TM)
        x = combine(dest, x, lat(5), rw, ys, 0)
        if with_ctx:
            xc = combine(dest, xc, cx(5), rw, ys, B * L)
    return x
```

```python
import functools
import math

import jax
import jax.numpy as jnp
import numpy as np
from jax import lax
from jax.experimental import pallas as pl
from jax.experimental.pallas import tpu as pltpu

F32 = jnp.float32
BF16 = jnp.bfloat16

HEAD_DIM = 128
GRID_W = 64
A_HEADS, A_KV_HEADS, A_WINDOW = 8, 2, 128
B_HEADS, NA_ROWS, NA_COLS = 8, 8, 16
C_HEADS, C_SUB_DIM = 8, 64
D_HEADS = 8
RET_CHUNK = 256
ROPE_BASE = 10000.0
N_GROUPS, EXPERTS_PER_GROUP, TOP_K = 4, 8, 2
N_EXPERTS = N_GROUPS * EXPERTS_PER_GROUP
EPS = 1e-6
NEG_INF = -1e30

A_Q, A_K, A_V = 0, 8, 10
B_Q, B_K, B_V = 12, 20, 28
C_Q, C_K, C_V = 36, 44, 52
D_Q, D_K, D_V, D_G = 60, 68, 76, 84
ABC_BLOCKS = 60
IN_COLS = 92 * 128

VMEM_LIMIT = 56 * 1024 * 1024


def _cparams(sem, vmem=VMEM_LIMIT):
    return pltpu.CompilerParams(dimension_semantics=sem, vmem_limit_bytes=vmem)


def _dot(a, b):
    return jnp.dot(a, b, preferred_element_type=F32)


def _dot_nt(a, b):
    return lax.dot_general(a, b, (((1,), (1,)), ((), ())), preferred_element_type=F32)


def _split_bf16(x):
    hi = x.astype(BF16)
    lo = (x - hi.astype(F32)).astype(BF16)
    return hi, lo


def _dot3(x, w):
    xh, xl = _split_bf16(x)
    wh, wl = _split_bf16(w)
    return _dot(xh, wh) + _dot(xl, wh) + _dot(xh, wl)


def _mod_kernel(c_ref, w_ref, b_ref, o_ref):
    c = c_ref[...]
    x = c * jax.nn.sigmoid(c)
    o_ref[0] = _dot3(x, w_ref[0]) + b_ref[0]


def mod_vectors(cvec, w_mod, b_mod, tn=512):
    depth, D, N = w_mod.shape
    return pl.pallas_call(
        _mod_kernel,
        out_shape=jax.ShapeDtypeStruct((depth, 8, N), F32),
        grid=(depth, N // tn),
        in_specs=[
            pl.BlockSpec((8, D), lambda l, j: (0, 0)),
            pl.BlockSpec((1, D, tn), lambda l, j: (l, 0, j)),
            pl.BlockSpec((1, 1, tn), lambda l, j: (l, 0, j)),
        ],
        out_specs=pl.BlockSpec((1, 8, tn), lambda l, j: (l, 0, j)),
        compiler_params=_cparams(("arbitrary", "arbitrary")),
        name="mod_vectors",
    )(cvec, w_mod, b_mod.reshape(depth, 1, N))


def _modulated(x, gain, shift, scale):
    ms = jnp.mean(x * x, axis=-1, keepdims=True)
    y = x * lax.rsqrt(ms + EPS) * gain
    return y * (1.0 + scale) + shift


def _in_proj_kernel(x_ref, gain_ref, shift_ref, scale_ref, w_ref, o_ref, h_ref):
    @pl.when(pl.program_id(2) == 0)
    def _():
        h_ref[...] = _modulated(x_ref[0], gain_ref[...], shift_ref[0], scale_ref[0]).astype(BF16)

    o_ref[0] = _dot(h_ref[...], w_ref[0]).astype(o_ref.dtype)


def in_proj(x, gain, shift, scale, w, layer, tm, tn=512):
    B, L, D = x.shape
    N = w.shape[2]
    return pl.pallas_call(
        _in_proj_kernel,
        out_shape=jax.ShapeDtypeStruct((B, L, N), BF16),
        grid=(B, L // tm, N // tn),
        in_specs=[
            pl.BlockSpec((1, tm, D), lambda b, i, j: (b, i, 0)),
            pl.BlockSpec((1, D), lambda b, i, j: (0, 0)),
            pl.BlockSpec((1, 1, D), lambda b, i, j: (b, 0, 0)),
            pl.BlockSpec((1, 1, D), lambda b, i, j: (b, 0, 0)),
            pl.BlockSpec((1, D, tn), lambda b, i, j: (layer, 0, j)),
        ],
        out_specs=pl.BlockSpec((1, tm, tn), lambda b, i, j: (b, i, j)),
        scratch_shapes=[pltpu.VMEM((tm, D), BF16)],
        compiler_params=_cparams(("arbitrary", "arbitrary", "arbitrary")),
        name="in_proj",
    )(x, gain, shift, scale, w)


def _rope_tables(L, dim):
    t = jnp.arange(L)
    row = (t // GRID_W).astype(F32)
    col = (t % GRID_W).astype(F32)
    quarter = dim // 4
    inv = ROPE_BASE ** (-jnp.arange(quarter, dtype=F32) / quarter)
    ar = row[:, None] * inv[None, :]
    ac = col[:, None] * inv[None, :]
    ang = jnp.concatenate([ar, ar, ac, ac], axis=-1)
    sign = jnp.where((jnp.arange(dim) % (dim // 2)) < quarter, -1.0, 1.0).astype(F32)
    reps = HEAD_DIM // dim
    return jnp.tile(jnp.cos(ang), (1, reps)), jnp.tile(jnp.sin(ang) * sign[None, :], (1, reps))


def _qk_prep_kernel(p_ref, cosa_ref, sina_ref, cosc_ref, sinc_ref, ga_ref, gb_ref, gc_ref, o_ref):
    t = p_ref.shape[1]
    lane = lax.broadcasted_iota(jnp.int32, (t, HEAD_DIM), 1)

    ri = lax.broadcasted_iota(jnp.int32, (HEAD_DIM, HEAD_DIM), 0)
    ci = lax.broadcasted_iota(jnp.int32, (HEAD_DIM, HEAD_DIM), 1)
    ones_full = jnp.ones((HEAD_DIM, HEAD_DIM), BF16)
    ones_halves = jnp.where((ri < C_SUB_DIM) == (ci < C_SUB_DIM), 1.0, 0.0).astype(BF16)

    def head_sums(x, ones):
        hi, lo = _split_bf16(x * x)
        return _dot(hi, ones) + _dot(lo, ones)

    def norm_full(x):
        return x * lax.rsqrt(head_sums(x, ones_full) * (1.0 / HEAD_DIM) + EPS)

    def norm_halves(x):
        return x * lax.rsqrt(head_sums(x, ones_halves) * (1.0 / C_SUB_DIM) + EPS)

    def rope(x, cos, sin, dim):
        quarter = dim // 4
        first = (lane % (dim // 2)) < quarter
        rot = jnp.where(first, pltpu.roll(x, HEAD_DIM - quarter, 1), pltpu.roll(x, quarter, 1))
        return x * cos + rot * sin

    cosa, sina = cosa_ref[...], sina_ref[...]
    cosc, sinc = cosc_ref[...], sinc_ref[...]
    scale_ab = HEAD_DIM ** -0.5
    scale_c = C_SUB_DIM ** -0.5 * math.log2(math.e)
    for blk in range(ABC_BLOCKS):
        sl = slice(blk * HEAD_DIM, (blk + 1) * HEAD_DIM)
        if A_V <= blk < B_Q or B_V <= blk < C_Q or C_V <= blk:
            o_ref[0, :, sl] = p_ref[0, :, sl]
            continue
        x = p_ref[0, :, sl].astype(F32)
        if blk < A_K:
            y = rope(norm_full(x) * ga_ref[0:1, :], cosa, sina, HEAD_DIM) * scale_ab
        elif blk < A_V:
            y = rope(norm_full(x) * ga_ref[1:2, :], cosa, sina, HEAD_DIM)
        elif blk < B_K:
            y = norm_full(x) * gb_ref[0:1, :] * scale_ab
        elif blk < B_V:
            y = norm_full(x) * gb_ref[1:2, :]
        elif blk < C_K:
            y = rope(norm_halves(x) * gc_ref[0:1, :], cosc, sinc, C_SUB_DIM) * scale_c
        else:
            y = rope(norm_halves(x) * gc_ref[1:2, :], cosc, sinc, C_SUB_DIM)
        o_ref[0, :, sl] = y.astype(o_ref.dtype)


def qk_prep(proj, cosa, sina, cosc, sinc, ga, gb, gc, t=256):
    B, L, _ = proj.shape
    W = ABC_BLOCKS * HEAD_DIM
    tab = pl.BlockSpec((t, HEAD_DIM), lambda b, i: (i, 0))
    par = pl.BlockSpec((2, HEAD_DIM), lambda b, i: (0, 0))
    return pl.pallas_call(
        _qk_prep_kernel,
        out_shape=jax.ShapeDtypeStruct((B, L, W), BF16),
        grid=(B, L // t),
        in_specs=[pl.BlockSpec((1, t, W), lambda b, i: (b, i, 0)), tab, tab, tab, tab, par, par, par],
        out_specs=pl.BlockSpec((1, t, W), lambda b, i: (b, i, 0)),
        compiler_params=_cparams(("arbitrary", "arbitrary")),
        name="qk_prep",
    )(proj, cosa, sina, cosc, sinc, ga, gb, gc)


def _attn_a_kernel(sink_ref, q_ref, k_ref, v_ref, kc_ref, vc_ref, o_ref, *, tq):
    L = k_ref.shape[1]
    kv, i = pl.program_id(1), pl.program_id(2)
    win = tq + 2 * A_WINDOW
    start = pl.multiple_of(jnp.clip(i * tq - A_WINDOW, 0, L - win), A_WINDOW)
    k = k_ref[0, pl.ds(start, win), :]
    v = v_ref[0, pl.ds(start, win), :]
    kc, vc = kc_ref[0], vc_ref[0]
    qpos = i * tq + lax.broadcasted_iota(jnp.int32, (tq, win), 0)
    kpos = start + lax.broadcasted_iota(jnp.int32, (tq, win), 1)
    in_window = jnp.abs(kpos - qpos) <= A_WINDOW
    group = A_HEADS // A_KV_HEADS
    for g in range(group):
        sl = slice(g * HEAD_DIM, (g + 1) * HEAD_DIM)
        q = q_ref[0, :, sl]
        s = jnp.where(in_window, _dot_nt(q, k), NEG_INF)
        sc = _dot_nt(q, kc)
        sink = sink_ref[kv * group + g]
        m = jnp.maximum(jnp.maximum(jnp.max(s, axis=-1, keepdims=True), jnp.max(sc, axis=-1, keepdims=True)), sink)
        p = jnp.exp(s - m)
        pc = jnp.exp(sc - m)
        denom = jnp.sum(p, axis=-1, keepdims=True) + jnp.sum(pc, axis=-1, keepdims=True) + jnp.exp(sink - m)
        o = _dot(p.astype(BF16), v) + _dot(pc.astype(BF16), vc)
        o_ref[0, :, sl] = (o / denom).astype(o_ref.dtype)


def attn_a(qk, qkc, sink, tq=512):
    B, L, _ = qk.shape
    C = qkc.shape[1]
    gw = (A_HEADS // A_KV_HEADS) * HEAD_DIM
    return pl.pallas_call(
        functools.partial(_attn_a_kernel, tq=tq),
        out_shape=jax.ShapeDtypeStruct((B, L, A_HEADS * HEAD_DIM), BF16),
        grid=(B, A_KV_HEADS, L // tq),
        in_specs=[
            pl.BlockSpec(memory_space=pltpu.SMEM),
            pl.BlockSpec((1, tq, gw), lambda b, kv, i: (b, i, kv)),
            pl.BlockSpec((1, L, HEAD_DIM), lambda b, kv, i: (b, 0, A_K + kv)),
            pl.BlockSpec((1, L, HEAD_DIM), lambda b, kv, i: (b, 0, A_V + kv)),
            pl.BlockSpec((1, C, HEAD_DIM), lambda b, kv, i: (b, 0, A_K + kv)),
            pl.BlockSpec((1, C, HEAD_DIM), lambda b, kv, i: (b, 0, A_V + kv)),
        ],
        out_specs=pl.BlockSpec((1, tq, gw), lambda b, kv, i: (b, i, kv)),
        compiler_params=_cparams(("arbitrary", "arbitrary", "arbitrary")),
        name="attn_a",
    )(sink, qk, qk, qk, qkc, qkc)


NA_QROWS = 8


def _na_bias_tables(rpb, R):
    W = GRID_W
    H = rpb.shape[0]
    wq = jnp.arange(W)
    cs = jnp.clip(wq - NA_COLS // 2, 0, W - NA_COLS)
    colmask = (wq[None, :] >= cs[:, None]) & (wq[None, :] < cs[:, None] + NA_COLS)
    rel_c = jnp.clip(wq[None, :] - wq[:, None] + (NA_COLS - 1), 0, 2 * NA_COLS - 2)
    per_row = jnp.where(colmask[None, None], rpb.astype(F32)[:, :, rel_c], NEG_INF)
    nq, nk = NA_QROWS, 2 * NA_QROWS
    tables = []
    for r0, ks in ((0, 0), (nq, nq - NA_ROWS // 2), (R - nq, R - nk)):
        qr = r0 + jnp.arange(nq)
        kr = ks + jnp.arange(nk)
        rs = jnp.clip(qr - NA_ROWS // 2, 0, R - NA_ROWS)
        valid = (kr[None, :] >= rs[:, None]) & (kr[None, :] < rs[:, None] + NA_ROWS)
        dr = jnp.clip(kr[None, :] - qr[:, None] + (NA_ROWS - 1), 0, 2 * NA_ROWS - 2)
        t = jnp.where(valid[None, :, :, None, None], per_row[:, dr], NEG_INF)
        tables.append(jnp.transpose(t, (0, 1, 3, 2, 4)).reshape(H, nq * W, nk * W))
    return jnp.stack(tables)


def _attn_b_kernel(q_ref, k_ref, v_ref, kc_ref, vc_ref, bias_ref, o_ref):
    L = k_ref.shape[1]
    R = L // GRID_W
    i = pl.program_id(2)
    nk = 2 * NA_QROWS
    krow = jnp.clip(i * NA_QROWS - NA_ROWS // 2, 0, R - nk)
    start = pl.multiple_of(krow * GRID_W, (NA_ROWS // 2) * GRID_W)
    k = k_ref[0, pl.ds(start, nk * GRID_W), :]
    v = v_ref[0, pl.ds(start, nk * GRID_W), :]
    q = q_ref[0]
    s = _dot_nt(q, k) + bias_ref[0, 0]
    sc = _dot_nt(q, kc_ref[0])
    m = jnp.maximum(jnp.max(s, axis=-1, keepdims=True), jnp.max(sc, axis=-1, keepdims=True))
    p = jnp.exp(s - m)
    pc = jnp.exp(sc - m)
    denom = jnp.sum(p, axis=-1, keepdims=True) + jnp.sum(pc, axis=-1, keepdims=True)
    o = _dot(p.astype(BF16), v) + _dot(pc.astype(BF16), vc_ref[0])
    o_ref[0] = (o / denom).astype(o_ref.dtype)


def attn_b(qk, qkc, bias):
    B, L, _ = qk.shape
    C = qkc.shape[1]
    tq = NA_QROWS * GRID_W
    nblk = L // tq

    def bias_idx(h, b, i):
        return (jnp.where(i == 0, 0, jnp.where(i == nblk - 1, 2, 1)), h, 0, 0)

    return pl.pallas_call(
        _attn_b_kernel,
        out_shape=jax.ShapeDtypeStruct((B, L, B_HEADS * HEAD_DIM), BF16),
        grid=(B_HEADS, B, nblk),
        in_specs=[
            pl.BlockSpec((1, tq, HEAD_DIM), lambda h, b, i: (b, i, B_Q + h)),
            pl.BlockSpec((1, L, HEAD_DIM), lambda h, b, i: (b, 0, B_K + h)),
            pl.BlockSpec((1, L, HEAD_DIM), lambda h, b, i: (b, 0, B_V + h)),
            pl.BlockSpec((1, C, HEAD_DIM), lambda h, b, i: (b, 0, B_K + h)),
            pl.BlockSpec((1, C, HEAD_DIM), lambda h, b, i: (b, 0, B_V + h)),
            pl.BlockSpec((1, 1, tq, 2 * tq), bias_idx),
        ],
        out_specs=pl.BlockSpec((1, tq, HEAD_DIM), lambda h, b, i: (b, i, h)),
        compiler_params=_cparams(("arbitrary", "arbitrary", "arbitrary")),
        name="attn_b",
    )(qk, qk, qk, qkc, qkc, bias)


def _diff_lambda(lambda_params, lam_init):
    lp = lambda_params.astype(F32)
    return jnp.exp(jnp.sum(lp[0] * lp[1])) - jnp.exp(jnp.sum(lp[2] * lp[3])) + lam_init


ATTN_C_SAFE_SCORE_BOUND = 40.0


def _attn_c_kernel(lam_ref, q_ref, k_ref, v_ref, kc_ref, vc_ref, g_ref, o_ref, vt_ref, vct_ref, *scratch,
                   tk, coef, online):
    if online:
        m_ref, l_ref, acc_ref = scratch
        m_ref[...] = jnp.full(m_ref.shape, NEG_INF, F32)
    else:
        l_ref, acc_ref = scratch
    nk = vt_ref.shape[0]

    @pl.when(pl.program_id(2) == 0)
    def _():
        for j in range(nk):
            vt_ref[j] = v_ref[0, j * tk:(j + 1) * tk, :].astype(F32).T.astype(BF16)
        vct_ref[...] = vc_ref[0].astype(F32).T.astype(BF16)

    q = q_ref[0]
    lane = lax.broadcasted_iota(jnp.int32, q.shape, 1)
    zero = jnp.zeros_like(q)
    qs = (jnp.where(lane < C_SUB_DIM, q, zero), jnp.where(lane < C_SUB_DIM, zero, q))

    l_ref[...] = jnp.zeros(l_ref.shape, F32)
    acc_ref[...] = jnp.zeros(acc_ref.shape, F32)

    def accumulate(kblk, vtblk):
        for s_idx in range(2):
            s = _dot_nt(kblk, qs[s_idx])
            if online:
                m_old = m_ref[s_idx]
                m_new = jnp.maximum(m_old, jnp.max(s, axis=0, keepdims=True))
                alpha = jnp.exp2(m_old - m_new)
                p = jnp.exp2(s - m_new)
                l_ref[s_idx] = alpha * l_ref[s_idx] + jnp.sum(p, axis=0, keepdims=True)
                acc_ref[s_idx] = alpha * acc_ref[s_idx] + _dot(vtblk, p.astype(BF16))
                m_ref[s_idx] = m_new
            else:
                p = jnp.exp2(s)
                l_ref[s_idx] += jnp.sum(p, axis=0, keepdims=True)
                acc_ref[s_idx] += _dot(vtblk, p.astype(BF16))

    def body(j, carry):
        accumulate(k_ref[0, pl.ds(pl.multiple_of(j * tk, tk), tk), :], vt_ref[j])
        return carry

    lax.fori_loop(0, nk, body, 0, unroll=4)
    accumulate(kc_ref[0], vct_ref[...])

    o = acc_ref[0] / l_ref[0] - lam_ref[0] * (acc_ref[1] / l_ref[1])
    y = o * lax.rsqrt(jnp.mean(o * o, axis=0, keepdims=True) + EPS) * g_ref[...] * coef
    o_ref[0] = y.T.astype(o_ref.dtype)


def attn_c(qk, qkc, lam, subln, coef, online, tq=1024, tk=512):
    B, L, _ = qk.shape
    C = qkc.shape[1]
    H, dv = C_HEADS, HEAD_DIM
    stats = [pltpu.VMEM((2, 1, tq), F32)] * (2 if online else 1)
    return pl.pallas_call(
        functools.partial(_attn_c_kernel, tk=tk, coef=coef, online=online),
        out_shape=jax.ShapeDtypeStruct((B, L, H * dv), BF16),
        grid=(B, H, L // tq),
        in_specs=[
            pl.BlockSpec(memory_space=pltpu.SMEM),
            pl.BlockSpec((1, tq, HEAD_DIM), lambda b, h, i: (b, i, C_Q + h)),
            pl.BlockSpec((1, L, HEAD_DIM), lambda b, h, i: (b, 0, C_K + h)),
            pl.BlockSpec((1, L, HEAD_DIM), lambda b, h, i: (b, 0, C_V + h)),
            pl.BlockSpec((1, C, HEAD_DIM), lambda b, h, i: (b, 0, C_K + h)),
            pl.BlockSpec((1, C, HEAD_DIM), lambda b, h, i: (b, 0, C_V + h)),
            pl.BlockSpec((dv, 1), lambda b, h, i: (0, 0)),
        ],
        out_specs=pl.BlockSpec((1, tq, dv), lambda b, h, i: (b, i, h)),
        scratch_shapes=[pltpu.VMEM((L // tk, dv, tk), BF16), pltpu.VMEM((dv, C), BF16)]
        + stats + [pltpu.VMEM((2, dv, tq), F32)],
        compiler_params=_cparams(("arbitrary", "arbitrary", "arbitrary")),
        name="attn_c_online" if online else "attn_c",
    )(lam, qk, qk, qk, qkc, qkc, subln.reshape(dv, 1).astype(F32))


def _retention_kernel(ld_ref, q_ref, k_ref, v_ref, g_ref, qc_ref, kc_ref, vc_ref, gc_ref, *rest, with_ctx):
    if with_ctx:
        y_ref, yc_ref, of_ref, ob_ref, ocf_ref = rest
    else:
        y_ref, of_ref, ob_ref = rest
        yc_ref = ocf_ref = None
    c = RET_CHUNK
    L, C = q_ref.shape[1], qc_ref.shape[1]
    h = pl.program_id(1)
    scale = HEAD_DIM ** -0.5
    row = lax.broadcasted_iota(jnp.int32, (c, c), 0).astype(F32)
    col = lax.broadcasted_iota(jnp.int32, (c, c), 1).astype(F32)
    rowp = lax.broadcasted_iota(jnp.int32, (c, HEAD_DIM), 0).astype(F32)

    def decays(lg, reverse):
        if reverse:
            rel, q_exp, k_exp = col - row, c - rowp, rowp
        else:
            rel, q_exp, k_exp = row - col, rowp + 1.0, (c - 1.0) - rowp
        intra = jnp.where(rel >= 0, jnp.exp(lg * jnp.maximum(rel, 0.0)), 0.0)
        return intra, jnp.exp(lg * q_exp), jnp.exp(lg * k_exp), jnp.exp(lg * c)

    def step(S, q, k, v, dec, want_out):
        intra, q_decay, k_decay, chunk_decay = dec
        o = None
        if want_out:
            qs = (q.astype(F32) * scale).astype(BF16)
            s = _dot_nt(qs, k) * intra
            o = _dot(s.astype(BF16), v) + _dot(qs, S.astype(BF16)) * q_decay
        kd_t = (k.astype(F32) * k_decay).T.astype(BF16)
        return S * chunk_decay + _dot(kd_t, v), o

    def gated(o, g):
        g = g.astype(F32)
        y = o * lax.rsqrt(jnp.mean(o * o, axis=-1, keepdims=True) + EPS)
        return y * (g * jax.nn.sigmoid(g))

    def lat(ref, j):
        return ref[0, pl.ds(pl.multiple_of(j * c, c), c), :]

    def ctx(ref, j):
        return ref[0, j * c:(j + 1) * c, :]

    dec_f = decays(ld_ref[0, h], False)
    dec_b = decays(ld_ref[1, h], True)
    S_f = jnp.zeros((HEAD_DIM, HEAD_DIM), F32)
    for j in range(C // c):
        S_f, o = step(S_f, ctx(qc_ref, j), ctx(kc_ref, j), ctx(vc_ref, j), dec_f, with_ctx)
        if with_ctx:
            ocf_ref[j * c:(j + 1) * c, :] = o
    S_b = jnp.zeros((HEAD_DIM, HEAD_DIM), F32)
    for j in reversed(range(C // c)):
        S_b, o = step(S_b, ctx(qc_ref, j), ctx(kc_ref, j), ctx(vc_ref, j), dec_b, with_ctx)
        if with_ctx:
            yc_ref[0, j * c:(j + 1) * c, :] = gated(o + ocf_ref[j * c:(j + 1) * c, :], ctx(gc_ref, j)).astype(yc_ref.dtype)

    n = L // c

    def both(t, carry):
        S_f, S_b = carry
        S_f, o_f = step(S_f, lat(q_ref, t), lat(k_ref, t), lat(v_ref, t), dec_f, True)
        of_ref[pl.ds(pl.multiple_of(t * c, c), c), :] = o_f
        j = n - 1 - t
        S_b, o_b = step(S_b, lat(q_ref, j), lat(k_ref, j), lat(v_ref, j), dec_b, True)
        ob_ref[pl.ds(pl.multiple_of(j * c, c), c), :] = o_b
        return S_f, S_b

    lax.fori_loop(0, n, both, (S_f, S_b), unroll=2)

    def finish(j, carry):
        sl = pl.ds(pl.multiple_of(j * c, c), c)
        y_ref[0, sl, :] = gated(of_ref[sl, :] + ob_ref[sl, :], lat(g_ref, j)).astype(y_ref.dtype)
        return carry

    lax.fori_loop(0, n, finish, 0, unroll=2)


def retention(proj, projc, log_decay, with_ctx):
    B, L, _ = proj.shape
    C = projc.shape[1]

    def lat(blk):
        return pl.BlockSpec((1, L, HEAD_DIM), lambda b, h: (b, 0, blk + h))

    def ctx(blk):
        return pl.BlockSpec((1, C, HEAD_DIM), lambda b, h: (b, 0, blk + h))

    out_shape = [jax.ShapeDtypeStruct((B, L, D_HEADS * HEAD_DIM), BF16)]
    out_specs = [pl.BlockSpec((1, L, HEAD_DIM), lambda b, h: (b, 0, h))]
    scratch = [pltpu.VMEM((L, HEAD_DIM), F32), pltpu.VMEM((L, HEAD_DIM), F32)]
    if with_ctx:
        out_shape.append(jax.ShapeDtypeStruct((B, C, D_HEADS * HEAD_DIM), BF16))
        out_specs.append(pl.BlockSpec((1, C, HEAD_DIM), lambda b, h: (b, 0, h)))
        scratch.append(pltpu.VMEM((C, HEAD_DIM), F32))
    outs = pl.pallas_call(
        functools.partial(_retention_kernel, with_ctx=with_ctx),
        out_shape=out_shape,
        grid=(B, D_HEADS),
        in_specs=[pl.BlockSpec(memory_space=pltpu.SMEM),
                  lat(D_Q), lat(D_K), lat(D_V), lat(D_G), ctx(D_Q), ctx(D_K), ctx(D_V), ctx(D_G)],
        out_specs=out_specs,
        scratch_shapes=scratch,
        compiler_params=_cparams(("arbitrary", "arbitrary")),
        name="retention_ctx" if with_ctx else "retention",
    )(log_decay.astype(F32), proj, proj, proj, proj, projc, projc, projc, projc)
    return (outs[0], outs[1]) if with_ctx else (outs[0], None)


def _ctx_attn_kernel(sink_ref, lam_ref, qk_ref, g_ref, o_ref, *, coef):
    def blk(idx):
        return qk_ref[0, :, idx * HEAD_DIM:(idx + 1) * HEAD_DIM]

    def softmax_out(q, k, v, sink=None, exp=jnp.exp):
        s = _dot_nt(q, k)
        m = jnp.max(s, axis=-1, keepdims=True)
        if sink is not None:
            m = jnp.maximum(m, sink)
        p = exp(s - m)
        denom = jnp.sum(p, axis=-1, keepdims=True)
        if sink is not None:
            denom = denom + jnp.exp(sink - m)
        return _dot(p.astype(BF16), v) / denom

    group = A_HEADS // A_KV_HEADS
    for h in range(A_HEADS):
        o = softmax_out(blk(A_Q + h), blk(A_K + h // group), blk(A_V + h // group), sink_ref[h])
        o_ref[0, :, h * HEAD_DIM:(h + 1) * HEAD_DIM] = o.astype(o_ref.dtype)
    for h in range(B_HEADS):
        o = softmax_out(blk(B_Q + h), blk(B_K + h), blk(B_V + h))
        o_ref[0, :, (A_HEADS + h) * HEAD_DIM:(A_HEADS + h + 1) * HEAD_DIM] = o.astype(o_ref.dtype)
    lane = lax.broadcasted_iota(jnp.int32, (qk_ref.shape[1], HEAD_DIM), 1)
    for h in range(C_HEADS):
        q, k, v = blk(C_Q + h), blk(C_K + h), blk(C_V + h)
        zero = jnp.zeros_like(q)
        o1 = softmax_out(jnp.where(lane < C_SUB_DIM, q, zero), k, v, exp=jnp.exp2)
        o2 = softmax_out(jnp.where(lane < C_SUB_DIM, zero, q), k, v, exp=jnp.exp2)
        o = o1 - lam_ref[0] * o2
        y = o * lax.rsqrt(jnp.mean(o * o, axis=-1, keepdims=True) + EPS) * g_ref[...] * coef
        base = A_HEADS + B_HEADS + h
        o_ref[0, :, base * HEAD_DIM:(base + 1) * HEAD_DIM] = y.astype(o_ref.dtype)


def ctx_attn(qkc, sink, lam, subln, coef):
    B, C, W = qkc.shape
    n_out = (A_HEADS + B_HEADS + C_HEADS) * HEAD_DIM
    return pl.pallas_call(
        functools.partial(_ctx_attn_kernel, coef=coef),
        out_shape=jax.ShapeDtypeStruct((B, C, n_out), BF16),
        grid=(B,),
        in_specs=[
            pl.BlockSpec(memory_space=pltpu.SMEM),
            pl.BlockSpec(memory_space=pltpu.SMEM),
            pl.BlockSpec((1, C, W), lambda b: (b, 0, 0)),
            pl.BlockSpec((1, HEAD_DIM), lambda b: (0, 0)),
        ],
        out_specs=pl.BlockSpec((1, C, n_out), lambda b: (b, 0, 0)),
        compiler_params=_cparams(("arbitrary",)),
        name="ctx_attn",
    )(sink, lam, qkc, subln.reshape(1, HEAD_DIM).astype(F32))


def _out_proj_kernel(*refs, n_parts):
    parts = refs[:n_parts]
    x_ref, gate_ref, w_ref, o_ref = refs[n_parts:]
    mix = jnp.concatenate([p[0] for p in parts], axis=-1)
    o_ref[0] = x_ref[0] + gate_ref[0] * _dot(mix, w_ref[0])


def out_proj(parts, x, gate, w, layer, tm, tn=512):
    B, L, D = x.shape
    part_specs = [pl.BlockSpec((1, tm, p.shape[2]), lambda b, i, j: (b, i, 0)) for p in parts]
    return pl.pallas_call(
        functools.partial(_out_proj_kernel, n_parts=len(parts)),
        out_shape=jax.ShapeDtypeStruct((B, L, D), F32),
        grid=(B, L // tm, D // tn),
        in_specs=part_specs + [
            pl.BlockSpec((1, tm, tn), lambda b, i, j: (b, i, j)),
            pl.BlockSpec((1, 1, tn), lambda b, i, j: (b, 0, j)),
            pl.BlockSpec((1, w.shape[1], tn), lambda b, i, j: (layer, 0, j)),
        ],
        out_specs=pl.BlockSpec((1, tm, tn), lambda b, i, j: (b, i, j)),
        compiler_params=_cparams(("arbitrary", "arbitrary", "arbitrary")),
        name="out_proj",
    )(*parts, x, gate, w)


def _ffn_prep_kernel(*refs, n_lat_tiles):
    if n_lat_tiles is None:
        x_ref, gain_ref, shift_ref, scale_ref, r_ref, h_ref, id_ref, w_ref = refs
        x = x_ref[...]
    else:
        xl_ref, xc_ref, gain_ref, shift_ref, scale_ref, r_ref, h_ref, id_ref, w_ref = refs
        x = jnp.where(pl.program_id(0) < n_lat_tiles, xl_ref[...], xc_ref[...])
    h = _modulated(x, gain_ref[...], shift_ref[0], scale_ref[0])
    h_ref[...] = h

    logits = _dot3(h, r_ref[...])
    lane = lax.broadcasted_iota(jnp.int32, logits.shape, 1)
    big = jnp.int32(logits.shape[1])

    def first_argmax(vals):
        top = jnp.max(vals, axis=-1, keepdims=True)
        return top, jnp.min(jnp.where(vals == top, lane, big), axis=-1, keepdims=True)

    is_group = lane < N_GROUPS
    g_top, g_idx = first_argmax(jnp.where(is_group, logits, NEG_INF))
    g_val = 1.0 / jnp.sum(jnp.where(is_group, jnp.exp(logits - g_top), 0.0), axis=-1, keepdims=True)
    lo = N_GROUPS + EXPERTS_PER_GROUP * g_idx
    e_logits = jnp.where((lane >= lo) & (lane < lo + EXPERTS_PER_GROUP), logits, NEG_INF)
    v1, i1 = first_argmax(e_logits)
    v2, i2 = first_argmax(jnp.where(lane == i1, NEG_INF, e_logits))
    e21 = jnp.exp(v2 - v1)
    w1 = g_val / (1.0 + e21)
    w2 = g_val * e21 / (1.0 + e21)
    id_ref[...] = jnp.where(lane == 0, i1 - N_GROUPS, jnp.where(lane == 1, i2 - N_GROUPS, 0))
    w_ref[...] = jnp.where(lane == 0, w1, jnp.where(lane == 1, w2, 0.0))


def ffn_prep(x, xc, gain, shift, scale, r_cat, t=256):
    B, L, D = x.shape
    n_lat = B * L // t
    lat_per_seg = L // t
    if xc is None:
        T = B * L
        xs = [x.reshape(B * L, D)]
        x_specs = [pl.BlockSpec((t, D), lambda i: (i, 0))]
        seg = lambda i: (i // lat_per_seg, 0, 0)
        n_lat_tiles = None
    else:
        C = xc.shape[1]
        T = B * (L + C)
        xs = [x.reshape(B * L, D), xc.reshape(B * C, D)]
        x_specs = [pl.BlockSpec((t, D), lambda i: (jnp.minimum(i, n_lat - 1), 0)),
                   pl.BlockSpec((t, D), lambda i: (jnp.maximum(i - n_lat, 0), 0))]
        seg = lambda i: (jnp.minimum(i // lat_per_seg, B), 0, 0)
        n_lat_tiles = n_lat
    return pl.pallas_call(
        functools.partial(_ffn_prep_kernel, n_lat_tiles=n_lat_tiles),
        out_shape=[jax.ShapeDtypeStruct((T, D), F32),
                   jax.ShapeDtypeStruct((T, 128), jnp.int32),
                   jax.ShapeDtypeStruct((T, 128), F32)],
        grid=(T // t,),
        in_specs=x_specs + [
            pl.BlockSpec((1, D), lambda i: (0, 0)),
            pl.BlockSpec((1, 1, D), seg),
            pl.BlockSpec((1, 1, D), seg),
            pl.BlockSpec((D, 128), lambda i: (0, 0)),
        ],
        out_specs=[pl.BlockSpec((t, D), lambda i: (i, 0)),
                   pl.BlockSpec((t, 128), lambda i: (i, 0)),
                   pl.BlockSpec((t, 128), lambda i: (i, 0))],
        compiler_params=_cparams(("arbitrary",)),
        name="ffn_prep",
    )(*xs, gain, shift, scale, r_cat)


def _route(ids, tm):
    A = ids.shape[0] * TOP_K
    flat_e = ids.reshape(A)
    onehot = (flat_e[:, None] == jnp.arange(N_EXPERTS, dtype=jnp.int32)[None, :]).astype(jnp.int32)
    csum = jnp.cumsum(onehot, axis=0)
    rank = jnp.sum(csum * onehot, axis=1) - 1
    counts = csum[-1]
    pcounts = (counts + tm - 1) // tm * tm
    pends = jnp.cumsum(pcounts)
    pstarts = pends - pcounts
    dest = jnp.sum(onehot * pstarts[None, :], axis=1) + rank
    NB = (A + N_EXPERTS * (tm - 1)) // tm
    row_tok = jnp.zeros((NB * tm,), jnp.int32).at[dest].set(jnp.arange(A, dtype=jnp.int32) // TOP_K)
    block_start = jnp.arange(NB, dtype=jnp.int32) * tm
    block_e = jnp.minimum(jnp.sum((pends[None, :] <= block_start[:, None]).astype(jnp.int32), axis=1), N_EXPERTS - 1)
    return dest.astype(jnp.int32), row_tok, block_e.astype(jnp.int32), (pends[-1:] // tm).astype(jnp.int32)


def _row_copy(src, src_row, dst, dst_row, sem):
    return pltpu.make_async_copy(src.at[pl.ds(src_row, 1), :], dst.at[pl.ds(dst_row, 1), :], sem)


def _expert_kernel(be_ref, nused_ref, tok_ref, h_ref, wg_ref, wu_ref, wd_ref, o_ref, xbuf, sem):
    i = pl.program_id(0)
    tm = o_ref.shape[0]
    n_used = nused_ref[0]

    def row(r, blk, slot):
        return _row_copy(h_ref, tok_ref[blk * tm + r], xbuf.at[slot], r, sem.at[slot])

    def gather(blk, slot):
        def issue(r, carry):
            row(r, blk, slot).start()
            return carry
        lax.fori_loop(0, tm, issue, 0, unroll=8)

    def drain(slot):
        def wait(r, carry):
            row(r, 0, slot).wait()
            return carry
        lax.fori_loop(0, tm, wait, 0, unroll=8)

    @pl.when(i == 0)
    def _():
        gather(0, 0)

    @pl.when(i < n_used)
    def _():
        slot = i % 2
        drain(slot)
        x = xbuf[slot].astype(BF16)
        nxt = jnp.minimum(i + 1, n_used - 1)
        for r in range(tm):
            row(r, nxt, 1 - slot).start()
        a = _dot(x, wg_ref[0, 0])
        u = _dot(x, wu_ref[0, 0])
        hmid = (a * jax.nn.sigmoid(a) * u).astype(BF16)
        o_ref[...] = _dot(hmid, wd_ref[0, 0])

    @pl.when(i == n_used - 1)
    def _():
        drain(1 - i % 2)

    @pl.when(i >= n_used)
    def _():
        o_ref[...] = jnp.zeros(o_ref.shape, o_ref.dtype)


def experts(block_e, n_used, row_tok, h, w_gate, w_up, w_down, layer, tm):
    P = row_tok.shape[0]
    _, _, D, De = w_gate.shape
    w_idx = lambda i, be, nu, tok: (layer, be[jnp.minimum(i, nu[0] - 1)], 0, 0)
    return pl.pallas_call(
        _expert_kernel,
        out_shape=jax.ShapeDtypeStruct((P, D), F32),
        grid_spec=pltpu.PrefetchScalarGridSpec(
            num_scalar_prefetch=3,
            grid=(P // tm,),
            in_specs=[
                pl.BlockSpec(memory_space=pl.ANY),
                pl.BlockSpec((1, 1, D, De), w_idx),
                pl.BlockSpec((1, 1, D, De), w_idx),
                pl.BlockSpec((1, 1, De, D), w_idx),
            ],
            out_specs=pl.BlockSpec((tm, D), lambda i, be, nu, tok: (i, 0)),
            scratch_shapes=[pltpu.VMEM((2, tm, D), F32), pltpu.SemaphoreType.DMA((2,))],
        ),
        compiler_params=_cparams(("arbitrary",)),
        name="moe_experts",
    )(block_e, n_used, row_tok, h, w_gate, w_up, w_down)


def _combine_kernel(dest_ref, x_ref, gate_ref, w_ref, y_ref, o_ref, buf, sem, *, tok_offset):
    t, D = x_ref.shape
    i = pl.program_id(0)
    slot = i % 2

    def gather(step, slot):
        base = (tok_offset + step * t) * TOP_K

        def issue(r, carry):
            for k in range(TOP_K):
                _row_copy(y_ref, dest_ref[base + r * TOP_K + k], buf.at[slot, k], r, sem.at[slot]).start()
            return carry
        lax.fori_loop(0, t, issue, 0, unroll=4)

    def drain(r, carry):
        _row_copy(y_ref, 0, buf.at[slot, 0], 0, sem.at[slot]).wait()
        return carry

    @pl.when(i == 0)
    def _():
        gather(0, 0)

    @pl.when(i + 1 < pl.num_programs(0))
    def _():
        gather(i + 1, 1 - slot)

    lax.fori_loop(0, t * TOP_K, drain, 0, unroll=8)
    w = w_ref[...]
    y = w[:, 0:1] * buf[slot, 0] + w[:, 1:2] * buf[slot, 1]
    o_ref[...] = x_ref[...] + gate_ref[0] * y


def combine(dest, x, gate, rw, ys, tok_offset, t=256):
    B, L, D = x.shape
    per_seg = L // t
    woff = tok_offset // t
    out = pl.pallas_call(
        functools.partial(_combine_kernel, tok_offset=tok_offset),
        out_shape=jax.ShapeDtypeStruct((B * L, D), F32),
        grid_spec=pltpu.PrefetchScalarGridSpec(
            num_scalar_prefetch=1,
            grid=(B * L // t,),
            in_specs=[
                pl.BlockSpec((t, D), lambda i, d: (i, 0)),
                pl.BlockSpec((1, 1, D), lambda i, d: (i // per_seg, 0, 0)),
                pl.BlockSpec((t, 128), lambda i, d: (woff + i, 0)),
                pl.BlockSpec(memory_space=pl.ANY),
            ],
            out_specs=pl.BlockSpec((t, D), lambda i, d: (i, 0)),
            scratch_shapes=[pltpu.VMEM((2, TOP_K, t, D), F32), pltpu.SemaphoreType.DMA((2,))],
        ),
        compiler_params=_cparams(("arbitrary",)),
        name="moe_combine",
    )(dest, x.reshape(B * L, D), gate, rw, ys)
    return out.reshape(B, L, D)


MOE_TM = 256


def kernel(x, c, ctx, c_ctx, w_mod, b_mod, norm_mix, norm_ffn, w_in, w_out, qk_norm_a, sink_a, qk_norm_b, rpb_b,
           qk_norm_c, lambda_c, subln_c, ret_log_decay, router_group, router_expert, w_gate, w_up, w_down):
    B, L, D = x.shape
    C = ctx.shape[1]
    depth = w_mod.shape[0]

    cvec = jnp.zeros((8, D), F32).at[:B].set(c).at[B].set(c_ctx)
    mod = mod_vectors(cvec, w_mod, b_mod).reshape(depth, 8, 6, D)

    cosa, sina = _rope_tables(L, HEAD_DIM)
    cosc, sinc = _rope_tables(L, C_SUB_DIM)
    one_tab = jnp.ones((C, HEAD_DIM), F32)
    zero_tab = jnp.zeros((C, HEAD_DIM), F32)

    w_in_b, w_out_b = w_in.astype(BF16), w_out.astype(BF16)
    w_gate_b, w_up_b, w_down_b = w_gate.astype(BF16), w_up.astype(BF16), w_down.astype(BF16)

    xc = ctx
    for l in range(depth):
        with_ctx = l < depth - 1
        lat = lambda j: mod[l, :B, j][:, None, :]
        cx = lambda j: jnp.broadcast_to(mod[l, B, j][None, None, :], (B, 1, D))
        gain_mix = norm_mix[l][None, :]

        proj = in_proj(x, gain_mix, lat(0), lat(1), w_in_b, l, tm=512)
        projc = in_proj(xc, gain_mix, cx(0), cx(1), w_in_b, l, tm=C)
        gain_c = jnp.tile(qk_norm_c[l], (1, HEAD_DIM // C_SUB_DIM))
        qk = qk_prep(proj, cosa, sina, cosc, sinc, qk_norm_a[l], qk_norm_b[l], gain_c)
        qkc = qk_prep(projc, one_tab, zero_tab, one_tab, zero_tab, qk_norm_a[l], qk_norm_b[l], gain_c)

        lam_init = 0.8 - 0.6 * math.exp(-0.3 * l)
        lam = _diff_lambda(lambda_c[l], lam_init).reshape(1)
        out_a = attn_a(qk, qkc, sink_a[l])
        out_b = attn_b(qk, qkc, _na_bias_tables(rpb_b[l], L // GRID_W))
        score_bound = (C_SUB_DIM ** 0.5) * jnp.max(jnp.abs(qk_norm_c[l, 0])) * jnp.max(jnp.abs(qk_norm_c[l, 1]))
        attn_c_l = functools.partial(attn_c, qk, qkc, lam, subln_c[l], 1.0 - lam_init)
        out_c = lax.cond(score_bound <= ATTN_C_SAFE_SCORE_BOUND,
                         lambda: attn_c_l(online=False), lambda: attn_c_l(online=True))
        out_d, out_dc = retention(proj, projc, ret_log_decay[l], with_ctx)

        x = out_proj([out_a, out_b, out_c, out_d], x, lat(2), w_out_b, l, tm=1024)
        if with_ctx:
            out_abc_c = ctx_attn(qkc, sink_a[l], lam, subln_c[l], 1.0 - lam_init)
            xc = out_proj([out_abc_c, out_dc], xc, cx(2), w_out_b, l, tm=C)

        r_cat = jnp.zeros((D, 128), F32).at[:, :N_GROUPS].set(router_group[l])
        r_cat = r_cat.at[:, N_GROUPS:N_GROUPS + N_EXPERTS].set(router_expert[l])
        if with_ctx:
            shift = jnp.concatenate([lat(3), cx(3)[:1]], axis=0)
            scale = jnp.concatenate([lat(4), cx(4)[:1]], axis=0)
        else:
            shift, scale = lat(3), lat(4)
        h2, rid, rw = ffn_prep(x, xc if with_ctx else None, norm_ffn[l][None, :], shift, scale, r_cat)
        dest, row_tok, block_e, n_used = _route(rid[:, :TOP_K], MOE_TM)
        ys = experts(block_e, n_used, row_tok, h2, w_gate_b, w_up_b, w_down_b, l, MOE_TM)
        x = combine(dest, x, lat(5), rw, ys, 0)
        if with_ctx:
            xc = combine(dest, xc, cx(5), rw, ys, B * L)
    return x
```

```python
import functools
import math

import jax
import jax.numpy as jnp
import numpy as np
from jax import lax
from jax.experimental import pallas as pl
from jax.experimental.pallas import tpu as pltpu

F32 = jnp.float32
BF16 = jnp.bfloat16

HEAD_DIM = 128
GRID_W = 64
A_HEADS, A_KV_HEADS, A_WINDOW = 8, 2, 128
B_HEADS, NA_ROWS, NA_COLS = 8, 8, 16
C_HEADS, C_SUB_DIM = 8, 64
D_HEADS = 8
RET_CHUNK = 256
ROPE_BASE = 10000.0
N_GROUPS, EXPERTS_PER_GROUP, TOP_K = 4, 8, 2
N_EXPERTS = N_GROUPS * EXPERTS_PER_GROUP
EPS = 1e-6
NEG_INF = -1e30

A_Q, A_K, A_V = 0, 8, 10
B_Q, B_K, B_V = 12, 20, 28
C_Q, C_K, C_V = 36, 44, 52
D_Q, D_K, D_V, D_G = 60, 68, 76, 84
ABC_BLOCKS = 60
IN_COLS = 92 * 128

VMEM_LIMIT = 56 * 1024 * 1024


def _cparams(sem, vmem=VMEM_LIMIT):
    return pltpu.CompilerParams(dimension_semantics=sem, vmem_limit_bytes=vmem)


def _dot(a, b):
    return jnp.dot(a, b, preferred_element_type=F32)


def _dot_nt(a, b):
    return lax.dot_general(a, b, (((1,), (1,)), ((), ())), preferred_element_type=F32)


def _split_bf16(x):
    hi = x.astype(BF16)
    lo = (x - hi.astype(F32)).astype(BF16)
    return hi, lo


def _pack_bf16_pairs(x):
    half = x.shape[1] // 2
    bits = lax.bitcast_convert_type(x.astype(BF16).astype(F32), jnp.uint32)
    return (bits[:, :half] >> 16) | (bits[:, half:] & jnp.uint32(0xFFFF0000))


def _unpack_bf16_pairs(p):
    lo = lax.bitcast_convert_type(p << 16, F32)
    hi = lax.bitcast_convert_type(p & jnp.uint32(0xFFFF0000), F32)
    return jnp.concatenate([lo, hi], axis=-1)


def _dot3(x, w):
    xh, xl = _split_bf16(x)
    wh, wl = _split_bf16(w)
    return _dot(xh, wh) + _dot(xl, wh) + _dot(xh, wl)


def _mod_kernel(c_ref, w_ref, b_ref, o_ref):
    c = c_ref[...]
    x = c * jax.nn.sigmoid(c)
    o_ref[0] = _dot3(x, w_ref[0]) + b_ref[0]


def mod_vectors(cvec, w_mod, b_mod, tn=512):
    depth, D, N = w_mod.shape
    return pl.pallas_call(
        _mod_kernel,
        out_shape=jax.ShapeDtypeStruct((depth, 8, N), F32),
        grid=(depth, N // tn),
        in_specs=[
            pl.BlockSpec((8, D), lambda l, j: (0, 0)),
            pl.BlockSpec((1, D, tn), lambda l, j: (l, 0, j)),
            pl.BlockSpec((1, 1, tn), lambda l, j: (l, 0, j)),
        ],
        out_specs=pl.BlockSpec((1, 8, tn), lambda l, j: (l, 0, j)),
        compiler_params=_cparams(("arbitrary", "arbitrary")),
        name="mod_vectors",
    )(cvec, w_mod, b_mod.reshape(depth, 1, N))


def _modulated(x, gain, shift, scale):
    ms = jnp.mean(x * x, axis=-1, keepdims=True)
    y = x * lax.rsqrt(ms + EPS) * gain
    return y * (1.0 + scale) + shift


def _in_proj_kernel(x_ref, gain_ref, shift_ref, scale_ref, w_ref, o_ref, h_ref):
    @pl.when(pl.program_id(2) == 0)
    def _():
        h_ref[...] = _modulated(x_ref[0], gain_ref[...], shift_ref[0], scale_ref[0]).astype(BF16)

    o_ref[0] = _dot(h_ref[...], w_ref[0]).astype(o_ref.dtype)


def in_proj(x, gain, shift, scale, w, layer, tm, tn=512):
    B, L, D = x.shape
    N = w.shape[2]
    return pl.pallas_call(
        _in_proj_kernel,
        out_shape=jax.ShapeDtypeStruct((B, L, N), BF16),
        grid=(B, L // tm, N // tn),
        in_specs=[
            pl.BlockSpec((1, tm, D), lambda b, i, j: (b, i, 0)),
            pl.BlockSpec((1, D), lambda b, i, j: (0, 0)),
            pl.BlockSpec((1, 1, D), lambda b, i, j: (b, 0, 0)),
            pl.BlockSpec((1, 1, D), lambda b, i, j: (b, 0, 0)),
            pl.BlockSpec((1, D, tn), lambda b, i, j: (layer, 0, j)),
        ],
        out_specs=pl.BlockSpec((1, tm, tn), lambda b, i, j: (b, i, j)),
        scratch_shapes=[pltpu.VMEM((tm, D), BF16)],
        compiler_params=_cparams(("arbitrary", "arbitrary", "arbitrary")),
        name="in_proj",
    )(x, gain, shift, scale, w)


def _rope_tables(L, dim):
    t = jnp.arange(L)
    row = (t // GRID_W).astype(F32)
    col = (t % GRID_W).astype(F32)
    quarter = dim // 4
    inv = ROPE_BASE ** (-jnp.arange(quarter, dtype=F32) / quarter)
    ar = row[:, None] * inv[None, :]
    ac = col[:, None] * inv[None, :]
    ang = jnp.concatenate([ar, ar, ac, ac], axis=-1)
    sign = jnp.where((jnp.arange(dim) % (dim // 2)) < quarter, -1.0, 1.0).astype(F32)
    reps = HEAD_DIM // dim
    return jnp.tile(jnp.cos(ang), (1, reps)), jnp.tile(jnp.sin(ang) * sign[None, :], (1, reps))


def _qk_prep_kernel(p_ref, cosa_ref, sina_ref, cosc_ref, sinc_ref, ga_ref, gb_ref, gc_ref, o_ref):
    t = p_ref.shape[1]
    lane = lax.broadcasted_iota(jnp.int32, (t, HEAD_DIM), 1)

    ri = lax.broadcasted_iota(jnp.int32, (HEAD_DIM, HEAD_DIM), 0)
    ci = lax.broadcasted_iota(jnp.int32, (HEAD_DIM, HEAD_DIM), 1)
    ones_full = jnp.ones((HEAD_DIM, HEAD_DIM), BF16)
    ones_halves = jnp.where((ri < C_SUB_DIM) == (ci < C_SUB_DIM), 1.0, 0.0).astype(BF16)

    def head_sums(x, ones):
        hi, lo = _split_bf16(x * x)
        return _dot(hi, ones) + _dot(lo, ones)

    def norm_full(x):
        return x * lax.rsqrt(head_sums(x, ones_full) * (1.0 / HEAD_DIM) + EPS)

    def norm_halves(x):
        return x * lax.rsqrt(head_sums(x, ones_halves) * (1.0 / C_SUB_DIM) + EPS)

    def rope(x, cos, sin, dim):
        quarter = dim // 4
        first = (lane % (dim // 2)) < quarter
        rot = jnp.where(first, pltpu.roll(x, HEAD_DIM - quarter, 1), pltpu.roll(x, quarter, 1))
        return x * cos + rot * sin

    cosa, sina = cosa_ref[...], sina_ref[...]
    cosc, sinc = cosc_ref[...], sinc_ref[...]
    scale_ab = HEAD_DIM ** -0.5
    scale_c = C_SUB_DIM ** -0.5 * math.log2(math.e)
    for blk in range(ABC_BLOCKS):
        sl = slice(blk * HEAD_DIM, (blk + 1) * HEAD_DIM)
        if A_V <= blk < B_Q or B_V <= blk < C_Q or C_V <= blk:
            o_ref[0, :, sl] = p_ref[0, :, sl]
            continue
        x = p_ref[0, :, sl].astype(F32)
        if blk < A_K:
            y = rope(norm_full(x) * ga_ref[0:1, :], cosa, sina, HEAD_DIM) * scale_ab
        elif blk < A_V:
            y = rope(norm_full(x) * ga_ref[1:2, :], cosa, sina, HEAD_DIM)
        elif blk < B_K:
            y = norm_full(x) * gb_ref[0:1, :] * scale_ab
        elif blk < B_V:
            y = norm_full(x) * gb_ref[1:2, :]
        elif blk < C_K:
            y = rope(norm_halves(x) * gc_ref[0:1, :], cosc, sinc, C_SUB_DIM) * scale_c
        else:
            y = rope(norm_halves(x) * gc_ref[1:2, :], cosc, sinc, C_SUB_DIM)
        o_ref[0, :, sl] = y.astype(o_ref.dtype)


def qk_prep(proj, cosa, sina, cosc, sinc, ga, gb, gc, t=256):
    B, L, _ = proj.shape
    W = ABC_BLOCKS * HEAD_DIM
    tab = pl.BlockSpec((t, HEAD_DIM), lambda b, i: (i, 0))
    par = pl.BlockSpec((2, HEAD_DIM), lambda b, i: (0, 0))
    return pl.pallas_call(
        _qk_prep_kernel,
        out_shape=jax.ShapeDtypeStruct((B, L, W), BF16),
        grid=(B, L // t),
        in_specs=[pl.BlockSpec((1, t, W), lambda b, i: (b, i, 0)), tab, tab, tab, tab, par, par, par],
        out_specs=pl.BlockSpec((1, t, W), lambda b, i: (b, i, 0)),
        compiler_params=_cparams(("arbitrary", "arbitrary")),
        name="qk_prep",
    )(proj, cosa, sina, cosc, sinc, ga, gb, gc)


def _attn_a_kernel(sink_ref, q_ref, k_ref, v_ref, kc_ref, vc_ref, o_ref, *, tq):
    L = k_ref.shape[1]
    kv, i = pl.program_id(1), pl.program_id(2)
    win = tq + 2 * A_WINDOW
    start = pl.multiple_of(jnp.clip(i * tq - A_WINDOW, 0, L - win), A_WINDOW)
    k = k_ref[0, pl.ds(start, win), :]
    v = v_ref[0, pl.ds(start, win), :]
    kc, vc = kc_ref[0], vc_ref[0]
    qpos = i * tq + lax.broadcasted_iota(jnp.int32, (tq, win), 0)
    kpos = start + lax.broadcasted_iota(jnp.int32, (tq, win), 1)
    in_window = jnp.abs(kpos - qpos) <= A_WINDOW
    group = A_HEADS // A_KV_HEADS
    for g in range(group):
        sl = slice(g * HEAD_DIM, (g + 1) * HEAD_DIM)
        q = q_ref[0, :, sl]
        s = jnp.where(in_window, _dot_nt(q, k), NEG_INF)
        sc = _dot_nt(q, kc)
        sink = sink_ref[kv * group + g]
        m = jnp.maximum(jnp.maximum(jnp.max(s, axis=-1, keepdims=True), jnp.max(sc, axis=-1, keepdims=True)), sink)
        p = jnp.exp(s - m)
        pc = jnp.exp(sc - m)
        denom = jnp.sum(p, axis=-1, keepdims=True) + jnp.sum(pc, axis=-1, keepdims=True) + jnp.exp(sink - m)
        o = _dot(p.astype(BF16), v) + _dot(pc.astype(BF16), vc)
        o_ref[0, :, sl] = (o / denom).astype(o_ref.dtype)


def attn_a(qk, qkc, sink, tq=512):
    B, L, _ = qk.shape
    C = qkc.shape[1]
    gw = (A_HEADS // A_KV_HEADS) * HEAD_DIM
    return pl.pallas_call(
        functools.partial(_attn_a_kernel, tq=tq),
        out_shape=jax.ShapeDtypeStruct((B, L, A_HEADS * HEAD_DIM), BF16),
        grid=(B, A_KV_HEADS, L // tq),
        in_specs=[
            pl.BlockSpec(memory_space=pltpu.SMEM),
            pl.BlockSpec((1, tq, gw), lambda b, kv, i: (b, i, kv)),
            pl.BlockSpec((1, L, HEAD_DIM), lambda b, kv, i: (b, 0, A_K + kv)),
            pl.BlockSpec((1, L, HEAD_DIM), lambda b, kv, i: (b, 0, A_V + kv)),
            pl.BlockSpec((1, C, HEAD_DIM), lambda b, kv, i: (b, 0, A_K + kv)),
            pl.BlockSpec((1, C, HEAD_DIM), lambda b, kv, i: (b, 0, A_V + kv)),
        ],
        out_specs=pl.BlockSpec((1, tq, gw), lambda b, kv, i: (b, i, kv)),
        compiler_params=_cparams(("arbitrary", "arbitrary", "arbitrary")),
        name="attn_a",
    )(sink, qk, qk, qk, qkc, qkc)


NA_QROWS = 8


def _na_bias_tables(rpb, R):
    W = GRID_W
    H = rpb.shape[0]
    wq = jnp.arange(W)
    cs = jnp.clip(wq - NA_COLS // 2, 0, W - NA_COLS)
    colmask = (wq[None, :] >= cs[:, None]) & (wq[None, :] < cs[:, None] + NA_COLS)
    rel_c = jnp.clip(wq[None, :] - wq[:, None] + (NA_COLS - 1), 0, 2 * NA_COLS - 2)
    per_row = jnp.where(colmask[None, None], rpb.astype(F32)[:, :, rel_c], NEG_INF)
    nq, nk = NA_QROWS, 2 * NA_QROWS
    tables = []
    for r0, ks in ((0, 0), (nq, nq - NA_ROWS // 2), (R - nq, R - nk)):
        qr = r0 + jnp.arange(nq)
        kr = ks + jnp.arange(nk)
        rs = jnp.clip(qr - NA_ROWS // 2, 0, R - NA_ROWS)
        valid = (kr[None, :] >= rs[:, None]) & (kr[None, :] < rs[:, None] + NA_ROWS)
        dr = jnp.clip(kr[None, :] - qr[:, None] + (NA_ROWS - 1), 0, 2 * NA_ROWS - 2)
        t = jnp.where(valid[None, :, :, None, None], per_row[:, dr], NEG_INF)
        tables.append(jnp.transpose(t, (0, 1, 3, 2, 4)).reshape(H, nq * W, nk * W))
    return jnp.stack(tables)


def _attn_b_kernel(q_ref, k_ref, v_ref, kc_ref, vc_ref, bias_ref, o_ref):
    L = k_ref.shape[1]
    R = L // GRID_W
    i = pl.program_id(2)
    nk = 2 * NA_QROWS
    krow = jnp.clip(i * NA_QROWS - NA_ROWS // 2, 0, R - nk)
    start = pl.multiple_of(krow * GRID_W, (NA_ROWS // 2) * GRID_W)
    k = k_ref[0, pl.ds(start, nk * GRID_W), :]
    v = v_ref[0, pl.ds(start, nk * GRID_W), :]
    q = q_ref[0]
    s = _dot_nt(q, k) + bias_ref[0, 0]
    sc = _dot_nt(q, kc_ref[0])
    m = jnp.maximum(jnp.max(s, axis=-1, keepdims=True), jnp.max(sc, axis=-1, keepdims=True))
    p = jnp.exp(s - m)
    pc = jnp.exp(sc - m)
    denom = jnp.sum(p, axis=-1, keepdims=True) + jnp.sum(pc, axis=-1, keepdims=True)
    o = _dot(p.astype(BF16), v) + _dot(pc.astype(BF16), vc_ref[0])
    o_ref[0] = (o / denom).astype(o_ref.dtype)


def attn_b(qk, qkc, bias):
    B, L, _ = qk.shape
    C = qkc.shape[1]
    tq = NA_QROWS * GRID_W
    nblk = L // tq

    def bias_idx(h, b, i):
        return (jnp.where(i == 0, 0, jnp.where(i == nblk - 1, 2, 1)), h, 0, 0)

    return pl.pallas_call(
        _attn_b_kernel,
        out_shape=jax.ShapeDtypeStruct((B, L, B_HEADS * HEAD_DIM), BF16),
        grid=(B_HEADS, B, nblk),
        in_specs=[
            pl.BlockSpec((1, tq, HEAD_DIM), lambda h, b, i: (b, i, B_Q + h)),
            pl.BlockSpec((1, L, HEAD_DIM), lambda h, b, i: (b, 0, B_K + h)),
            pl.BlockSpec((1, L, HEAD_DIM), lambda h, b, i: (b, 0, B_V + h)),
            pl.BlockSpec((1, C, HEAD_DIM), lambda h, b, i: (b, 0, B_K + h)),
            pl.BlockSpec((1, C, HEAD_DIM), lambda h, b, i: (b, 0, B_V + h)),
            pl.BlockSpec((1, 1, tq, 2 * tq), bias_idx),
        ],
        out_specs=pl.BlockSpec((1, tq, HEAD_DIM), lambda h, b, i: (b, i, h)),
        compiler_params=_cparams(("arbitrary", "arbitrary", "arbitrary")),
        name="attn_b",
    )(qk, qk, qk, qkc, qkc, bias)


def _diff_lambda(lambda_params, lam_init):
    lp = lambda_params.astype(F32)
    return jnp.exp(jnp.sum(lp[0] * lp[1])) - jnp.exp(jnp.sum(lp[2] * lp[3])) + lam_init


ATTN_C_SAFE_SCORE_BOUND = 40.0


def _attn_c_kernel(lam_ref, q_ref, k_ref, v_ref, kc_ref, vc_ref, g_ref, o_ref, vt_ref, vct_ref, *scratch,
                   tk, coef, online):
    if online:
        m_ref, l_ref, acc_ref = scratch
        m_ref[...] = jnp.full(m_ref.shape, NEG_INF, F32)
    else:
        l_ref, acc_ref = scratch
    nk = vt_ref.shape[0]

    @pl.when(pl.program_id(2) == 0)
    def _():
        for j in range(nk):
            vt_ref[j] = v_ref[0, j * tk:(j + 1) * tk, :].astype(F32).T.astype(BF16)
        vct_ref[...] = vc_ref[0].astype(F32).T.astype(BF16)

    q = q_ref[0]
    lane = lax.broadcasted_iota(jnp.int32, q.shape, 1)
    zero = jnp.zeros_like(q)
    qs = (jnp.where(lane < C_SUB_DIM, q, zero), jnp.where(lane < C_SUB_DIM, zero, q))

    l_ref[...] = jnp.zeros(l_ref.shape, F32)
    acc_ref[...] = jnp.zeros(acc_ref.shape, F32)

    def accumulate(kblk, vtblk):
        for s_idx in range(2):
            s = _dot_nt(kblk, qs[s_idx])
            if online:
                m_old = m_ref[s_idx]
                m_new = jnp.maximum(m_old, jnp.max(s, axis=0, keepdims=True))
                alpha = jnp.exp2(m_old - m_new)
                p = jnp.exp2(s - m_new)
                l_ref[s_idx] = alpha * l_ref[s_idx] + jnp.sum(p, axis=0, keepdims=True)
                acc_ref[s_idx] = alpha * acc_ref[s_idx] + _dot(vtblk, p.astype(BF16))
                m_ref[s_idx] = m_new
            else:
                p = jnp.exp2(s)
                l_ref[s_idx] += jnp.sum(p, axis=0, keepdims=True)
                acc_ref[s_idx] += _dot(vtblk, p.astype(BF16))

    def body(j, carry):
        accumulate(k_ref[0, pl.ds(pl.multiple_of(j * tk, tk), tk), :], vt_ref[j])
        return carry

    lax.fori_loop(0, nk, body, 0, unroll=4)
    accumulate(kc_ref[0], vct_ref[...])

    o = acc_ref[0] / l_ref[0] - lam_ref[0] * (acc_ref[1] / l_ref[1])
    y = o * lax.rsqrt(jnp.mean(o * o, axis=0, keepdims=True) + EPS) * g_ref[...] * coef
    o_ref[0] = y.T.astype(o_ref.dtype)


def attn_c(qk, qkc, lam, subln, coef, online, tq=1024, tk=512):
    B, L, _ = qk.shape
    C = qkc.shape[1]
    H, dv = C_HEADS, HEAD_DIM
    stats = [pltpu.VMEM((2, 1, tq), F32)] * (2 if online else 1)
    return pl.pallas_call(
        functools.partial(_attn_c_kernel, tk=tk, coef=coef, online=online),
        out_shape=jax.ShapeDtypeStruct((B, L, H * dv), BF16),
        grid=(B, H, L // tq),
        in_specs=[
            pl.BlockSpec(memory_space=pltpu.SMEM),
            pl.BlockSpec((1, tq, HEAD_DIM), lambda b, h, i: (b, i, C_Q + h)),
            pl.BlockSpec((1, L, HEAD_DIM), lambda b, h, i: (b, 0, C_K + h)),
            pl.BlockSpec((1, L, HEAD_DIM), lambda b, h, i: (b, 0, C_V + h)),
            pl.BlockSpec((1, C, HEAD_DIM), lambda b, h, i: (b, 0, C_K + h)),
            pl.BlockSpec((1, C, HEAD_DIM), lambda b, h, i: (b, 0, C_V + h)),
            pl.BlockSpec((dv, 1), lambda b, h, i: (0, 0)),
        ],
        out_specs=pl.BlockSpec((1, tq, dv), lambda b, h, i: (b, i, h)),
        scratch_shapes=[pltpu.VMEM((L // tk, dv, tk), BF16), pltpu.VMEM((dv, C), BF16)]
        + stats + [pltpu.VMEM((2, dv, tq), F32)],
        compiler_params=_cparams(("arbitrary", "arbitrary", "arbitrary")),
        name="attn_c_online" if online else "attn_c",
    )(lam, qk, qk, qk, qkc, qkc, subln.reshape(dv, 1).astype(F32))


def _retention_kernel(ld_ref, q_ref, k_ref, v_ref, g_ref, qc_ref, kc_ref, vc_ref, gc_ref, *rest, with_ctx):
    if with_ctx:
        y_ref, yc_ref, of_ref, ob_ref, ocf_ref = rest
    else:
        y_ref, of_ref, ob_ref = rest
        yc_ref = ocf_ref = None
    c = RET_CHUNK
    L, C = q_ref.shape[1], qc_ref.shape[1]
    h = pl.program_id(1)
    scale = HEAD_DIM ** -0.5
    row = lax.broadcasted_iota(jnp.int32, (c, c), 0).astype(F32)
    col = lax.broadcasted_iota(jnp.int32, (c, c), 1).astype(F32)
    rowp = lax.broadcasted_iota(jnp.int32, (c, HEAD_DIM), 0).astype(F32)

    def decays(lg, reverse):
        if reverse:
            rel, q_exp, k_exp = col - row, c - rowp, rowp
        else:
            rel, q_exp, k_exp = row - col, rowp + 1.0, (c - 1.0) - rowp
        intra = jnp.where(rel >= 0, jnp.exp(lg * jnp.maximum(rel, 0.0)), 0.0)
        return intra, jnp.exp(lg * q_exp), jnp.exp(lg * k_exp), jnp.exp(lg * c)

    def step(S, q, k, v, dec, want_out):
        intra, q_decay, k_decay, chunk_decay = dec
        o = None
        if want_out:
            qs = (q.astype(F32) * scale).astype(BF16)
            s = _dot_nt(qs, k) * intra
            o = _dot(s.astype(BF16), v) + _dot(qs, S.astype(BF16)) * q_decay
        kd_t = (k.astype(F32) * k_decay).T.astype(BF16)
        return S * chunk_decay + _dot(kd_t, v), o

    def gated(o, g):
        g = g.astype(F32)
        y = o * lax.rsqrt(jnp.mean(o * o, axis=-1, keepdims=True) + EPS)
        return y * (g * jax.nn.sigmoid(g))

    def lat(ref, j):
        return ref[0, pl.ds(pl.multiple_of(j * c, c), c), :]

    def ctx(ref, j):
        return ref[0, j * c:(j + 1) * c, :]

    dec_f = decays(ld_ref[0, h], False)
    dec_b = decays(ld_ref[1, h], True)
    S_f = jnp.zeros((HEAD_DIM, HEAD_DIM), F32)
    for j in range(C // c):
        S_f, o = step(S_f, ctx(qc_ref, j), ctx(kc_ref, j), ctx(vc_ref, j), dec_f, with_ctx)
        if with_ctx:
            ocf_ref[j * c:(j + 1) * c, :] = o
    S_b = jnp.zeros((HEAD_DIM, HEAD_DIM), F32)
    for j in reversed(range(C // c)):
        S_b, o = step(S_b, ctx(qc_ref, j), ctx(kc_ref, j), ctx(vc_ref, j), dec_b, with_ctx)
        if with_ctx:
            yc_ref[0, j * c:(j + 1) * c, :] = gated(o + ocf_ref[j * c:(j + 1) * c, :], ctx(gc_ref, j)).astype(yc_ref.dtype)

    n = L // c

    def both(t, carry):
        S_f, S_b = carry
        S_f, o_f = step(S_f, lat(q_ref, t), lat(k_ref, t), lat(v_ref, t), dec_f, True)
        of_ref[pl.ds(pl.multiple_of(t * c, c), c), :] = o_f
        j = n - 1 - t
        S_b, o_b = step(S_b, lat(q_ref, j), lat(k_ref, j), lat(v_ref, j), dec_b, True)
        ob_ref[pl.ds(pl.multiple_of(j * c, c), c), :] = o_b
        return S_f, S_b

    lax.fori_loop(0, n, both, (S_f, S_b), unroll=2)

    def finish(j, carry):
        sl = pl.ds(pl.multiple_of(j * c, c), c)
        y_ref[0, sl, :] = gated(of_ref[sl, :] + ob_ref[sl, :], lat(g_ref, j)).astype(y_ref.dtype)
        return carry

    lax.fori_loop(0, n, finish, 0, unroll=2)


def retention(proj, projc, log_decay, with_ctx):
    B, L, _ = proj.shape
    C = projc.shape[1]

    def lat(blk):
        return pl.BlockSpec((1, L, HEAD_DIM), lambda b, h: (b, 0, blk + h))

    def ctx(blk):
        return pl.BlockSpec((1, C, HEAD_DIM), lambda b, h: (b, 0, blk + h))

    out_shape = [jax.ShapeDtypeStruct((B, L, D_HEADS * HEAD_DIM), BF16)]
    out_specs = [pl.BlockSpec((1, L, HEAD_DIM), lambda b, h: (b, 0, h))]
    scratch = [pltpu.VMEM((L, HEAD_DIM), F32), pltpu.VMEM((L, HEAD_DIM), F32)]
    if with_ctx:
        out_shape.append(jax.ShapeDtypeStruct((B, C, D_HEADS * HEAD_DIM), BF16))
        out_specs.append(pl.BlockSpec((1, C, HEAD_DIM), lambda b, h: (b, 0, h)))
        scratch.append(pltpu.VMEM((C, HEAD_DIM), F32))
    outs = pl.pallas_call(
        functools.partial(_retention_kernel, with_ctx=with_ctx),
        out_shape=out_shape,
        grid=(B, D_HEADS),
        in_specs=[pl.BlockSpec(memory_space=pltpu.SMEM),
                  lat(D_Q), lat(D_K), lat(D_V), lat(D_G), ctx(D_Q), ctx(D_K), ctx(D_V), ctx(D_G)],
        out_specs=out_specs,
        scratch_shapes=scratch,
        compiler_params=_cparams(("arbitrary", "arbitrary")),
        name="retention_ctx" if with_ctx else "retention",
    )(log_decay.astype(F32), proj, proj, proj, proj, projc, projc, projc, projc)
    return (outs[0], outs[1]) if with_ctx else (outs[0], None)


def _ctx_attn_kernel(sink_ref, lam_ref, qk_ref, g_ref, o_ref, *, coef):
    def blk(idx):
        return qk_ref[0, :, idx * HEAD_DIM:(idx + 1) * HEAD_DIM]

    def softmax_out(q, k, v, sink=None, exp=jnp.exp):
        s = _dot_nt(q, k)
        m = jnp.max(s, axis=-1, keepdims=True)
        if sink is not None:
            m = jnp.maximum(m, sink)
        p = exp(s - m)
        denom = jnp.sum(p, axis=-1, keepdims=True)
        if sink is not None:
            denom = denom + jnp.exp(sink - m)
        return _dot(p.astype(BF16), v) / denom

    group = A_HEADS // A_KV_HEADS
    for h in range(A_HEADS):
        o = softmax_out(blk(A_Q + h), blk(A_K + h // group), blk(A_V + h // group), sink_ref[h])
        o_ref[0, :, h * HEAD_DIM:(h + 1) * HEAD_DIM] = o.astype(o_ref.dtype)
    for h in range(B_HEADS):
        o = softmax_out(blk(B_Q + h), blk(B_K + h), blk(B_V + h))
        o_ref[0, :, (A_HEADS + h) * HEAD_DIM:(A_HEADS + h + 1) * HEAD_DIM] = o.astype(o_ref.dtype)
    lane = lax.broadcasted_iota(jnp.int32, (qk_ref.shape[1], HEAD_DIM), 1)
    for h in range(C_HEADS):
        q, k, v = blk(C_Q + h), blk(C_K + h), blk(C_V + h)
        zero = jnp.zeros_like(q)
        o1 = softmax_out(jnp.where(lane < C_SUB_DIM, q, zero), k, v, exp=jnp.exp2)
        o2 = softmax_out(jnp.where(lane < C_SUB_DIM, zero, q), k, v, exp=jnp.exp2)
        o = o1 - lam_ref[0] * o2
        y = o * lax.rsqrt(jnp.mean(o * o, axis=-1, keepdims=True) + EPS) * g_ref[...] * coef
        base = A_HEADS + B_HEADS + h
        o_ref[0, :, base * HEAD_DIM:(base + 1) * HEAD_DIM] = y.astype(o_ref.dtype)


def ctx_attn(qkc, sink, lam, subln, coef):
    B, C, W = qkc.shape
    n_out = (A_HEADS + B_HEADS + C_HEADS) * HEAD_DIM
    return pl.pallas_call(
        functools.partial(_ctx_attn_kernel, coef=coef),
        out_shape=jax.ShapeDtypeStruct((B, C, n_out), BF16),
        grid=(B,),
        in_specs=[
            pl.BlockSpec(memory_space=pltpu.SMEM),
            pl.BlockSpec(memory_space=pltpu.SMEM),
            pl.BlockSpec((1, C, W), lambda b: (b, 0, 0)),
            pl.BlockSpec((1, HEAD_DIM), lambda b: (0, 0)),
        ],
        out_specs=pl.BlockSpec((1, C, n_out), lambda b: (b, 0, 0)),
        compiler_params=_cparams(("arbitrary",)),
        name="ctx_attn",
    )(sink, lam, qkc, subln.reshape(1, HEAD_DIM).astype(F32))


def _out_proj_kernel(*refs, n_parts):
    parts = refs[:n_parts]
    x_ref, gate_ref, w_ref, o_ref = refs[n_parts:]
    mix = jnp.concatenate([p[0] for p in parts], axis=-1)
    o_ref[0] = x_ref[0] + gate_ref[0] * _dot(mix, w_ref[0])


def out_proj(parts, x, gate, w, layer, tm, tn=512):
    B, L, D = x.shape
    part_specs = [pl.BlockSpec((1, tm, p.shape[2]), lambda b, i, j: (b, i, 0)) for p in parts]
    return pl.pallas_call(
        functools.partial(_out_proj_kernel, n_parts=len(parts)),
        out_shape=jax.ShapeDtypeStruct((B, L, D), F32),
        grid=(B, L // tm, D // tn),
        in_specs=part_specs + [
            pl.BlockSpec((1, tm, tn), lambda b, i, j: (b, i, j)),
            pl.BlockSpec((1, 1, tn), lambda b, i, j: (b, 0, j)),
            pl.BlockSpec((1, w.shape[1], tn), lambda b, i, j: (layer, 0, j)),
        ],
        out_specs=pl.BlockSpec((1, tm, tn), lambda b, i, j: (b, i, j)),
        compiler_params=_cparams(("arbitrary", "arbitrary", "arbitrary")),
        name="out_proj",
    )(*parts, x, gate, w)


def _ffn_prep_kernel(*refs, n_lat_tiles):
    if n_lat_tiles is None:
        x_ref, gain_ref, shift_ref, scale_ref, r_ref, h_ref, id_ref, w_ref = refs
        x = x_ref[...]
    else:
        xl_ref, xc_ref, gain_ref, shift_ref, scale_ref, r_ref, h_ref, id_ref, w_ref = refs
        x = jnp.where(pl.program_id(0) < n_lat_tiles, xl_ref[...], xc_ref[...])
    h = _modulated(x, gain_ref[...], shift_ref[0], scale_ref[0])
    h_ref[...] = _pack_bf16_pairs(h)

    logits = _dot3(h, r_ref[...])
    lane = lax.broadcasted_iota(jnp.int32, logits.shape, 1)
    big = jnp.int32(logits.shape[1])

    def first_argmax(vals):
        top = jnp.max(vals, axis=-1, keepdims=True)
        return top, jnp.min(jnp.where(vals == top, lane, big), axis=-1, keepdims=True)

    is_group = lane < N_GROUPS
    g_top, g_idx = first_argmax(jnp.where(is_group, logits, NEG_INF))
    g_val = 1.0 / jnp.sum(jnp.where(is_group, jnp.exp(logits - g_top), 0.0), axis=-1, keepdims=True)
    lo = N_GROUPS + EXPERTS_PER_GROUP * g_idx
    e_logits = jnp.where((lane >= lo) & (lane < lo + EXPERTS_PER_GROUP), logits, NEG_INF)
    v1, i1 = first_argmax(e_logits)
    v2, i2 = first_argmax(jnp.where(lane == i1, NEG_INF, e_logits))
    e21 = jnp.exp(v2 - v1)
    w1 = g_val / (1.0 + e21)
    w2 = g_val * e21 / (1.0 + e21)
    id_ref[...] = jnp.where(lane == 0, i1 - N_GROUPS, jnp.where(lane == 1, i2 - N_GROUPS, 0))
    w_ref[...] = jnp.where(lane == 0, w1, jnp.where(lane == 1, w2, 0.0))


def ffn_prep(x, xc, gain, shift, scale, r_cat, t=256):
    B, L, D = x.shape
    n_lat = B * L // t
    lat_per_seg = L // t
    if xc is None:
        T = B * L
        xs = [x.reshape(B * L, D)]
        x_specs = [pl.BlockSpec((t, D), lambda i: (i, 0))]
        seg = lambda i: (i // lat_per_seg, 0, 0)
        n_lat_tiles = None
    else:
        C = xc.shape[1]
        T = B * (L + C)
        xs = [x.reshape(B * L, D), xc.reshape(B * C, D)]
        x_specs = [pl.BlockSpec((t, D), lambda i: (jnp.minimum(i, n_lat - 1), 0)),
                   pl.BlockSpec((t, D), lambda i: (jnp.maximum(i - n_lat, 0), 0))]
        seg = lambda i: (jnp.minimum(i // lat_per_seg, B), 0, 0)
        n_lat_tiles = n_lat
    return pl.pallas_call(
        functools.partial(_ffn_prep_kernel, n_lat_tiles=n_lat_tiles),
        out_shape=[jax.ShapeDtypeStruct((T, D // 2), jnp.uint32),
                   jax.ShapeDtypeStruct((T, 128), jnp.int32),
                   jax.ShapeDtypeStruct((T, 128), F32)],
        grid=(T // t,),
        in_specs=x_specs + [
            pl.BlockSpec((1, D), lambda i: (0, 0)),
            pl.BlockSpec((1, 1, D), seg),
            pl.BlockSpec((1, 1, D), seg),
            pl.BlockSpec((D, 128), lambda i: (0, 0)),
        ],
        out_specs=[pl.BlockSpec((t, D // 2), lambda i: (i, 0)),
                   pl.BlockSpec((t, 128), lambda i: (i, 0)),
                   pl.BlockSpec((t, 128), lambda i: (i, 0))],
        compiler_params=_cparams(("arbitrary",)),
        name="ffn_prep",
    )(*xs, gain, shift, scale, r_cat)


def _route(ids, tm):
    A = ids.shape[0] * TOP_K
    flat_e = ids.reshape(A)
    onehot = (flat_e[:, None] == jnp.arange(N_EXPERTS, dtype=jnp.int32)[None, :]).astype(jnp.int32)
    csum = jnp.cumsum(onehot, axis=0)
    rank = jnp.sum(csum * onehot, axis=1) - 1
    counts = csum[-1]
    pcounts = (counts + tm - 1) // tm * tm
    pends = jnp.cumsum(pcounts)
    pstarts = pends - pcounts
    dest = jnp.sum(onehot * pstarts[None, :], axis=1) + rank
    NB = (A + N_EXPERTS * (tm - 1)) // tm
    row_tok = jnp.zeros((NB * tm,), jnp.int32).at[dest].set(jnp.arange(A, dtype=jnp.int32) // TOP_K)
    block_start = jnp.arange(NB, dtype=jnp.int32) * tm
    block_e = jnp.minimum(jnp.sum((pends[None, :] <= block_start[:, None]).astype(jnp.int32), axis=1), N_EXPERTS - 1)
    return dest.astype(jnp.int32), row_tok, block_e.astype(jnp.int32), (pends[-1:] // tm).astype(jnp.int32)


def _row_copy(src, src_row, dst, dst_row, sem):
    return pltpu.make_async_copy(src.at[pl.ds(src_row, 1), :], dst.at[pl.ds(dst_row, 1), :], sem)


def _expert_kernel(be_ref, nused_ref, tok_ref, h_ref, wg_ref, wu_ref, wd_ref, o_ref, xbuf, sem):
    i = pl.program_id(0)
    tm = o_ref.shape[0]
    n_used = nused_ref[0]

    def row(r, blk, slot):
        return _row_copy(h_ref, tok_ref[blk * tm + r], xbuf.at[slot], r, sem.at[slot])

    def gather(blk, slot):
        def issue(r2, carry):
            for k in range(2):
                row(2 * r2 + k, blk, slot).start(priority=k)
            return carry
        lax.fori_loop(0, tm // 2, issue, 0, unroll=4)

    def drain(slot):
        def wait(r, carry):
            row(r, 0, slot).wait()
            return carry
        lax.fori_loop(0, tm, wait, 0, unroll=8)

    @pl.when(i == 0)
    def _():
        gather(0, 0)

    @pl.when(i + 1 < n_used)
    def _():
        gather(i + 1, (i + 1) % 2)

    @pl.when(i < n_used)
    def _():
        slot = i % 2
        drain(slot)
        x = _unpack_bf16_pairs(xbuf[slot]).astype(BF16)
        a = _dot(x, wg_ref[0, 0])
        u = _dot(x, wu_ref[0, 0])
        hmid = (a * jax.nn.sigmoid(a) * u).astype(BF16)
        o_ref[...] = _pack_bf16_pairs(_dot(hmid, wd_ref[0, 0]))

    @pl.when(i >= n_used)
    def _():
        o_ref[...] = jnp.zeros(o_ref.shape, o_ref.dtype)


def experts(block_e, n_used, row_tok, h, w_gate, w_up, w_down, layer, tm):
    P = row_tok.shape[0]
    _, _, D, De = w_gate.shape
    w_idx = lambda i, be, nu, tok: (layer, be[jnp.minimum(i, nu[0] - 1)], 0, 0)
    return pl.pallas_call(
        _expert_kernel,
        out_shape=jax.ShapeDtypeStruct((P, D // 2), jnp.uint32),
        grid_spec=pltpu.PrefetchScalarGridSpec(
            num_scalar_prefetch=3,
            grid=(P // tm,),
            in_specs=[
                pl.BlockSpec(memory_space=pl.ANY),
                pl.BlockSpec((1, 1, D, De), w_idx),
                pl.BlockSpec((1, 1, D, De), w_idx),
                pl.BlockSpec((1, 1, De, D), w_idx),
            ],
            out_specs=pl.BlockSpec((tm, D // 2), lambda i, be, nu, tok: (i, 0)),
            scratch_shapes=[pltpu.VMEM((2, tm, D // 2), jnp.uint32), pltpu.SemaphoreType.DMA((2,))],
        ),
        compiler_params=_cparams(("arbitrary",)),
        name="moe_experts",
    )(block_e, n_used, row_tok, h, w_gate, w_up, w_down)


def _combine_kernel(dest_ref, x_ref, gate_ref, w_ref, y_ref, o_ref, buf, sem, *, tok_offset):
    t, D = x_ref.shape
    i = pl.program_id(0)
    slot = i % 2

    def gather(step, slot):
        base = (tok_offset + step * t) * TOP_K

        def issue(r, carry):
            for k in range(TOP_K):
                _row_copy(y_ref, dest_ref[base + r * TOP_K + k], buf.at[slot, k], r, sem.at[slot]).start(priority=k)
            return carry
        lax.fori_loop(0, t, issue, 0, unroll=4)

    def drain(r, carry):
        _row_copy(y_ref, 0, buf.at[slot, 0], 0, sem.at[slot]).wait()
        return carry

    @pl.when(i == 0)
    def _():
        gather(0, 0)

    @pl.when(i + 1 < pl.num_programs(0))
    def _():
        gather(i + 1, 1 - slot)

    lax.fori_loop(0, t * TOP_K, drain, 0, unroll=8)
    w = w_ref[...]
    y = w[:, 0:1] * _unpack_bf16_pairs(buf[slot, 0]) + w[:, 1:2] * _unpack_bf16_pairs(buf[slot, 1])
    o_ref[...] = x_ref[...] + gate_ref[0] * y


def combine(dest, x, gate, rw, ys, tok_offset, t=256):
    B, L, D = x.shape
    per_seg = L // t
    woff = tok_offset // t
    out = pl.pallas_call(
        functools.partial(_combine_kernel, tok_offset=tok_offset),
        out_shape=jax.ShapeDtypeStruct((B * L, D), F32),
        grid_spec=pltpu.PrefetchScalarGridSpec(
            num_scalar_prefetch=1,
            grid=(B * L // t,),
            in_specs=[
                pl.BlockSpec((t, D), lambda i, d: (i, 0)),
                pl.BlockSpec((1, 1, D), lambda i, d: (i // per_seg, 0, 0)),
                pl.BlockSpec((t, 128), lambda i, d: (woff + i, 0)),
                pl.BlockSpec(memory_space=pl.ANY),
            ],
            out_specs=pl.BlockSpec((t, D), lambda i, d: (i, 0)),
            scratch_shapes=[pltpu.VMEM((2, TOP_K, t, D // 2), jnp.uint32), pltpu.SemaphoreType.DMA((2,))],
        ),
        compiler_params=_cparams(("arbitrary",)),
        name="moe_combine",
    )(dest, x.reshape(B * L, D), gate, rw, ys)
    return out.reshape(B, L, D)


MOE_TM = 256


def kernel(x, c, ctx, c_ctx, w_mod, b_mod, norm_mix, norm_ffn, w_in, w_out, qk_norm_a, sink_a, qk_norm_b, rpb_b,
           qk_norm_c, lambda_c, subln_c, ret_log_decay, router_group, router_expert, w_gate, w_up, w_down):
    B, L, D = x.shape
    C = ctx.shape[1]
    depth = w_mod.shape[0]

    cvec = jnp.zeros((8, D), F32).at[:B].set(c).at[B].set(c_ctx)
    mod = mod_vectors(cvec, w_mod, b_mod).reshape(depth, 8, 6, D)

    cosa, sina = _rope_tables(L, HEAD_DIM)
    cosc, sinc = _rope_tables(L, C_SUB_DIM)
    one_tab = jnp.ones((C, HEAD_DIM), F32)
    zero_tab = jnp.zeros((C, HEAD_DIM), F32)

    w_in_b, w_out_b = w_in.astype(BF16), w_out.astype(BF16)
    w_gate_b, w_up_b, w_down_b = w_gate.astype(BF16), w_up.astype(BF16), w_down.astype(BF16)

    xc = ctx
    for l in range(depth):
        with_ctx = l < depth - 1
        lat = lambda j: mod[l, :B, j][:, None, :]
        cx = lambda j: jnp.broadcast_to(mod[l, B, j][None, None, :], (B, 1, D))
        gain_mix = norm_mix[l][None, :]

        proj = in_proj(x, gain_mix, lat(0), lat(1), w_in_b, l, tm=512)
        projc = in_proj(xc, gain_mix, cx(0), cx(1), w_in_b, l, tm=C)
        gain_c = jnp.tile(qk_norm_c[l], (1, HEAD_DIM // C_SUB_DIM))
        qk = qk_prep(proj, cosa, sina, cosc, sinc, qk_norm_a[l], qk_norm_b[l], gain_c)
        qkc = qk_prep(projc, one_tab, zero_tab, one_tab, zero_tab, qk_norm_a[l], qk_norm_b[l], gain_c)

        lam_init = 0.8 - 0.6 * math.exp(-0.3 * l)
        lam = _diff_lambda(lambda_c[l], lam_init).reshape(1)
        out_a = attn_a(qk, qkc, sink_a[l])
        out_b = attn_b(qk, qkc, _na_bias_tables(rpb_b[l], L // GRID_W))
        score_bound = (C_SUB_DIM ** 0.5) * jnp.max(jnp.abs(qk_norm_c[l, 0])) * jnp.max(jnp.abs(qk_norm_c[l, 1]))
        attn_c_l = functools.partial(attn_c, qk, qkc, lam, subln_c[l], 1.0 - lam_init)
        out_c = lax.cond(score_bound <= ATTN_C_SAFE_SCORE_BOUND,
                         lambda: attn_c_l(online=False), lambda: attn_c_l(online=True))
        out_d, out_dc = retention(proj, projc, ret_log_decay[l], with_ctx)

        x = out_proj([out_a, out_b, out_c, out_d], x, lat(2), w_out_b, l, tm=1024)
        if with_ctx:
            out_abc_c = ctx_attn(qkc, sink_a[l], lam, subln_c[l], 1.0 - lam_init)
            xc = out_proj([out_abc_c, out_dc], xc, cx(2), w_out_b, l, tm=C)

        r_cat = jnp.zeros((D, 128), F32).at[:, :N_GROUPS].set(router_group[l])
        r_cat = r_cat.at[:, N_GROUPS:N_GROUPS + N_EXPERTS].set(router_expert[l])
        if with_ctx:
            shift = jnp.concatenate([lat(3), cx(3)[:1]], axis=0)
            scale = jnp.concatenate([lat(4), cx(4)[:1]], axis=0)
        else:
            shift, scale = lat(3), lat(4)
        h2, rid, rw = ffn_prep(x, xc if with_ctx else None, norm_ffn[l][None, :], shift, scale, r_cat)
        dest, row_tok, block_e, n_used = _route(rid[:, :TOP_K], MOE_TM)
        ys = experts(block_e, n_used, row_tok, h2, w_gate_b, w_up_b, w_down_b, l, MOE_TM)
        x = combine(dest, x, lat(5), rw, ys, 0)
        if with_ctx:
            xc = combine(dest, xc, cx(5), rw, ys, B * L)
    return x
```

```python
import functools
import math

import jax
import jax.numpy as jnp
import numpy as np
from jax import lax
from jax.experimental import pallas as pl
from jax.experimental.pallas import tpu as pltpu

F32 = jnp.float32
BF16 = jnp.bfloat16

HEAD_DIM = 128
GRID_W = 64
A_HEADS, A_KV_HEADS, A_WINDOW = 8, 2, 128
B_HEADS, NA_ROWS, NA_COLS = 8, 8, 16
C_HEADS, C_SUB_DIM = 8, 64
D_HEADS = 8
RET_CHUNK = 256
ROPE_BASE = 10000.0
N_GROUPS, EXPERTS_PER_GROUP, TOP_K = 4, 8, 2
N_EXPERTS = N_GROUPS * EXPERTS_PER_GROUP
EPS = 1e-6
NEG_INF = -1e30

A_Q, A_K, A_V = 0, 8, 10
B_Q, B_K, B_V = 12, 20, 28
C_Q, C_K, C_V = 36, 44, 52
D_Q, D_K, D_V, D_G = 60, 68, 76, 84
ABC_BLOCKS = 60
IN_COLS = 92 * 128

VMEM_LIMIT = 56 * 1024 * 1024


def _cparams(sem, vmem=VMEM_LIMIT):
    return pltpu.CompilerParams(dimension_semantics=sem, vmem_limit_bytes=vmem)


def _dot(a, b):
    return jnp.dot(a, b, preferred_element_type=F32)


def _dot_nt(a, b):
    return lax.dot_general(a, b, (((1,), (1,)), ((), ())), preferred_element_type=F32)


def _split_bf16(x):
    hi = x.astype(BF16)
    lo = (x - hi.astype(F32)).astype(BF16)
    return hi, lo


def _pack_bf16_pairs(x):
    half = x.shape[1] // 2
    bits = lax.bitcast_convert_type(x.astype(BF16).astype(F32), jnp.uint32)
    return (bits[:, :half] >> 16) | (bits[:, half:] & jnp.uint32(0xFFFF0000))


def _unpack_bf16_pairs(p):
    lo = lax.bitcast_convert_type(p << 16, F32)
    hi = lax.bitcast_convert_type(p & jnp.uint32(0xFFFF0000), F32)
    return jnp.concatenate([lo, hi], axis=-1)


def _dot3(x, w):
    xh, xl = _split_bf16(x)
    wh, wl = _split_bf16(w)
    return _dot(xh, wh) + _dot(xl, wh) + _dot(xh, wl)


def _mod_kernel(c_ref, w_ref, b_ref, o_ref):
    c = c_ref[...]
    x = c * jax.nn.sigmoid(c)
    o_ref[0] = _dot3(x, w_ref[0]) + b_ref[0]


def mod_vectors(cvec, w_mod, b_mod, tn=512):
    depth, D, N = w_mod.shape
    return pl.pallas_call(
        _mod_kernel,
        out_shape=jax.ShapeDtypeStruct((depth, 8, N), F32),
        grid=(depth, N // tn),
        in_specs=[
            pl.BlockSpec((8, D), lambda l, j: (0, 0)),
            pl.BlockSpec((1, D, tn), lambda l, j: (l, 0, j)),
            pl.BlockSpec((1, 1, tn), lambda l, j: (l, 0, j)),
        ],
        out_specs=pl.BlockSpec((1, 8, tn), lambda l, j: (l, 0, j)),
        compiler_params=_cparams(("arbitrary", "arbitrary")),
        name="mod_vectors",
    )(cvec, w_mod, b_mod.reshape(depth, 1, N))


def _modulated(x, gain, shift, scale):
    ms = jnp.mean(x * x, axis=-1, keepdims=True)
    y = x * lax.rsqrt(ms + EPS) * gain
    return y * (1.0 + scale) + shift


def _in_proj_kernel(x_ref, gain_ref, shift_ref, scale_ref, w_ref, o_ref, h_ref):
    @pl.when(pl.program_id(2) == 0)
    def _():
        h_ref[...] = _modulated(x_ref[0], gain_ref[...], shift_ref[0], scale_ref[0]).astype(BF16)

    o_ref[0] = _dot(h_ref[...], w_ref[0]).astype(o_ref.dtype)


def in_proj(x, gain, shift, scale, w, layer, tm, tn=512):
    B, L, D = x.shape
    N = w.shape[2]
    return pl.pallas_call(
        _in_proj_kernel,
        out_shape=jax.ShapeDtypeStruct((B, L, N), BF16),
        grid=(B, L // tm, N // tn),
        in_specs=[
            pl.BlockSpec((1, tm, D), lambda b, i, j: (b, i, 0)),
            pl.BlockSpec((1, D), lambda b, i, j: (0, 0)),
            pl.BlockSpec((1, 1, D), lambda b, i, j: (b, 0, 0)),
            pl.BlockSpec((1, 1, D), lambda b, i, j: (b, 0, 0)),
            pl.BlockSpec((1, D, tn), lambda b, i, j: (layer, 0, j)),
        ],
        out_specs=pl.BlockSpec((1, tm, tn), lambda b, i, j: (b, i, j)),
        scratch_shapes=[pltpu.VMEM((tm, D), BF16)],
        compiler_params=_cparams(("arbitrary", "arbitrary", "arbitrary")),
        name="in_proj",
    )(x, gain, shift, scale, w)


def _rope_tables(L, dim):
    t = jnp.arange(L)
    row = (t // GRID_W).astype(F32)
    col = (t % GRID_W).astype(F32)
    quarter = dim // 4
    inv = ROPE_BASE ** (-jnp.arange(quarter, dtype=F32) / quarter)
    ar = row[:, None] * inv[None, :]
    ac = col[:, None] * inv[None, :]
    ang = jnp.concatenate([ar, ar, ac, ac], axis=-1)
    sign = jnp.where((jnp.arange(dim) % (dim // 2)) < quarter, -1.0, 1.0).astype(F32)
    reps = HEAD_DIM // dim
    return jnp.tile(jnp.cos(ang), (1, reps)), jnp.tile(jnp.sin(ang) * sign[None, :], (1, reps))


def _qk_prep_kernel(p_ref, cosa_ref, sina_ref, cosc_ref, sinc_ref, ga_ref, gb_ref, gc_ref, o_ref):
    t = p_ref.shape[1]
    lane = lax.broadcasted_iota(jnp.int32, (t, HEAD_DIM), 1)

    ri = lax.broadcasted_iota(jnp.int32, (HEAD_DIM, HEAD_DIM), 0)
    ci = lax.broadcasted_iota(jnp.int32, (HEAD_DIM, HEAD_DIM), 1)
    ones_full = jnp.ones((HEAD_DIM, HEAD_DIM), BF16)
    ones_halves = jnp.where((ri < C_SUB_DIM) == (ci < C_SUB_DIM), 1.0, 0.0).astype(BF16)

    def head_sums(x, ones):
        hi, lo = _split_bf16(x * x)
        return _dot(hi, ones) + _dot(lo, ones)

    def norm_full(x):
        return x * lax.rsqrt(head_sums(x, ones_full) * (1.0 / HEAD_DIM) + EPS)

    def norm_halves(x):
        return x * lax.rsqrt(head_sums(x, ones_halves) * (1.0 / C_SUB_DIM) + EPS)

    def rope(x, cos, sin, dim):
        quarter = dim // 4
        first = (lane % (dim // 2)) < quarter
        rot = jnp.where(first, pltpu.roll(x, HEAD_DIM - quarter, 1), pltpu.roll(x, quarter, 1))
        return x * cos + rot * sin

    cosa, sina = cosa_ref[...], sina_ref[...]
    cosc, sinc = cosc_ref[...], sinc_ref[...]
    scale_ab = HEAD_DIM ** -0.5
    scale_c = C_SUB_DIM ** -0.5 * math.log2(math.e)
    for blk in range(ABC_BLOCKS):
        sl = slice(blk * HEAD_DIM, (blk + 1) * HEAD_DIM)
        if A_V <= blk < B_Q or B_V <= blk < C_Q or C_V <= blk:
            o_ref[0, :, sl] = p_ref[0, :, sl]
            continue
        x = p_ref[0, :, sl].astype(F32)
        if blk < A_K:
            y = rope(norm_full(x) * ga_ref[0:1, :], cosa, sina, HEAD_DIM) * scale_ab
        elif blk < A_V:
            y = rope(norm_full(x) * ga_ref[1:2, :], cosa, sina, HEAD_DIM)
        elif blk < B_K:
            y = norm_full(x) * gb_ref[0:1, :] * scale_ab
        elif blk < B_V:
            y = norm_full(x) * gb_ref[1:2, :]
        elif blk < C_K:
            y = rope(norm_halves(x) * gc_ref[0:1, :], cosc, sinc, C_SUB_DIM) * scale_c
        else:
            y = rope(norm_halves(x) * gc_ref[1:2, :], cosc, sinc, C_SUB_DIM)
        o_ref[0, :, sl] = y.astype(o_ref.dtype)


def qk_prep(proj, cosa, sina, cosc, sinc, ga, gb, gc, t=256):
    B, L, _ = proj.shape
    W = ABC_BLOCKS * HEAD_DIM
    tab = pl.BlockSpec((t, HEAD_DIM), lambda b, i: (i, 0))
    par = pl.BlockSpec((2, HEAD_DIM), lambda b, i: (0, 0))
    return pl.pallas_call(
        _qk_prep_kernel,
        out_shape=jax.ShapeDtypeStruct((B, L, W), BF16),
        grid=(B, L // t),
        in_specs=[pl.BlockSpec((1, t, W), lambda b, i: (b, i, 0)), tab, tab, tab, tab, par, par, par],
        out_specs=pl.BlockSpec((1, t, W), lambda b, i: (b, i, 0)),
        compiler_params=_cparams(("arbitrary", "arbitrary")),
        name="qk_prep",
    )(proj, cosa, sina, cosc, sinc, ga, gb, gc)


def _attn_a_kernel(sink_ref, q_ref, k_ref, v_ref, kc_ref, vc_ref, o_ref, *, tq):
    L = k_ref.shape[1]
    kv, i = pl.program_id(1), pl.program_id(2)
    win = tq + 2 * A_WINDOW
    start = pl.multiple_of(jnp.clip(i * tq - A_WINDOW, 0, L - win), A_WINDOW)
    k = k_ref[0, pl.ds(start, win), :]
    v = v_ref[0, pl.ds(start, win), :]
    kc, vc = kc_ref[0], vc_ref[0]
    qpos = i * tq + lax.broadcasted_iota(jnp.int32, (tq, win), 0)
    kpos = start + lax.broadcasted_iota(jnp.int32, (tq, win), 1)
    in_window = jnp.abs(kpos - qpos) <= A_WINDOW
    group = A_HEADS // A_KV_HEADS
    for g in range(group):
        sl = slice(g * HEAD_DIM, (g + 1) * HEAD_DIM)
        q = q_ref[0, :, sl]
        s = jnp.where(in_window, _dot_nt(q, k), NEG_INF)
        sc = _dot_nt(q, kc)
        sink = sink_ref[kv * group + g]
        m = jnp.maximum(jnp.maximum(jnp.max(s, axis=-1, keepdims=True), jnp.max(sc, axis=-1, keepdims=True)), sink)
        p = jnp.exp(s - m)
        pc = jnp.exp(sc - m)
        denom = jnp.sum(p, axis=-1, keepdims=True) + jnp.sum(pc, axis=-1, keepdims=True) + jnp.exp(sink - m)
        o = _dot(p.astype(BF16), v) + _dot(pc.astype(BF16), vc)
        o_ref[0, :, sl] = (o / denom).astype(o_ref.dtype)


def attn_a(qk, qkc, sink, tq=512):
    B, L, _ = qk.shape
    C = qkc.shape[1]
    gw = (A_HEADS // A_KV_HEADS) * HEAD_DIM
    return pl.pallas_call(
        functools.partial(_attn_a_kernel, tq=tq),
        out_shape=jax.ShapeDtypeStruct((B, L, A_HEADS * HEAD_DIM), BF16),
        grid=(B, A_KV_HEADS, L // tq),
        in_specs=[
            pl.BlockSpec(memory_space=pltpu.SMEM),
            pl.BlockSpec((1, tq, gw), lambda b, kv, i: (b, i, kv)),
            pl.BlockSpec((1, L, HEAD_DIM), lambda b, kv, i: (b, 0, A_K + kv)),
            pl.BlockSpec((1, L, HEAD_DIM), lambda b, kv, i: (b, 0, A_V + kv)),
            pl.BlockSpec((1, C, HEAD_DIM), lambda b, kv, i: (b, 0, A_K + kv)),
            pl.BlockSpec((1, C, HEAD_DIM), lambda b, kv, i: (b, 0, A_V + kv)),
        ],
        out_specs=pl.BlockSpec((1, tq, gw), lambda b, kv, i: (b, i, kv)),
        compiler_params=_cparams(("arbitrary", "arbitrary", "arbitrary")),
        name="attn_a",
    )(sink, qk, qk, qk, qkc, qkc)


NA_QROWS = 8
NA_SPLIT = 2


def _na_bias_tables(rpb, R):
    W = GRID_W
    H = rpb.shape[0]
    wq = jnp.arange(W)
    cs = jnp.clip(wq - NA_COLS // 2, 0, W - NA_COLS)
    colmask = (wq[None, :] >= cs[:, None]) & (wq[None, :] < cs[:, None] + NA_COLS)
    rel_c = jnp.clip(wq[None, :] - wq[:, None] + (NA_COLS - 1), 0, 2 * NA_COLS - 2)
    per_row = jnp.where(colmask[None, None], rpb.astype(F32)[:, :, rel_c], NEG_INF)
    nq, nk = NA_QROWS, 2 * NA_QROWS
    tables = []
    for r0, ks in ((0, 0), (nq, nq - NA_ROWS // 2), (R - nq, R - nk)):
        qr = r0 + jnp.arange(nq)
        kr = ks + jnp.arange(nk)
        rs = jnp.clip(qr - NA_ROWS // 2, 0, R - NA_ROWS)
        valid = (kr[None, :] >= rs[:, None]) & (kr[None, :] < rs[:, None] + NA_ROWS)
        dr = jnp.clip(kr[None, :] - qr[:, None] + (NA_ROWS - 1), 0, 2 * NA_ROWS - 2)
        t = jnp.where(valid[None, :, :, None, None], per_row[:, dr], NEG_INF)
        tables.append(jnp.transpose(t, (0, 1, 3, 2, 4)).reshape(H, nq * W, nk * W))
    return jnp.stack(tables)


def _attn_b_kernel(q_ref, k_ref, v_ref, kc_ref, vc_ref, bias_ref, o_ref):
    L = k_ref.shape[1]
    R = L // GRID_W
    i = pl.program_id(2)
    nk = 2 * NA_QROWS
    krow = jnp.clip(i * NA_QROWS - NA_ROWS // 2, 0, R - nk)
    start = pl.multiple_of(krow * GRID_W, (NA_ROWS // 2) * GRID_W)
    k = k_ref[0, pl.ds(start, nk * GRID_W), :]
    v = v_ref[0, pl.ds(start, nk * GRID_W), :]
    kc, vc = kc_ref[0], vc_ref[0]
    sub = q_ref.shape[1] // NA_SPLIT
    for part in range(NA_SPLIT):
        rows = slice(part * sub, (part + 1) * sub)
        q = q_ref[0, rows, :]
        s = _dot_nt(q, k) + bias_ref[0, 0, rows, :]
        sc = _dot_nt(q, kc)
        m = jnp.maximum(jnp.max(s, axis=-1, keepdims=True), jnp.max(sc, axis=-1, keepdims=True))
        p = jnp.exp(s - m)
        pc = jnp.exp(sc - m)
        denom = jnp.sum(p, axis=-1, keepdims=True) + jnp.sum(pc, axis=-1, keepdims=True)
        o = _dot(p.astype(BF16), v) + _dot(pc.astype(BF16), vc)
        o_ref[0, rows, :] = (o / denom).astype(o_ref.dtype)


def attn_b(qk, qkc, bias):
    B, L, _ = qk.shape
    C = qkc.shape[1]
    tq = NA_QROWS * GRID_W
    nblk = L // tq

    def bias_idx(h, b, i):
        return (jnp.where(i == 0, 0, jnp.where(i == nblk - 1, 2, 1)), h, 0, 0)

    return pl.pallas_call(
        _attn_b_kernel,
        out_shape=jax.ShapeDtypeStruct((B, L, B_HEADS * HEAD_DIM), BF16),
        grid=(B_HEADS, B, nblk),
        in_specs=[
            pl.BlockSpec((1, tq, HEAD_DIM), lambda h, b, i: (b, i, B_Q + h)),
            pl.BlockSpec((1, L, HEAD_DIM), lambda h, b, i: (b, 0, B_K + h)),
            pl.BlockSpec((1, L, HEAD_DIM), lambda h, b, i: (b, 0, B_V + h)),
            pl.BlockSpec((1, C, HEAD_DIM), lambda h, b, i: (b, 0, B_K + h)),
            pl.BlockSpec((1, C, HEAD_DIM), lambda h, b, i: (b, 0, B_V + h)),
            pl.BlockSpec((1, 1, tq, 2 * tq), bias_idx),
        ],
        out_specs=pl.BlockSpec((1, tq, HEAD_DIM), lambda h, b, i: (b, i, h)),
        compiler_params=_cparams(("arbitrary", "arbitrary", "arbitrary")),
        name="attn_b",
    )(qk, qk, qk, qkc, qkc, bias)


def _diff_lambda(lambda_params, lam_init):
    lp = lambda_params.astype(F32)
    return jnp.exp(jnp.sum(lp[0] * lp[1])) - jnp.exp(jnp.sum(lp[2] * lp[3])) + lam_init


ATTN_C_SAFE_SCORE_BOUND = 40.0


def _attn_c_kernel(lam_ref, q_ref, k_ref, v_ref, kc_ref, vc_ref, g_ref, o_ref, vt_ref, vct_ref, *scratch,
                   tk, coef, online):
    if online:
        m_ref, l_ref, acc_ref = scratch
        m_ref[...] = jnp.full(m_ref.shape, NEG_INF, F32)
    else:
        l_ref, acc_ref = scratch
    nk = vt_ref.shape[0]

    @pl.when(pl.program_id(2) == 0)
    def _():
        for j in range(nk):
            vt_ref[j] = v_ref[0, j * tk:(j + 1) * tk, :].astype(F32).T.astype(BF16)
        vct_ref[...] = vc_ref[0].astype(F32).T.astype(BF16)

    q = q_ref[0]
    lane = lax.broadcasted_iota(jnp.int32, q.shape, 1)
    zero = jnp.zeros_like(q)
    qs = (jnp.where(lane < C_SUB_DIM, q, zero), jnp.where(lane < C_SUB_DIM, zero, q))

    l_ref[...] = jnp.zeros(l_ref.shape, F32)
    acc_ref[...] = jnp.zeros(acc_ref.shape, F32)

    def accumulate(kblk, vtblk):
        for s_idx in range(2):
            s = _dot_nt(kblk, qs[s_idx])
            if online:
                m_old = m_ref[s_idx]
                m_new = jnp.maximum(m_old, jnp.max(s, axis=0, keepdims=True))
                alpha = jnp.exp2(m_old - m_new)
                p = jnp.exp2(s - m_new)
                l_ref[s_idx] = alpha * l_ref[s_idx] + jnp.sum(p, axis=0, keepdims=True)
                acc_ref[s_idx] = alpha * acc_ref[s_idx] + _dot(vtblk, p.astype(BF16))
                m_ref[s_idx] = m_new
            else:
                p = jnp.exp2(s)
                l_ref[s_idx] += jnp.sum(p, axis=0, keepdims=True)
                acc_ref[s_idx] += _dot(vtblk, p.astype(BF16))

    def body(j, carry):
        accumulate(k_ref[0, pl.ds(pl.multiple_of(j * tk, tk), tk), :], vt_ref[j])
        return carry

    lax.fori_loop(0, nk, body, 0, unroll=4)
    accumulate(kc_ref[0], vct_ref[...])

    o = acc_ref[0] / l_ref[0] - lam_ref[0] * (acc_ref[1] / l_ref[1])
    y = o * lax.rsqrt(jnp.mean(o * o, axis=0, keepdims=True) + EPS) * g_ref[...] * coef
    o_ref[0] = y.T.astype(o_ref.dtype)


def attn_c(qk, qkc, lam, subln, coef, online, tq=1024, tk=512):
    B, L, _ = qk.shape
    C = qkc.shape[1]
    H, dv = C_HEADS, HEAD_DIM
    stats = [pltpu.VMEM((2, 1, tq), F32)] * (2 if online else 1)
    return pl.pallas_call(
        functools.partial(_attn_c_kernel, tk=tk, coef=coef, online=online),
        out_shape=jax.ShapeDtypeStruct((B, L, H * dv), BF16),
        grid=(B, H, L // tq),
        in_specs=[
            pl.BlockSpec(memory_space=pltpu.SMEM),
            pl.BlockSpec((1, tq, HEAD_DIM), lambda b, h, i: (b, i, C_Q + h)),
            pl.BlockSpec((1, L, HEAD_DIM), lambda b, h, i: (b, 0, C_K + h)),
            pl.BlockSpec((1, L, HEAD_DIM), lambda b, h, i: (b, 0, C_V + h)),
            pl.BlockSpec((1, C, HEAD_DIM), lambda b, h, i: (b, 0, C_K + h)),
            pl.BlockSpec((1, C, HEAD_DIM), lambda b, h, i: (b, 0, C_V + h)),
            pl.BlockSpec((dv, 1), lambda b, h, i: (0, 0)),
        ],
        out_specs=pl.BlockSpec((1, tq, dv), lambda b, h, i: (b, i, h)),
        scratch_shapes=[pltpu.VMEM((L // tk, dv, tk), BF16), pltpu.VMEM((dv, C), BF16)]
        + stats + [pltpu.VMEM((2, dv, tq), F32)],
        compiler_params=_cparams(("arbitrary", "arbitrary", "arbitrary")),
        name="attn_c_online" if online else "attn_c",
    )(lam, qk, qk, qk, qkc, qkc, subln.reshape(dv, 1).astype(F32))


def _retention_kernel(ld_ref, q_ref, k_ref, v_ref, g_ref, qc_ref, kc_ref, vc_ref, gc_ref, *rest, with_ctx):
    if with_ctx:
        y_ref, yc_ref, of_ref, ob_ref, ocf_ref = rest
    else:
        y_ref, of_ref, ob_ref = rest
        yc_ref = ocf_ref = None
    c = RET_CHUNK
    L, C = q_ref.shape[1], qc_ref.shape[1]
    h = pl.program_id(1)
    scale = HEAD_DIM ** -0.5
    row = lax.broadcasted_iota(jnp.int32, (c, c), 0).astype(F32)
    col = lax.broadcasted_iota(jnp.int32, (c, c), 1).astype(F32)
    rowp = lax.broadcasted_iota(jnp.int32, (c, HEAD_DIM), 0).astype(F32)

    def decays(lg, reverse):
        if reverse:
            rel, q_exp, k_exp = col - row, c - rowp, rowp
        else:
            rel, q_exp, k_exp = row - col, rowp + 1.0, (c - 1.0) - rowp
        intra = jnp.where(rel >= 0, jnp.exp(lg * jnp.maximum(rel, 0.0)), 0.0)
        return intra, jnp.exp(lg * q_exp), jnp.exp(lg * k_exp), jnp.exp(lg * c)

    def step(S, q, k, v, dec, want_out):
        intra, q_decay, k_decay, chunk_decay = dec
        o = None
        if want_out:
            qs = (q.astype(F32) * scale).astype(BF16)
            s = _dot_nt(qs, k) * intra
            o = _dot(s.astype(BF16), v) + _dot(qs, S.astype(BF16)) * q_decay
        kd_t = (k.astype(F32) * k_decay).T.astype(BF16)
        return S * chunk_decay + _dot(kd_t, v), o

    def gated(o, g):
        g = g.astype(F32)
        y = o * lax.rsqrt(jnp.mean(o * o, axis=-1, keepdims=True) + EPS)
        return y * (g * jax.nn.sigmoid(g))

    def lat(ref, j):
        return ref[0, pl.ds(pl.multiple_of(j * c, c), c), :]

    def ctx(ref, j):
        return ref[0, j * c:(j + 1) * c, :]

    dec_f = decays(ld_ref[0, h], False)
    dec_b = decays(ld_ref[1, h], True)
    S_f = jnp.zeros((HEAD_DIM, HEAD_DIM), F32)
    for j in range(C // c):
        S_f, o = step(S_f, ctx(qc_ref, j), ctx(kc_ref, j), ctx(vc_ref, j), dec_f, with_ctx)
        if with_ctx:
            ocf_ref[j * c:(j + 1) * c, :] = o
    S_b = jnp.zeros((HEAD_DIM, HEAD_DIM), F32)
    for j in reversed(range(C // c)):
        S_b, o = step(S_b, ctx(qc_ref, j), ctx(kc_ref, j), ctx(vc_ref, j), dec_b, with_ctx)
        if with_ctx:
            yc_ref[0, j * c:(j + 1) * c, :] = gated(o + ocf_ref[j * c:(j + 1) * c, :], ctx(gc_ref, j)).astype(yc_ref.dtype)

    n = L // c

    def both(t, carry):
        S_f, S_b = carry
        S_f, o_f = step(S_f, lat(q_ref, t), lat(k_ref, t), lat(v_ref, t), dec_f, True)
        of_ref[pl.ds(pl.multiple_of(t * c, c), c), :] = o_f
        j = n - 1 - t
        S_b, o_b = step(S_b, lat(q_ref, j), lat(k_ref, j), lat(v_ref, j), dec_b, True)
        ob_ref[pl.ds(pl.multiple_of(j * c, c), c), :] = o_b
        return S_f, S_b

    lax.fori_loop(0, n, both, (S_f, S_b), unroll=2)

    def finish(j, carry):
        sl = pl.ds(pl.multiple_of(j * c, c), c)
        y_ref[0, sl, :] = gated(of_ref[sl, :] + ob_ref[sl, :], lat(g_ref, j)).astype(y_ref.dtype)
        return carry

    lax.fori_loop(0, n, finish, 0, unroll=2)


def retention(proj, projc, log_decay, with_ctx):
    B, L, _ = proj.shape
    C = projc.shape[1]

    def lat(blk):
        return pl.BlockSpec((1, L, HEAD_DIM), lambda b, h: (b, 0, blk + h))

    def ctx(blk):
        return pl.BlockSpec((1, C, HEAD_DIM), lambda b, h: (b, 0, blk + h))

    out_shape = [jax.ShapeDtypeStruct((B, L, D_HEADS * HEAD_DIM), BF16)]
    out_specs = [pl.BlockSpec((1, L, HEAD_DIM), lambda b, h: (b, 0, h))]
    scratch = [pltpu.VMEM((L, HEAD_DIM), F32), pltpu.VMEM((L, HEAD_DIM), F32)]
    if with_ctx:
        out_shape.append(jax.ShapeDtypeStruct((B, C, D_HEADS * HEAD_DIM), BF16))
        out_specs.append(pl.BlockSpec((1, C, HEAD_DIM), lambda b, h: (b, 0, h)))
        scratch.append(pltpu.VMEM((C, HEAD_DIM), F32))
    outs = pl.pallas_call(
        functools.partial(_retention_kernel, with_ctx=with_ctx),
        out_shape=out_shape,
        grid=(B, D_HEADS),
        in_specs=[pl.BlockSpec(memory_space=pltpu.SMEM),
                  lat(D_Q), lat(D_K), lat(D_V), lat(D_G), ctx(D_Q), ctx(D_K), ctx(D_V), ctx(D_G)],
        out_specs=out_specs,
        scratch_shapes=scratch,
        compiler_params=_cparams(("arbitrary", "arbitrary")),
        name="retention_ctx" if with_ctx else "retention",
    )(log_decay.astype(F32), proj, proj, proj, proj, projc, projc, projc, projc)
    return (outs[0], outs[1]) if with_ctx else (outs[0], None)


def _ctx_attn_kernel(sink_ref, lam_ref, qk_ref, g_ref, o_ref, *, coef):
    def blk(idx):
        return qk_ref[0, :, idx * HEAD_DIM:(idx + 1) * HEAD_DIM]

    def softmax_out(q, k, v, sink=None, exp=jnp.exp):
        s = _dot_nt(q, k)
        m = jnp.max(s, axis=-1, keepdims=True)
        if sink is not None:
            m = jnp.maximum(m, sink)
        p = exp(s - m)
        denom = jnp.sum(p, axis=-1, keepdims=True)
        if sink is not None:
            denom = denom + jnp.exp(sink - m)
        return _dot(p.astype(BF16), v) / denom

    group = A_HEADS // A_KV_HEADS
    for h in range(A_HEADS):
        o = softmax_out(blk(A_Q + h), blk(A_K + h // group), blk(A_V + h // group), sink_ref[h])
        o_ref[0, :, h * HEAD_DIM:(h + 1) * HEAD_DIM] = o.astype(o_ref.dtype)
    for h in range(B_HEADS):
        o = softmax_out(blk(B_Q + h), blk(B_K + h), blk(B_V + h))
        o_ref[0, :, (A_HEADS + h) * HEAD_DIM:(A_HEADS + h + 1) * HEAD_DIM] = o.astype(o_ref.dtype)
    lane = lax.broadcasted_iota(jnp.int32, (qk_ref.shape[1], HEAD_DIM), 1)
    for h in range(C_HEADS):
        q, k, v = blk(C_Q + h), blk(C_K + h), blk(C_V + h)
        zero = jnp.zeros_like(q)
        o1 = softmax_out(jnp.where(lane < C_SUB_DIM, q, zero), k, v, exp=jnp.exp2)
        o2 = softmax_out(jnp.where(lane < C_SUB_DIM, zero, q), k, v, exp=jnp.exp2)
        o = o1 - lam_ref[0] * o2
        y = o * lax.rsqrt(jnp.mean(o * o, axis=-1, keepdims=True) + EPS) * g_ref[...] * coef
        base = A_HEADS + B_HEADS + h
        o_ref[0, :, base * HEAD_DIM:(base + 1) * HEAD_DIM] = y.astype(o_ref.dtype)


def ctx_attn(qkc, sink, lam, subln, coef):
    B, C, W = qkc.shape
    n_out = (A_HEADS + B_HEADS + C_HEADS) * HEAD_DIM
    return pl.pallas_call(
        functools.partial(_ctx_attn_kernel, coef=coef),
        out_shape=jax.ShapeDtypeStruct((B, C, n_out), BF16),
        grid=(B,),
        in_specs=[
            pl.BlockSpec(memory_space=pltpu.SMEM),
            pl.BlockSpec(memory_space=pltpu.SMEM),
            pl.BlockSpec((1, C, W), lambda b: (b, 0, 0)),
            pl.BlockSpec((1, HEAD_DIM), lambda b: (0, 0)),
        ],
        out_specs=pl.BlockSpec((1, C, n_out), lambda b: (b, 0, 0)),
        compiler_params=_cparams(("arbitrary",)),
        name="ctx_attn",
    )(sink, lam, qkc, subln.reshape(1, HEAD_DIM).astype(F32))


def _out_proj_kernel(*refs, n_parts):
    parts = refs[:n_parts]
    x_ref, gate_ref, w_ref, o_ref = refs[n_parts:]
    mix = jnp.concatenate([p[0] for p in parts], axis=-1)
    o_ref[0] = x_ref[0] + gate_ref[0] * _dot(mix, w_ref[0])


def out_proj(parts, x, gate, w, layer, tm, tn=512):
    B, L, D = x.shape
    part_specs = [pl.BlockSpec((1, tm, p.shape[2]), lambda b, i, j: (b, i, 0)) for p in parts]
    return pl.pallas_call(
        functools.partial(_out_proj_kernel, n_parts=len(parts)),
        out_shape=jax.ShapeDtypeStruct((B, L, D), F32),
        grid=(B, L // tm, D // tn),
        in_specs=part_specs + [
            pl.BlockSpec((1, tm, tn), lambda b, i, j: (b, i, j)),
            pl.BlockSpec((1, 1, tn), lambda b, i, j: (b, 0, j)),
            pl.BlockSpec((1, w.shape[1], tn), lambda b, i, j: (layer, 0, j)),
        ],
        out_specs=pl.BlockSpec((1, tm, tn), lambda b, i, j: (b, i, j)),
        compiler_params=_cparams(("arbitrary", "arbitrary", "arbitrary")),
        name="out_proj",
    )(*parts, x, gate, w)


def _ffn_prep_kernel(*refs, n_lat_tiles):
    if n_lat_tiles is None:
        x_ref, gain_ref, shift_ref, scale_ref, r_ref, h_ref, id_ref, w_ref = refs
        x = x_ref[...]
    else:
        xl_ref, xc_ref, gain_ref, shift_ref, scale_ref, r_ref, h_ref, id_ref, w_ref = refs
        x = jnp.where(pl.program_id(0) < n_lat_tiles, xl_ref[...], xc_ref[...])
    h = _modulated(x, gain_ref[...], shift_ref[0], scale_ref[0])
    h_ref[...] = _pack_bf16_pairs(h)

    logits = _dot3(h, r_ref[...])
    lane = lax.broadcasted_iota(jnp.int32, logits.shape, 1)
    big = jnp.int32(logits.shape[1])

    def first_argmax(vals):
        top = jnp.max(vals, axis=-1, keepdims=True)
        return top, jnp.min(jnp.where(vals == top, lane, big), axis=-1, keepdims=True)

    is_group = lane < N_GROUPS
    g_top, g_idx = first_argmax(jnp.where(is_group, logits, NEG_INF))
    g_val = 1.0 / jnp.sum(jnp.where(is_group, jnp.exp(logits - g_top), 0.0), axis=-1, keepdims=True)
    lo = N_GROUPS + EXPERTS_PER_GROUP * g_idx
    e_logits = jnp.where((lane >= lo) & (lane < lo + EXPERTS_PER_GROUP), logits, NEG_INF)
    v1, i1 = first_argmax(e_logits)
    v2, i2 = first_argmax(jnp.where(lane == i1, NEG_INF, e_logits))
    e21 = jnp.exp(v2 - v1)
    w1 = g_val / (1.0 + e21)
    w2 = g_val * e21 / (1.0 + e21)
    id_ref[...] = jnp.where(lane == 0, i1 - N_GROUPS, jnp.where(lane == 1, i2 - N_GROUPS, 0))
    w_ref[...] = jnp.where(lane == 0, w1, jnp.where(lane == 1, w2, 0.0))


def ffn_prep(x, xc, gain, shift, scale, r_cat, t=256):
    B, L, D = x.shape
    n_lat = B * L // t
    lat_per_seg = L // t
    if xc is None:
        T = B * L
        xs = [x.reshape(B * L, D)]
        x_specs = [pl.BlockSpec((t, D), lambda i: (i, 0))]
        seg = lambda i: (i // lat_per_seg, 0, 0)
        n_lat_tiles = None
    else:
        C = xc.shape[1]
        T = B * (L + C)
        xs = [x.reshape(B * L, D), xc.reshape(B * C, D)]
        x_specs = [pl.BlockSpec((t, D), lambda i: (jnp.minimum(i, n_lat - 1), 0)),
                   pl.BlockSpec((t, D), lambda i: (jnp.maximum(i - n_lat, 0), 0))]
        seg = lambda i: (jnp.minimum(i // lat_per_seg, B), 0, 0)
        n_lat_tiles = n_lat
    return pl.pallas_call(
        functools.partial(_ffn_prep_kernel, n_lat_tiles=n_lat_tiles),
        out_shape=[jax.ShapeDtypeStruct((T, D // 2), jnp.uint32),
                   jax.ShapeDtypeStruct((T, 128), jnp.int32),
                   jax.ShapeDtypeStruct((T, 128), F32)],
        grid=(T // t,),
        in_specs=x_specs + [
            pl.BlockSpec((1, D), lambda i: (0, 0)),
            pl.BlockSpec((1, 1, D), seg),
            pl.BlockSpec((1, 1, D), seg),
            pl.BlockSpec((D, 128), lambda i: (0, 0)),
        ],
        out_specs=[pl.BlockSpec((t, D // 2), lambda i: (i, 0)),
                   pl.BlockSpec((t, 128), lambda i: (i, 0)),
                   pl.BlockSpec((t, 128), lambda i: (i, 0))],
        compiler_params=_cparams(("arbitrary",)),
        name="ffn_prep",
    )(*xs, gain, shift, scale, r_cat)


def _route(ids, tm):
    A = ids.shape[0] * TOP_K
    flat_e = ids.reshape(A)
    onehot = (flat_e[:, None] == jnp.arange(N_EXPERTS, dtype=jnp.int32)[None, :]).astype(jnp.int32)
    csum = jnp.cumsum(onehot, axis=0)
    rank = jnp.sum(csum * onehot, axis=1) - 1
    counts = csum[-1]
    pcounts = (counts + tm - 1) // tm * tm
    pends = jnp.cumsum(pcounts)
    pstarts = pends - pcounts
    dest = jnp.sum(onehot * pstarts[None, :], axis=1) + rank
    NB = (A + N_EXPERTS * (tm - 1)) // tm
    row_tok = jnp.zeros((NB * tm,), jnp.int32).at[dest].set(jnp.arange(A, dtype=jnp.int32) // TOP_K)
    block_start = jnp.arange(NB, dtype=jnp.int32) * tm
    block_e = jnp.minimum(jnp.sum((pends[None, :] <= block_start[:, None]).astype(jnp.int32), axis=1), N_EXPERTS - 1)
    return dest.astype(jnp.int32), row_tok, block_e.astype(jnp.int32), (pends[-1:] // tm).astype(jnp.int32)


def _row_copy(src, src_row, dst, dst_row, sem):
    return pltpu.make_async_copy(src.at[pl.ds(src_row, 1), :], dst.at[pl.ds(dst_row, 1), :], sem)


def _expert_kernel(be_ref, nused_ref, tok_ref, h_ref, wg_ref, wu_ref, wd_ref, o_ref, xbuf, sem):
    i = pl.program_id(0)
    tm = o_ref.shape[0]
    n_used = nused_ref[0]

    def row(r, blk, slot):
        return _row_copy(h_ref, tok_ref[blk * tm + r], xbuf.at[slot], r, sem.at[slot])

    def gather(blk, slot):
        def issue(r2, carry):
            for k in range(2):
                row(2 * r2 + k, blk, slot).start(priority=k)
            return carry
        lax.fori_loop(0, tm // 2, issue, 0, unroll=4)

    def drain(slot):
        def wait(r, carry):
            row(r, 0, slot).wait()
            return carry
        lax.fori_loop(0, tm, wait, 0, unroll=8)

    @pl.when(i == 0)
    def _():
        gather(0, 0)

    @pl.when(i + 1 < n_used)
    def _():
        gather(i + 1, (i + 1) % 2)

    @pl.when(i < n_used)
    def _():
        slot = i % 2
        drain(slot)
        x = _unpack_bf16_pairs(xbuf[slot]).astype(BF16)
        a = _dot(x, wg_ref[0, 0])
        u = _dot(x, wu_ref[0, 0])
        hmid = (a * jax.nn.sigmoid(a) * u).astype(BF16)
        o_ref[...] = _pack_bf16_pairs(_dot(hmid, wd_ref[0, 0]))

    @pl.when(i >= n_used)
    def _():
        o_ref[...] = jnp.zeros(o_ref.shape, o_ref.dtype)


def experts(block_e, n_used, row_tok, h, w_gate, w_up, w_down, layer, tm):
    P = row_tok.shape[0]
    _, _, D, De = w_gate.shape
    w_idx = lambda i, be, nu, tok: (layer, be[jnp.minimum(i, nu[0] - 1)], 0, 0)
    return pl.pallas_call(
        _expert_kernel,
        out_shape=jax.ShapeDtypeStruct((P, D // 2), jnp.uint32),
        grid_spec=pltpu.PrefetchScalarGridSpec(
            num_scalar_prefetch=3,
            grid=(P // tm,),
            in_specs=[
                pl.BlockSpec(memory_space=pl.ANY),
                pl.BlockSpec((1, 1, D, De), w_idx),
                pl.BlockSpec((1, 1, D, De), w_idx),
                pl.BlockSpec((1, 1, De, D), w_idx),
            ],
            out_specs=pl.BlockSpec((tm, D // 2), lambda i, be, nu, tok: (i, 0)),
            scratch_shapes=[pltpu.VMEM((2, tm, D // 2), jnp.uint32), pltpu.SemaphoreType.DMA((2,))],
        ),
        compiler_params=_cparams(("arbitrary",)),
        name="moe_experts",
    )(block_e, n_used, row_tok, h, w_gate, w_up, w_down)


def _combine_kernel(dest_ref, x_ref, gate_ref, w_ref, y_ref, o_ref, buf, sem, *, tok_offset):
    t, D = x_ref.shape
    i = pl.program_id(0)
    slot = i % 2

    def gather(step, slot):
        base = (tok_offset + step * t) * TOP_K

        def issue(r, carry):
            for k in range(TOP_K):
                _row_copy(y_ref, dest_ref[base + r * TOP_K + k], buf.at[slot, k], r, sem.at[slot]).start(priority=k)
            return carry
        lax.fori_loop(0, t, issue, 0, unroll=4)

    def drain(r, carry):
        _row_copy(y_ref, 0, buf.at[slot, 0], 0, sem.at[slot]).wait()
        return carry

    @pl.when(i == 0)
    def _():
        gather(0, 0)

    @pl.when(i + 1 < pl.num_programs(0))
    def _():
        gather(i + 1, 1 - slot)

    lax.fori_loop(0, t * TOP_K, drain, 0, unroll=8)
    w = w_ref[...]
    y = w[:, 0:1] * _unpack_bf16_pairs(buf[slot, 0]) + w[:, 1:2] * _unpack_bf16_pairs(buf[slot, 1])
    o_ref[...] = x_ref[...] + gate_ref[0] * y


def combine(dest, x, gate, rw, ys, tok_offset, t=256):
    B, L, D = x.shape
    per_seg = L // t
    woff = tok_offset // t
    out = pl.pallas_call(
        functools.partial(_combine_kernel, tok_offset=tok_offset),
        out_shape=jax.ShapeDtypeStruct((B * L, D), F32),
        grid_spec=pltpu.PrefetchScalarGridSpec(
            num_scalar_prefetch=1,
            grid=(B * L // t,),
            in_specs=[
                pl.BlockSpec((t, D), lambda i, d: (i, 0)),
                pl.BlockSpec((1, 1, D), lambda i, d: (i // per_seg, 0, 0)),
                pl.BlockSpec((t, 128), lambda i, d: (woff + i, 0)),
                pl.BlockSpec(memory_space=pl.ANY),
            ],
            out_specs=pl.BlockSpec((t, D), lambda i, d: (i, 0)),
            scratch_shapes=[pltpu.VMEM((2, TOP_K, t, D // 2), jnp.uint32), pltpu.SemaphoreType.DMA((2,))],
        ),
        compiler_params=_cparams(("arbitrary",)),
        name="moe_combine",
    )(dest, x.reshape(B * L, D), gate, rw, ys)
    return out.reshape(B, L, D)


MOE_TM = 256


def kernel(x, c, ctx, c_ctx, w_mod, b_mod, norm_mix, norm_ffn, w_in, w_out, qk_norm_a, sink_a, qk_norm_b, rpb_b,
           qk_norm_c, lambda_c, subln_c, ret_log_decay, router_group, router_expert, w_gate, w_up, w_down):
    B, L, D = x.shape
    C = ctx.shape[1]
    depth = w_mod.shape[0]

    cvec = jnp.zeros((8, D), F32).at[:B].set(c).at[B].set(c_ctx)
    mod = mod_vectors(cvec, w_mod, b_mod).reshape(depth, 8, 6, D)

    cosa, sina = _rope_tables(L, HEAD_DIM)
    cosc, sinc = _rope_tables(L, C_SUB_DIM)
    one_tab = jnp.ones((C, HEAD_DIM), F32)
    zero_tab = jnp.zeros((C, HEAD_DIM), F32)

    w_in_b, w_out_b = w_in.astype(BF16), w_out.astype(BF16)
    w_gate_b, w_up_b, w_down_b = w_gate.astype(BF16), w_up.astype(BF16), w_down.astype(BF16)

    xc = ctx
    for l in range(depth):
        with_ctx = l < depth - 1
        lat = lambda j: mod[l, :B, j][:, None, :]
        cx = lambda j: jnp.broadcast_to(mod[l, B, j][None, None, :], (B, 1, D))
        gain_mix = norm_mix[l][None, :]

        proj = in_proj(x, gain_mix, lat(0), lat(1), w_in_b, l, tm=512)
        projc = in_proj(xc.reshape(1, B * C, D), gain_mix, cx(0)[:1], cx(1)[:1], w_in_b, l,
                        tm=B * C).reshape(B, C, IN_COLS)
        gain_c = jnp.tile(qk_norm_c[l], (1, HEAD_DIM // C_SUB_DIM))
        qk = qk_prep(proj, cosa, sina, cosc, sinc, qk_norm_a[l], qk_norm_b[l], gain_c)
        qkc = qk_prep(projc, one_tab, zero_tab, one_tab, zero_tab, qk_norm_a[l], qk_norm_b[l], gain_c)

        lam_init = 0.8 - 0.6 * math.exp(-0.3 * l)
        lam = _diff_lambda(lambda_c[l], lam_init).reshape(1)
        out_a = attn_a(qk, qkc, sink_a[l])
        out_b = attn_b(qk, qkc, _na_bias_tables(rpb_b[l], L // GRID_W))
        score_bound = (C_SUB_DIM ** 0.5) * jnp.max(jnp.abs(qk_norm_c[l, 0])) * jnp.max(jnp.abs(qk_norm_c[l, 1]))
        attn_c_l = functools.partial(attn_c, qk, qkc, lam, subln_c[l], 1.0 - lam_init)
        out_c = lax.cond(score_bound <= ATTN_C_SAFE_SCORE_BOUND,
                         lambda: attn_c_l(online=False), lambda: attn_c_l(online=True))
        out_d, out_dc = retention(proj, projc, ret_log_decay[l], with_ctx)

        x = out_proj([out_a, out_b, out_c, out_d], x, lat(2), w_out_b, l, tm=1024)
        if with_ctx:
            out_abc_c = ctx_attn(qkc, sink_a[l], lam, subln_c[l], 1.0 - lam_init)
            flat = lambda t: t.reshape(1, B * C, t.shape[-1])
            xc = out_proj([flat(out_abc_c), flat(out_dc)], flat(xc), cx(2)[:1], w_out_b, l,
                          tm=B * C).reshape(B, C, D)

        r_cat = jnp.zeros((D, 128), F32).at[:, :N_GROUPS].set(router_group[l])
        r_cat = r_cat.at[:, N_GROUPS:N_GROUPS + N_EXPERTS].set(router_expert[l])
        if with_ctx:
            shift = jnp.concatenate([lat(3), cx(3)[:1]], axis=0)
            scale = jnp.concatenate([lat(4), cx(4)[:1]], axis=0)
        else:
            shift, scale = lat(3), lat(4)
        h2, rid, rw = ffn_prep(x, xc if with_ctx else None, norm_ffn[l][None, :], shift, scale, r_cat)
        dest, row_tok, block_e, n_used = _route(rid[:, :TOP_K], MOE_TM)
        ys = experts(block_e, n_used, row_tok, h2, w_gate_b, w_up_b, w_down_b, l, MOE_TM)
        x = combine(dest, x, lat(5), rw, ys, 0)
        if with_ctx:
            xc = combine(dest, xc, cx(5), rw, ys, B * L)
    return x
```

```python
import functools
import math

import jax
import jax.numpy as jnp
import numpy as np
from jax import lax
from jax.experimental import pallas as pl
from jax.experimental.pallas import tpu as pltpu

F32 = jnp.float32
BF16 = jnp.bfloat16

HEAD_DIM = 128
GRID_W = 64
A_HEADS, A_KV_HEADS, A_WINDOW = 8, 2, 128
B_HEADS, NA_ROWS, NA_COLS = 8, 8, 16
C_HEADS, C_SUB_DIM = 8, 64
D_HEADS = 8
RET_CHUNK = 256
ROPE_BASE = 10000.0
N_GROUPS, EXPERTS_PER_GROUP, TOP_K = 4, 8, 2
N_EXPERTS = N_GROUPS * EXPERTS_PER_GROUP
EPS = 1e-6
NEG_INF = -1e30

A_Q, A_K, A_V = 0, 8, 10
B_Q, B_K, B_V = 12, 20, 28
C_Q, C_K, C_V = 36, 44, 52
D_Q, D_K, D_V, D_G = 60, 68, 76, 84
ABC_BLOCKS = 60
IN_COLS = 92 * 128

VMEM_LIMIT = 56 * 1024 * 1024


def _cparams(sem, vmem=VMEM_LIMIT):
    return pltpu.CompilerParams(dimension_semantics=sem, vmem_limit_bytes=vmem)


def _dot(a, b):
    return jnp.dot(a, b, preferred_element_type=F32)


def _dot_nt(a, b):
    return lax.dot_general(a, b, (((1,), (1,)), ((), ())), preferred_element_type=F32)


def _split_bf16(x):
    hi = x.astype(BF16)
    lo = (x - hi.astype(F32)).astype(BF16)
    return hi, lo


def _pack_bf16_pairs(x):
    half = x.shape[1] // 2
    bits = lax.bitcast_convert_type(x.astype(BF16).astype(F32), jnp.uint32)
    return (bits[:, :half] >> 16) | (bits[:, half:] & jnp.uint32(0xFFFF0000))


def _unpack_bf16_pairs(p):
    lo = lax.bitcast_convert_type(p << 16, F32)
    hi = lax.bitcast_convert_type(p & jnp.uint32(0xFFFF0000), F32)
    return jnp.concatenate([lo, hi], axis=-1)


def _dot3(x, w):
    xh, xl = _split_bf16(x)
    wh, wl = _split_bf16(w)
    return _dot(xh, wh) + _dot(xl, wh) + _dot(xh, wl)


def _mod_kernel(c_ref, w_ref, b_ref, o_ref):
    c = c_ref[...]
    x = c * jax.nn.sigmoid(c)
    o_ref[0] = _dot3(x, w_ref[0]) + b_ref[0]


def mod_vectors(cvec, w_mod, b_mod, tn=512):
    depth, D, N = w_mod.shape
    return pl.pallas_call(
        _mod_kernel,
        out_shape=jax.ShapeDtypeStruct((depth, 8, N), F32),
        grid=(depth, N // tn),
        in_specs=[
            pl.BlockSpec((8, D), lambda l, j: (0, 0)),
            pl.BlockSpec((1, D, tn), lambda l, j: (l, 0, j)),
            pl.BlockSpec((1, 1, tn), lambda l, j: (l, 0, j)),
        ],
        out_specs=pl.BlockSpec((1, 8, tn), lambda l, j: (l, 0, j)),
        compiler_params=_cparams(("arbitrary", "arbitrary")),
        name="mod_vectors",
    )(cvec, w_mod, b_mod.reshape(depth, 1, N))


def _modulated(x, gain, shift, scale):
    ms = jnp.mean(x * x, axis=-1, keepdims=True)
    y = x * lax.rsqrt(ms + EPS) * gain
    return y * (1.0 + scale) + shift


def _in_proj_kernel(x_ref, gain_ref, shift_ref, scale_ref, w_ref, o_ref, h_ref):
    @pl.when(pl.program_id(2) == 0)
    def _():
        h_ref[...] = _modulated(x_ref[0], gain_ref[...], shift_ref[0], scale_ref[0]).astype(BF16)

    o_ref[0] = _dot(h_ref[...], w_ref[0]).astype(o_ref.dtype)


def in_proj(x, gain, shift, scale, w, layer, tm, tn=512):
    B, L, D = x.shape
    N = w.shape[2]
    return pl.pallas_call(
        _in_proj_kernel,
        out_shape=jax.ShapeDtypeStruct((B, L, N), BF16),
        grid=(B, L // tm, N // tn),
        in_specs=[
            pl.BlockSpec((1, tm, D), lambda b, i, j: (b, i, 0)),
            pl.BlockSpec((1, D), lambda b, i, j: (0, 0)),
            pl.BlockSpec((1, 1, D), lambda b, i, j: (b, 0, 0)),
            pl.BlockSpec((1, 1, D), lambda b, i, j: (b, 0, 0)),
            pl.BlockSpec((1, D, tn), lambda b, i, j: (layer, 0, j)),
        ],
        out_specs=pl.BlockSpec((1, tm, tn), lambda b, i, j: (b, i, j)),
        scratch_shapes=[pltpu.VMEM((tm, D), BF16)],
        compiler_params=_cparams(("arbitrary", "arbitrary", "arbitrary")),
        name="in_proj",
    )(x, gain, shift, scale, w)


def _rope_tables(L, dim):
    t = jnp.arange(L)
    row = (t // GRID_W).astype(F32)
    col = (t % GRID_W).astype(F32)
    quarter = dim // 4
    inv = ROPE_BASE ** (-jnp.arange(quarter, dtype=F32) / quarter)
    ar = row[:, None] * inv[None, :]
    ac = col[:, None] * inv[None, :]
    ang = jnp.concatenate([ar, ar, ac, ac], axis=-1)
    sign = jnp.where((jnp.arange(dim) % (dim // 2)) < quarter, -1.0, 1.0).astype(F32)
    reps = HEAD_DIM // dim
    return jnp.tile(jnp.cos(ang), (1, reps)), jnp.tile(jnp.sin(ang) * sign[None, :], (1, reps))


def _qk_prep_kernel(p_ref, cosa_ref, sina_ref, cosc_ref, sinc_ref, ga_ref, gb_ref, gc_ref, o_ref):
    t = p_ref.shape[1]
    lane = lax.broadcasted_iota(jnp.int32, (t, HEAD_DIM), 1)

    ri = lax.broadcasted_iota(jnp.int32, (HEAD_DIM, HEAD_DIM), 0)
    ci = lax.broadcasted_iota(jnp.int32, (HEAD_DIM, HEAD_DIM), 1)
    ones_full = jnp.ones((HEAD_DIM, HEAD_DIM), BF16)
    ones_halves = jnp.where((ri < C_SUB_DIM) == (ci < C_SUB_DIM), 1.0, 0.0).astype(BF16)

    def head_sums(x, ones):
        hi, lo = _split_bf16(x * x)
        return _dot(hi, ones) + _dot(lo, ones)

    def norm_full(x):
        return x * lax.rsqrt(head_sums(x, ones_full) * (1.0 / HEAD_DIM) + EPS)

    def norm_halves(x):
        return x * lax.rsqrt(head_sums(x, ones_halves) * (1.0 / C_SUB_DIM) + EPS)

    def rope(x, cos, sin, dim):
        quarter = dim // 4
        first = (lane % (dim // 2)) < quarter
        rot = jnp.where(first, pltpu.roll(x, HEAD_DIM - quarter, 1), pltpu.roll(x, quarter, 1))
        return x * cos + rot * sin

    cosa, sina = cosa_ref[...], sina_ref[...]
    cosc, sinc = cosc_ref[...], sinc_ref[...]
    scale_ab = HEAD_DIM ** -0.5
    scale_c = C_SUB_DIM ** -0.5 * math.log2(math.e)
    for blk in range(ABC_BLOCKS):
        sl = slice(blk * HEAD_DIM, (blk + 1) * HEAD_DIM)
        if A_V <= blk < B_Q or B_V <= blk < C_Q or C_V <= blk:
            o_ref[0, :, sl] = p_ref[0, :, sl]
            continue
        x = p_ref[0, :, sl].astype(F32)
        if blk < A_K:
            y = rope(norm_full(x) * ga_ref[0:1, :], cosa, sina, HEAD_DIM) * scale_ab
        elif blk < A_V:
            y = rope(norm_full(x) * ga_ref[1:2, :], cosa, sina, HEAD_DIM)
        elif blk < B_K:
            y = norm_full(x) * gb_ref[0:1, :] * scale_ab
        elif blk < B_V:
            y = norm_full(x) * gb_ref[1:2, :]
        elif blk < C_K:
            y = rope(norm_halves(x) * gc_ref[0:1, :], cosc, sinc, C_SUB_DIM) * scale_c
        else:
            y = rope(norm_halves(x) * gc_ref[1:2, :], cosc, sinc, C_SUB_DIM)
        o_ref[0, :, sl] = y.astype(o_ref.dtype)


def qk_prep(proj, cosa, sina, cosc, sinc, ga, gb, gc, t=256):
    B, L, _ = proj.shape
    W = ABC_BLOCKS * HEAD_DIM
    tab = pl.BlockSpec((t, HEAD_DIM), lambda b, i: (i, 0))
    par = pl.BlockSpec((2, HEAD_DIM), lambda b, i: (0, 0))
    return pl.pallas_call(
        _qk_prep_kernel,
        out_shape=jax.ShapeDtypeStruct((B, L, W), BF16),
        grid=(B, L // t),
        in_specs=[pl.BlockSpec((1, t, W), lambda b, i: (b, i, 0)), tab, tab, tab, tab, par, par, par],
        out_specs=pl.BlockSpec((1, t, W), lambda b, i: (b, i, 0)),
        compiler_params=_cparams(("arbitrary", "arbitrary")),
        name="qk_prep",
    )(proj, cosa, sina, cosc, sinc, ga, gb, gc)


def _attn_a_kernel(sink_ref, q_ref, k_ref, v_ref, kc_ref, vc_ref, o_ref, *, tq):
    L = k_ref.shape[1]
    kv, i = pl.program_id(1), pl.program_id(2)
    win = tq + 2 * A_WINDOW
    start = pl.multiple_of(jnp.clip(i * tq - A_WINDOW, 0, L - win), A_WINDOW)
    k = k_ref[0, pl.ds(start, win), :]
    v = v_ref[0, pl.ds(start, win), :]
    kc, vc = kc_ref[0], vc_ref[0]
    qpos = i * tq + lax.broadcasted_iota(jnp.int32, (tq, win), 0)
    kpos = start + lax.broadcasted_iota(jnp.int32, (tq, win), 1)
    in_window = jnp.abs(kpos - qpos) <= A_WINDOW
    group = A_HEADS // A_KV_HEADS
    for g in range(group):
        sl = slice(g * HEAD_DIM, (g + 1) * HEAD_DIM)
        q = q_ref[0, :, sl]
        s = jnp.where(in_window, _dot_nt(q, k), NEG_INF)
        sc = _dot_nt(q, kc)
        sink = sink_ref[kv * group + g]
        m = jnp.maximum(jnp.maximum(jnp.max(s, axis=-1, keepdims=True), jnp.max(sc, axis=-1, keepdims=True)), sink)
        p = jnp.exp(s - m)
        pc = jnp.exp(sc - m)
        denom = jnp.sum(p, axis=-1, keepdims=True) + jnp.sum(pc, axis=-1, keepdims=True) + jnp.exp(sink - m)
        o = _dot(p.astype(BF16), v) + _dot(pc.astype(BF16), vc)
        o_ref[0, :, sl] = (o / denom).astype(o_ref.dtype)


def attn_a(qk, qkc, sink, tq=512):
    B, L, _ = qk.shape
    C = qkc.shape[1]
    gw = (A_HEADS // A_KV_HEADS) * HEAD_DIM
    return pl.pallas_call(
        functools.partial(_attn_a_kernel, tq=tq),
        out_shape=jax.ShapeDtypeStruct((B, L, A_HEADS * HEAD_DIM), BF16),
        grid=(B, A_KV_HEADS, L // tq),
        in_specs=[
            pl.BlockSpec(memory_space=pltpu.SMEM),
            pl.BlockSpec((1, tq, gw), lambda b, kv, i: (b, i, kv)),
            pl.BlockSpec((1, L, HEAD_DIM), lambda b, kv, i: (b, 0, A_K + kv)),
            pl.BlockSpec((1, L, HEAD_DIM), lambda b, kv, i: (b, 0, A_V + kv)),
            pl.BlockSpec((1, C, HEAD_DIM), lambda b, kv, i: (b, 0, A_K + kv)),
            pl.BlockSpec((1, C, HEAD_DIM), lambda b, kv, i: (b, 0, A_V + kv)),
        ],
        out_specs=pl.BlockSpec((1, tq, gw), lambda b, kv, i: (b, i, kv)),
        compiler_params=_cparams(("arbitrary", "arbitrary", "arbitrary")),
        name="attn_a",
    )(sink, qk, qk, qk, qkc, qkc)


NA_QROWS = 8
NA_SPLIT = 2


def _na_bias_tables(rpb, R):
    W = GRID_W
    H = rpb.shape[0]
    wq = jnp.arange(W)
    cs = jnp.clip(wq - NA_COLS // 2, 0, W - NA_COLS)
    colmask = (wq[None, :] >= cs[:, None]) & (wq[None, :] < cs[:, None] + NA_COLS)
    rel_c = jnp.clip(wq[None, :] - wq[:, None] + (NA_COLS - 1), 0, 2 * NA_COLS - 2)
    per_row = jnp.where(colmask[None, None], rpb.astype(F32)[:, :, rel_c], NEG_INF)
    per_row = jnp.transpose(per_row, (0, 2, 1, 3))
    masked = jnp.full((H, W, 1, W), NEG_INF, F32)
    nq, nk = NA_QROWS, 2 * NA_QROWS
    tables = []
    for r0, ks in ((0, 0), (nq, nq - NA_ROWS // 2), (R - nq, R - nk)):
        rows = []
        for a in range(nq):
            qr = r0 + a
            rs = min(max(qr - NA_ROWS // 2, 0), R - NA_ROWS)
            blocks = []
            for c in range(nk):
                kr = ks + c
                if rs <= kr < rs + NA_ROWS:
                    d = kr - qr + (NA_ROWS - 1)
                    blocks.append(per_row[:, :, d:d + 1, :])
                else:
                    blocks.append(masked)
            rows.append(jnp.concatenate(blocks, axis=2))
        tables.append(jnp.stack(rows, axis=1).reshape(H, nq * W, nk * W))
    return jnp.stack(tables)


def _attn_b_kernel(q_ref, k_ref, v_ref, kc_ref, vc_ref, bias_ref, o_ref):
    L = k_ref.shape[1]
    R = L // GRID_W
    i = pl.program_id(2)
    nk = 2 * NA_QROWS
    krow = jnp.clip(i * NA_QROWS - NA_ROWS // 2, 0, R - nk)
    start = pl.multiple_of(krow * GRID_W, (NA_ROWS // 2) * GRID_W)
    k = k_ref[0, pl.ds(start, nk * GRID_W), :]
    v = v_ref[0, pl.ds(start, nk * GRID_W), :]
    kc, vc = kc_ref[0], vc_ref[0]
    sub = q_ref.shape[1] // NA_SPLIT
    for part in range(NA_SPLIT):
        rows = slice(part * sub, (part + 1) * sub)
        q = q_ref[0, rows, :]
        s = _dot_nt(q, k) + bias_ref[0, 0, rows, :]
        sc = _dot_nt(q, kc)
        m = jnp.maximum(jnp.max(s, axis=-1, keepdims=True), jnp.max(sc, axis=-1, keepdims=True))
        p = jnp.exp(s - m)
        pc = jnp.exp(sc - m)
        denom = jnp.sum(p, axis=-1, keepdims=True) + jnp.sum(pc, axis=-1, keepdims=True)
        o = _dot(p.astype(BF16), v) + _dot(pc.astype(BF16), vc)
        o_ref[0, rows, :] = (o / denom).astype(o_ref.dtype)


def attn_b(qk, qkc, bias):
    B, L, _ = qk.shape
    C = qkc.shape[1]
    tq = NA_QROWS * GRID_W
    nblk = L // tq

    def bias_idx(h, b, i):
        return (jnp.where(i == 0, 0, jnp.where(i == nblk - 1, 2, 1)), h, 0, 0)

    return pl.pallas_call(
        _attn_b_kernel,
        out_shape=jax.ShapeDtypeStruct((B, L, B_HEADS * HEAD_DIM), BF16),
        grid=(B_HEADS, B, nblk),
        in_specs=[
            pl.BlockSpec((1, tq, HEAD_DIM), lambda h, b, i: (b, i, B_Q + h)),
            pl.BlockSpec((1, L, HEAD_DIM), lambda h, b, i: (b, 0, B_K + h)),
            pl.BlockSpec((1, L, HEAD_DIM), lambda h, b, i: (b, 0, B_V + h)),
            pl.BlockSpec((1, C, HEAD_DIM), lambda h, b, i: (b, 0, B_K + h)),
            pl.BlockSpec((1, C, HEAD_DIM), lambda h, b, i: (b, 0, B_V + h)),
            pl.BlockSpec((1, 1, tq, 2 * tq), bias_idx),
        ],
        out_specs=pl.BlockSpec((1, tq, HEAD_DIM), lambda h, b, i: (b, i, h)),
        compiler_params=_cparams(("arbitrary", "arbitrary", "arbitrary")),
        name="attn_b",
    )(qk, qk, qk, qkc, qkc, bias)


def _diff_lambda(lambda_params, lam_init):
    lp = lambda_params.astype(F32)
    return jnp.exp(jnp.sum(lp[0] * lp[1])) - jnp.exp(jnp.sum(lp[2] * lp[3])) + lam_init


ATTN_C_SAFE_SCORE_BOUND = 40.0


def _attn_c_kernel(lam_ref, q_ref, k_ref, v_ref, kc_ref, vc_ref, g_ref, o_ref, vt_ref, vct_ref, *scratch,
                   tk, coef, online):
    if online:
        m_ref, l_ref, acc_ref = scratch
        m_ref[...] = jnp.full(m_ref.shape, NEG_INF, F32)
    else:
        l_ref, acc_ref = scratch
    nk = vt_ref.shape[0]

    @pl.when(pl.program_id(2) == 0)
    def _():
        for j in range(nk):
            vt_ref[j] = v_ref[0, j * tk:(j + 1) * tk, :].astype(F32).T.astype(BF16)
        vct_ref[...] = vc_ref[0].astype(F32).T.astype(BF16)

    q = q_ref[0]
    lane = lax.broadcasted_iota(jnp.int32, q.shape, 1)
    zero = jnp.zeros_like(q)
    qs = (jnp.where(lane < C_SUB_DIM, q, zero), jnp.where(lane < C_SUB_DIM, zero, q))

    l_ref[...] = jnp.zeros(l_ref.shape, F32)
    acc_ref[...] = jnp.zeros(acc_ref.shape, F32)

    def accumulate(kblk, vtblk):
        for s_idx in range(2):
            s = _dot_nt(kblk, qs[s_idx])
            if online:
                m_old = m_ref[s_idx]
                m_new = jnp.maximum(m_old, jnp.max(s, axis=0, keepdims=True))
                alpha = jnp.exp2(m_old - m_new)
                p = jnp.exp2(s - m_new)
                l_ref[s_idx] = alpha * l_ref[s_idx] + jnp.sum(p, axis=0, keepdims=True)
                acc_ref[s_idx] = alpha * acc_ref[s_idx] + _dot(vtblk, p.astype(BF16))
                m_ref[s_idx] = m_new
            else:
                p = jnp.exp2(s)
                l_ref[s_idx] += jnp.sum(p, axis=0, keepdims=True)
                acc_ref[s_idx] += _dot(vtblk, p.astype(BF16))

    def body(j, carry):
        accumulate(k_ref[0, pl.ds(pl.multiple_of(j * tk, tk), tk), :], vt_ref[j])
        return carry

    lax.fori_loop(0, nk, body, 0, unroll=4)
    accumulate(kc_ref[0], vct_ref[...])

    o = acc_ref[0] / l_ref[0] - lam_ref[0] * (acc_ref[1] / l_ref[1])
    y = o * lax.rsqrt(jnp.mean(o * o, axis=0, keepdims=True) + EPS) * g_ref[...] * coef
    o_ref[0] = y.T.astype(o_ref.dtype)


def attn_c(qk, qkc, lam, subln, coef, online, tq=2048, tk=512):
    B, L, _ = qk.shape
    C = qkc.shape[1]
    H, dv = C_HEADS, HEAD_DIM
    stats = [pltpu.VMEM((2, 1, tq), F32)] * (2 if online else 1)
    return pl.pallas_call(
        functools.partial(_attn_c_kernel, tk=tk, coef=coef, online=online),
        out_shape=jax.ShapeDtypeStruct((B, L, H * dv), BF16),
        grid=(B, H, L // tq),
        in_specs=[
            pl.BlockSpec(memory_space=pltpu.SMEM),
            pl.BlockSpec((1, tq, HEAD_DIM), lambda b, h, i: (b, i, C_Q + h)),
            pl.BlockSpec((1, L, HEAD_DIM), lambda b, h, i: (b, 0, C_K + h)),
            pl.BlockSpec((1, L, HEAD_DIM), lambda b, h, i: (b, 0, C_V + h)),
            pl.BlockSpec((1, C, HEAD_DIM), lambda b, h, i: (b, 0, C_K + h)),
            pl.BlockSpec((1, C, HEAD_DIM), lambda b, h, i: (b, 0, C_V + h)),
            pl.BlockSpec((dv, 1), lambda b, h, i: (0, 0)),
        ],
        out_specs=pl.BlockSpec((1, tq, dv), lambda b, h, i: (b, i, h)),
        scratch_shapes=[pltpu.VMEM((L // tk, dv, tk), BF16), pltpu.VMEM((dv, C), BF16)]
        + stats + [pltpu.VMEM((2, dv, tq), F32)],
        compiler_params=_cparams(("arbitrary", "arbitrary", "arbitrary")),
        name="attn_c_online" if online else "attn_c",
    )(lam, qk, qk, qk, qkc, qkc, subln.reshape(dv, 1).astype(F32))


def _retention_kernel(ld_ref, q_ref, k_ref, v_ref, g_ref, qc_ref, kc_ref, vc_ref, gc_ref, *rest, with_ctx):
    if with_ctx:
        y_ref, yc_ref, of_ref, ob_ref, ocf_ref = rest
    else:
        y_ref, of_ref, ob_ref = rest
        yc_ref = ocf_ref = None
    c = RET_CHUNK
    L, C = q_ref.shape[1], qc_ref.shape[1]
    h = pl.program_id(1)
    scale = HEAD_DIM ** -0.5
    row = lax.broadcasted_iota(jnp.int32, (c, c), 0).astype(F32)
    col = lax.broadcasted_iota(jnp.int32, (c, c), 1).astype(F32)
    rowp = lax.broadcasted_iota(jnp.int32, (c, HEAD_DIM), 0).astype(F32)

    def decays(lg, reverse):
        if reverse:
            rel, q_exp, k_exp = col - row, c - rowp, rowp
        else:
            rel, q_exp, k_exp = row - col, rowp + 1.0, (c - 1.0) - rowp
        intra = jnp.where(rel >= 0, jnp.exp(lg * jnp.maximum(rel, 0.0)), 0.0)
        return intra, jnp.exp(lg * q_exp), jnp.exp(lg * k_exp), jnp.exp(lg * c)

    def step(S, q, k, v, dec, want_out):
        intra, q_decay, k_decay, chunk_decay = dec
        o = None
        if want_out:
            qs = (q.astype(F32) * scale).astype(BF16)
            s = _dot_nt(qs, k) * intra
            o = _dot(s.astype(BF16), v) + _dot(qs, S.astype(BF16)) * q_decay
        kd_t = (k.astype(F32) * k_decay).T.astype(BF16)
        return S * chunk_decay + _dot(kd_t, v), o

    def gated(o, g):
        g = g.astype(F32)
        y = o * lax.rsqrt(jnp.mean(o * o, axis=-1, keepdims=True) + EPS)
        return y * (g * jax.nn.sigmoid(g))

    def lat(ref, j):
        return ref[0, pl.ds(pl.multiple_of(j * c, c), c), :]

    def ctx(ref, j):
        return ref[0, j * c:(j + 1) * c, :]

    dec_f = decays(ld_ref[0, h], False)
    dec_b = decays(ld_ref[1, h], True)
    S_f = jnp.zeros((HEAD_DIM, HEAD_DIM), F32)
    for j in range(C // c):
        S_f, o = step(S_f, ctx(qc_ref, j), ctx(kc_ref, j), ctx(vc_ref, j), dec_f, with_ctx)
        if with_ctx:
            ocf_ref[j * c:(j + 1) * c, :] = o
    S_b = jnp.zeros((HEAD_DIM, HEAD_DIM), F32)
    for j in reversed(range(C // c)):
        S_b, o = step(S_b, ctx(qc_ref, j), ctx(kc_ref, j), ctx(vc_ref, j), dec_b, with_ctx)
        if with_ctx:
            yc_ref[0, j * c:(j + 1) * c, :] = gated(o + ocf_ref[j * c:(j + 1) * c, :], ctx(gc_ref, j)).astype(yc_ref.dtype)

    n = L // c

    def both(t, carry):
        S_f, S_b = carry
        S_f, o_f = step(S_f, lat(q_ref, t), lat(k_ref, t), lat(v_ref, t), dec_f, True)
        of_ref[pl.ds(pl.multiple_of(t * c, c), c), :] = o_f
        j = n - 1 - t
        S_b, o_b = step(S_b, lat(q_ref, j), lat(k_ref, j), lat(v_ref, j), dec_b, True)
        ob_ref[pl.ds(pl.multiple_of(j * c, c), c), :] = o_b
        return S_f, S_b

    lax.fori_loop(0, n, both, (S_f, S_b), unroll=2)

    def finish(j, carry):
        sl = pl.ds(pl.multiple_of(j * c, c), c)
        y_ref[0, sl, :] = gated(of_ref[sl, :] + ob_ref[sl, :], lat(g_ref, j)).astype(y_ref.dtype)
        return carry

    lax.fori_loop(0, n, finish, 0, unroll=2)


def retention(proj, projc, log_decay, with_ctx):
    B, L, _ = proj.shape
    C = projc.shape[1]

    def lat(blk):
        return pl.BlockSpec((1, L, HEAD_DIM), lambda b, h: (b, 0, blk + h))

    def ctx(blk):
        return pl.BlockSpec((1, C, HEAD_DIM), lambda b, h: (b, 0, blk + h))

    out_shape = [jax.ShapeDtypeStruct((B, L, D_HEADS * HEAD_DIM), BF16)]
    out_specs = [pl.BlockSpec((1, L, HEAD_DIM), lambda b, h: (b, 0, h))]
    scratch = [pltpu.VMEM((L, HEAD_DIM), F32), pltpu.VMEM((L, HEAD_DIM), F32)]
    if with_ctx:
        out_shape.append(jax.ShapeDtypeStruct((B, C, D_HEADS * HEAD_DIM), BF16))
        out_specs.append(pl.BlockSpec((1, C, HEAD_DIM), lambda b, h: (b, 0, h)))
        scratch.append(pltpu.VMEM((C, HEAD_DIM), F32))
    outs = pl.pallas_call(
        functools.partial(_retention_kernel, with_ctx=with_ctx),
        out_shape=out_shape,
        grid=(B, D_HEADS),
        in_specs=[pl.BlockSpec(memory_space=pltpu.SMEM),
                  lat(D_Q), lat(D_K), lat(D_V), lat(D_G), ctx(D_Q), ctx(D_K), ctx(D_V), ctx(D_G)],
        out_specs=out_specs,
        scratch_shapes=scratch,
        compiler_params=_cparams(("arbitrary", "arbitrary")),
        name="retention_ctx" if with_ctx else "retention",
    )(log_decay.astype(F32), proj, proj, proj, proj, projc, projc, projc, projc)
    return (outs[0], outs[1]) if with_ctx else (outs[0], None)


def _ctx_attn_kernel(sink_ref, lam_ref, qk_ref, g_ref, o_ref, *, coef):
    def blk(idx):
        return qk_ref[0, :, idx * HEAD_DIM:(idx + 1) * HEAD_DIM]

    def softmax_out(q, k, v, sink=None, exp=jnp.exp):
        s = _dot_nt(q, k)
        m = jnp.max(s, axis=-1, keepdims=True)
        if sink is not None:
            m = jnp.maximum(m, sink)
        p = exp(s - m)
        denom = jnp.sum(p, axis=-1, keepdims=True)
        if sink is not None:
            denom = denom + jnp.exp(sink - m)
        return _dot(p.astype(BF16), v) / denom

    group = A_HEADS // A_KV_HEADS
    for h in range(A_HEADS):
        o = softmax_out(blk(A_Q + h), blk(A_K + h // group), blk(A_V + h // group), sink_ref[h])
        o_ref[0, :, h * HEAD_DIM:(h + 1) * HEAD_DIM] = o.astype(o_ref.dtype)
    for h in range(B_HEADS):
        o = softmax_out(blk(B_Q + h), blk(B_K + h), blk(B_V + h))
        o_ref[0, :, (A_HEADS + h) * HEAD_DIM:(A_HEADS + h + 1) * HEAD_DIM] = o.astype(o_ref.dtype)
    lane = lax.broadcasted_iota(jnp.int32, (qk_ref.shape[1], HEAD_DIM), 1)
    for h in range(C_HEADS):
        q, k, v = blk(C_Q + h), blk(C_K + h), blk(C_V + h)
        zero = jnp.zeros_like(q)
        o1 = softmax_out(jnp.where(lane < C_SUB_DIM, q, zero), k, v, exp=jnp.exp2)
        o2 = softmax_out(jnp.where(lane < C_SUB_DIM, zero, q), k, v, exp=jnp.exp2)
        o = o1 - lam_ref[0] * o2
        y = o * lax.rsqrt(jnp.mean(o * o, axis=-1, keepdims=True) + EPS) * g_ref[...] * coef
        base = A_HEADS + B_HEADS + h
        o_ref[0, :, base * HEAD_DIM:(base + 1) * HEAD_DIM] = y.astype(o_ref.dtype)


def ctx_attn(qkc, sink, lam, subln, coef):
    B, C, W = qkc.shape
    n_out = (A_HEADS + B_HEADS + C_HEADS) * HEAD_DIM
    return pl.pallas_call(
        functools.partial(_ctx_attn_kernel, coef=coef),
        out_shape=jax.ShapeDtypeStruct((B, C, n_out), BF16),
        grid=(B,),
        in_specs=[
            pl.BlockSpec(memory_space=pltpu.SMEM),
            pl.BlockSpec(memory_space=pltpu.SMEM),
            pl.BlockSpec((1, C, W), lambda b: (b, 0, 0)),
            pl.BlockSpec((1, HEAD_DIM), lambda b: (0, 0)),
        ],
        out_specs=pl.BlockSpec((1, C, n_out), lambda b: (b, 0, 0)),
        compiler_params=_cparams(("arbitrary",)),
        name="ctx_attn",
    )(sink, lam, qkc, subln.reshape(1, HEAD_DIM).astype(F32))


def _out_proj_kernel(*refs, n_parts):
    parts = refs[:n_parts]
    x_ref, gate_ref, w_ref, o_ref = refs[n_parts:]
    mix = jnp.concatenate([p[0] for p in parts], axis=-1)
    o_ref[0] = x_ref[0] + gate_ref[0] * _dot(mix, w_ref[0])


def out_proj(parts, x, gate, w, layer, tm, tn=512):
    B, L, D = x.shape
    part_specs = [pl.BlockSpec((1, tm, p.shape[2]), lambda b, i, j: (b, i, 0)) for p in parts]
    return pl.pallas_call(
        functools.partial(_out_proj_kernel, n_parts=len(parts)),
        out_shape=jax.ShapeDtypeStruct((B, L, D), F32),
        grid=(B, L // tm, D // tn),
        in_specs=part_specs + [
            pl.BlockSpec((1, tm, tn), lambda b, i, j: (b, i, j)),
            pl.BlockSpec((1, 1, tn), lambda b, i, j: (b, 0, j)),
            pl.BlockSpec((1, w.shape[1], tn), lambda b, i, j: (layer, 0, j)),
        ],
        out_specs=pl.BlockSpec((1, tm, tn), lambda b, i, j: (b, i, j)),
        compiler_params=_cparams(("arbitrary", "arbitrary", "arbitrary")),
        name="out_proj",
    )(*parts, x, gate, w)


def _ffn_prep_kernel(*refs, n_lat_tiles):
    if n_lat_tiles is None:
        x_ref, gain_ref, shift_ref, scale_ref, r_ref, h_ref, id_ref, w_ref = refs
        x = x_ref[...]
    else:
        xl_ref, xc_ref, gain_ref, shift_ref, scale_ref, r_ref, h_ref, id_ref, w_ref = refs
        x = jnp.where(pl.program_id(0) < n_lat_tiles, xl_ref[...], xc_ref[...])
    h = _modulated(x, gain_ref[...], shift_ref[0], scale_ref[0])
    h_ref[...] = _pack_bf16_pairs(h)

    logits = _dot3(h, r_ref[...])
    lane = lax.broadcasted_iota(jnp.int32, logits.shape, 1)
    big = jnp.int32(logits.shape[1])

    def first_argmax(vals):
        top = jnp.max(vals, axis=-1, keepdims=True)
        return top, jnp.min(jnp.where(vals == top, lane, big), axis=-1, keepdims=True)

    is_group = lane < N_GROUPS
    g_top, g_idx = first_argmax(jnp.where(is_group, logits, NEG_INF))
    g_val = 1.0 / jnp.sum(jnp.where(is_group, jnp.exp(logits - g_top), 0.0), axis=-1, keepdims=True)
    lo = N_GROUPS + EXPERTS_PER_GROUP * g_idx
    e_logits = jnp.where((lane >= lo) & (lane < lo + EXPERTS_PER_GROUP), logits, NEG_INF)
    v1, i1 = first_argmax(e_logits)
    v2, i2 = first_argmax(jnp.where(lane == i1, NEG_INF, e_logits))
    e21 = jnp.exp(v2 - v1)
    w1 = g_val / (1.0 + e21)
    w2 = g_val * e21 / (1.0 + e21)
    id_ref[...] = jnp.where(lane == 0, i1 - N_GROUPS, jnp.where(lane == 1, i2 - N_GROUPS, 0))
    w_ref[...] = jnp.where(lane == 0, w1, jnp.where(lane == 1, w2, 0.0))


def ffn_prep(x, xc, gain, shift, scale, r_cat, t=256):
    B, L, D = x.shape
    n_lat = B * L // t
    lat_per_seg = L // t
    if xc is None:
        T = B * L
        xs = [x.reshape(B * L, D)]
        x_specs = [pl.BlockSpec((t, D), lambda i: (i, 0))]
        seg = lambda i: (i // lat_per_seg, 0, 0)
        n_lat_tiles = None
    else:
        C = xc.shape[1]
        T = B * (L + C)
        xs = [x.reshape(B * L, D), xc.reshape(B * C, D)]
        x_specs = [pl.BlockSpec((t, D), lambda i: (jnp.minimum(i, n_lat - 1), 0)),
                   pl.BlockSpec((t, D), lambda i: (jnp.maximum(i - n_lat, 0), 0))]
        seg = lambda i: (jnp.minimum(i // lat_per_seg, B), 0, 0)
        n_lat_tiles = n_lat
    return pl.pallas_call(
        functools.partial(_ffn_prep_kernel, n_lat_tiles=n_lat_tiles),
        out_shape=[jax.ShapeDtypeStruct((T, D // 2), jnp.uint32),
                   jax.ShapeDtypeStruct((T, 128), jnp.int32),
                   jax.ShapeDtypeStruct((T, 128), F32)],
        grid=(T // t,),
        in_specs=x_specs + [
            pl.BlockSpec((1, D), lambda i: (0, 0)),
            pl.BlockSpec((1, 1, D), seg),
            pl.BlockSpec((1, 1, D), seg),
            pl.BlockSpec((D, 128), lambda i: (0, 0)),
        ],
        out_specs=[pl.BlockSpec((t, D // 2), lambda i: (i, 0)),
                   pl.BlockSpec((t, 128), lambda i: (i, 0)),
                   pl.BlockSpec((t, 128), lambda i: (i, 0))],
        compiler_params=_cparams(("arbitrary",)),
        name="ffn_prep",
    )(*xs, gain, shift, scale, r_cat)


def _route(ids, tm):
    A = ids.shape[0] * TOP_K
    flat_e = ids.reshape(A)
    onehot = (flat_e[:, None] == jnp.arange(N_EXPERTS, dtype=jnp.int32)[None, :]).astype(jnp.int32)
    csum = jnp.cumsum(onehot, axis=0)
    rank = jnp.sum(csum * onehot, axis=1) - 1
    counts = csum[-1]
    pcounts = (counts + tm - 1) // tm * tm
    pends = jnp.cumsum(pcounts)
    pstarts = pends - pcounts
    dest = jnp.sum(onehot * pstarts[None, :], axis=1) + rank
    NB = (A + N_EXPERTS * (tm - 1)) // tm
    row_tok = jnp.zeros((NB * tm,), jnp.int32).at[dest].set(jnp.arange(A, dtype=jnp.int32) // TOP_K)
    block_start = jnp.arange(NB, dtype=jnp.int32) * tm
    block_e = jnp.minimum(jnp.sum((pends[None, :] <= block_start[:, None]).astype(jnp.int32), axis=1), N_EXPERTS - 1)
    return dest.astype(jnp.int32), row_tok, block_e.astype(jnp.int32), (pends[-1:] // tm).astype(jnp.int32)


def _row_copy(src, src_row, dst, dst_row, sem):
    return pltpu.make_async_copy(src.at[pl.ds(src_row, 1), :], dst.at[pl.ds(dst_row, 1), :], sem)


def _expert_kernel(be_ref, nused_ref, tok_ref, h_ref, wg_ref, wu_ref, wd_ref, o_ref, xbuf, sem):
    i = pl.program_id(0)
    tm = o_ref.shape[0]
    n_used = nused_ref[0]

    def row(r, blk, slot):
        return _row_copy(h_ref, tok_ref[blk * tm + r], xbuf.at[slot], r, sem.at[slot])

    def gather(blk, slot):
        def issue(r2, carry):
            for k in range(2):
                row(2 * r2 + k, blk, slot).start(priority=k)
            return carry
        lax.fori_loop(0, tm // 2, issue, 0, unroll=4)

    def drain(slot):
        def wait(r, carry):
            row(r, 0, slot).wait()
            return carry
        lax.fori_loop(0, tm, wait, 0, unroll=8)

    @pl.when(i == 0)
    def _():
        gather(0, 0)

    @pl.when(i + 1 < n_used)
    def _():
        gather(i + 1, (i + 1) % 2)

    @pl.when(i < n_used)
    def _():
        slot = i % 2
        drain(slot)
        x = _unpack_bf16_pairs(xbuf[slot]).astype(BF16)
        a = _dot(x, wg_ref[0, 0])
        u = _dot(x, wu_ref[0, 0])
        hmid = (a * jax.nn.sigmoid(a) * u).astype(BF16)
        o_ref[...] = _pack_bf16_pairs(_dot(hmid, wd_ref[0, 0]))

    @pl.when(i >= n_used)
    def _():
        o_ref[...] = jnp.zeros(o_ref.shape, o_ref.dtype)


def experts(block_e, n_used, row_tok, h, w_gate, w_up, w_down, layer, tm):
    P = row_tok.shape[0]
    _, _, D, De = w_gate.shape
    w_idx = lambda i, be, nu, tok: (layer, be[jnp.minimum(i, nu[0] - 1)], 0, 0)
    return pl.pallas_call(
        _expert_kernel,
        out_shape=jax.ShapeDtypeStruct((P, D // 2), jnp.uint32),
        grid_spec=pltpu.PrefetchScalarGridSpec(
            num_scalar_prefetch=3,
            grid=(P // tm,),
            in_specs=[
                pl.BlockSpec(memory_space=pl.ANY),
                pl.BlockSpec((1, 1, D, De), w_idx),
                pl.BlockSpec((1, 1, D, De), w_idx),
                pl.BlockSpec((1, 1, De, D), w_idx),
            ],
            out_specs=pl.BlockSpec((tm, D // 2), lambda i, be, nu, tok: (i, 0)),
            scratch_shapes=[pltpu.VMEM((2, tm, D // 2), jnp.uint32), pltpu.SemaphoreType.DMA((2,))],
        ),
        compiler_params=_cparams(("arbitrary",)),
        name="moe_experts",
    )(block_e, n_used, row_tok, h, w_gate, w_up, w_down)


def _combine_kernel(dest_ref, x_ref, gate_ref, w_ref, y_ref, o_ref, buf, sem, *, tok_offset):
    t, D = x_ref.shape
    i = pl.program_id(0)
    slot = i % 2

    def gather(step, slot):
        base = (tok_offset + step * t) * TOP_K

        def issue(r, carry):
            for k in range(TOP_K):
                _row_copy(y_ref, dest_ref[base + r * TOP_K + k], buf.at[slot, k], r, sem.at[slot]).start(priority=k)
            return carry
        lax.fori_loop(0, t, issue, 0, unroll=4)

    def drain(r, carry):
        _row_copy(y_ref, 0, buf.at[slot, 0], 0, sem.at[slot]).wait()
        return carry

    @pl.when(i == 0)
    def _():
        gather(0, 0)

    @pl.when(i + 1 < pl.num_programs(0))
    def _():
        gather(i + 1, 1 - slot)

    lax.fori_loop(0, t * TOP_K, drain, 0, unroll=8)
    w = w_ref[...]
    y = w[:, 0:1] * _unpack_bf16_pairs(buf[slot, 0]) + w[:, 1:2] * _unpack_bf16_pairs(buf[slot, 1])
    o_ref[...] = x_ref[...] + gate_ref[0] * y


def combine(dest, x, gate, rw, ys, tok_offset, t=256):
    B, L, D = x.shape
    per_seg = L // t
    woff = tok_offset // t
    out = pl.pallas_call(
        functools.partial(_combine_kernel, tok_offset=tok_offset),
        out_shape=jax.ShapeDtypeStruct((B * L, D), F32),
        grid_spec=pltpu.PrefetchScalarGridSpec(
            num_scalar_prefetch=1,
            grid=(B * L // t,),
            in_specs=[
                pl.BlockSpec((t, D), lambda i, d: (i, 0)),
                pl.BlockSpec((1, 1, D), lambda i, d: (i // per_seg, 0, 0)),
                pl.BlockSpec((t, 128), lambda i, d: (woff + i, 0)),
                pl.BlockSpec(memory_space=pl.ANY),
            ],
            out_specs=pl.BlockSpec((t, D), lambda i, d: (i, 0)),
            scratch_shapes=[pltpu.VMEM((2, TOP_K, t, D // 2), jnp.uint32), pltpu.SemaphoreType.DMA((2,))],
        ),
        compiler_params=_cparams(("arbitrary",)),
        name="moe_combine",
    )(dest, x.reshape(B * L, D), gate, rw, ys)
    return out.reshape(B, L, D)


MOE_TM = 256


def kernel(x, c, ctx, c_ctx, w_mod, b_mod, norm_mix, norm_ffn, w_in, w_out, qk_norm_a, sink_a, qk_norm_b, rpb_b,
           qk_norm_c, lambda_c, subln_c, ret_log_decay, router_group, router_expert, w_gate, w_up, w_down):
    B, L, D = x.shape
    C = ctx.shape[1]
    depth = w_mod.shape[0]

    cvec = jnp.zeros((8, D), F32).at[:B].set(c).at[B].set(c_ctx)
    mod = mod_vectors(cvec, w_mod, b_mod).reshape(depth, 8, 6, D)

    cosa, sina = _rope_tables(L, HEAD_DIM)
    cosc, sinc = _rope_tables(L, C_SUB_DIM)
    one_tab = jnp.ones((C, HEAD_DIM), F32)
    zero_tab = jnp.zeros((C, HEAD_DIM), F32)

    w_in_b, w_out_b = w_in.astype(BF16), w_out.astype(BF16)
    w_gate_b, w_up_b, w_down_b = w_gate.astype(BF16), w_up.astype(BF16), w_down.astype(BF16)

    xc = ctx
    for l in range(depth):
        with_ctx = l < depth - 1
        lat = lambda j: mod[l, :B, j][:, None, :]
        cx = lambda j: jnp.broadcast_to(mod[l, B, j][None, None, :], (B, 1, D))
        gain_mix = norm_mix[l][None, :]

        proj = in_proj(x, gain_mix, lat(0), lat(1), w_in_b, l, tm=512)
        projc = in_proj(xc.reshape(1, B * C, D), gain_mix, cx(0)[:1], cx(1)[:1], w_in_b, l,
                        tm=B * C).reshape(B, C, IN_COLS)
        gain_c = jnp.tile(qk_norm_c[l], (1, HEAD_DIM // C_SUB_DIM))
        qk = qk_prep(proj, cosa, sina, cosc, sinc, qk_norm_a[l], qk_norm_b[l], gain_c)
        qkc = qk_prep(projc, one_tab, zero_tab, one_tab, zero_tab, qk_norm_a[l], qk_norm_b[l], gain_c)

        lam_init = 0.8 - 0.6 * math.exp(-0.3 * l)
        lam = _diff_lambda(lambda_c[l], lam_init).reshape(1)
        out_a = attn_a(qk, qkc, sink_a[l])
        out_b = attn_b(qk, qkc, _na_bias_tables(rpb_b[l], L // GRID_W))
        score_bound = (C_SUB_DIM ** 0.5) * jnp.max(jnp.abs(qk_norm_c[l, 0])) * jnp.max(jnp.abs(qk_norm_c[l, 1]))
        attn_c_l = functools.partial(attn_c, qk, qkc, lam, subln_c[l], 1.0 - lam_init)
        out_c = lax.cond(score_bound <= ATTN_C_SAFE_SCORE_BOUND,
                         lambda: attn_c_l(online=False), lambda: attn_c_l(online=True))
        out_d, out_dc = retention(proj, projc, ret_log_decay[l], with_ctx)

        x = out_proj([out_a, out_b, out_c, out_d], x, lat(2), w_out_b, l, tm=1024)
        if with_ctx:
            out_abc_c = ctx_attn(qkc, sink_a[l], lam, subln_c[l], 1.0 - lam_init)
            flat = lambda t: t.reshape(1, B * C, t.shape[-1])
            xc = out_proj([flat(out_abc_c), flat(out_dc)], flat(xc), cx(2)[:1], w_out_b, l,
                          tm=B * C).reshape(B, C, D)

        r_cat = jnp.zeros((D, 128), F32).at[:, :N_GROUPS].set(router_group[l])
        r_cat = r_cat.at[:, N_GROUPS:N_GROUPS + N_EXPERTS].set(router_expert[l])
        if with_ctx:
            shift = jnp.concatenate([lat(3), cx(3)[:1]], axis=0)
            scale = jnp.concatenate([lat(4), cx(4)[:1]], axis=0)
        else:
            shift, scale = lat(3), lat(4)
        h2, rid, rw = ffn_prep(x, xc if with_ctx else None, norm_ffn[l][None, :], shift, scale, r_cat)
        dest, row_tok, block_e, n_used = _route(rid[:, :TOP_K], MOE_TM)
        ys = experts(block_e, n_used, row_tok, h2, w_gate_b, w_up_b, w_down_b, l, MOE_TM)
        x = combine(dest, x, lat(5), rw, ys, 0)
        if with_ctx:
            xc = combine(dest, xc, cx(5), rw, ys, B * L)
    return x
```

```python
import functools
import math

import jax
import jax.numpy as jnp
from jax import lax
from jax.experimental import pallas as pl
from jax.experimental.pallas import tpu as pltpu

F32 = jnp.float32
BF16 = jnp.bfloat16

HEAD_DIM = 128
GRID_W = 64
A_HEADS, A_KV_HEADS, A_WINDOW = 8, 2, 128
B_HEADS, NA_ROWS, NA_COLS = 8, 8, 16
C_HEADS, C_SUB_DIM = 8, 64
D_HEADS = 8
RET_CHUNK = 256
ROPE_BASE = 10000.0
N_GROUPS, EXPERTS_PER_GROUP, TOP_K = 4, 8, 2
N_EXPERTS = N_GROUPS * EXPERTS_PER_GROUP
EPS = 1e-6
NEG_INF = -1e30

A_Q, A_K, A_V = 0, 8, 10
B_Q, B_K, B_V = 12, 20, 28
C_Q, C_K, C_V = 36, 44, 52
D_Q, D_K, D_V, D_G = 60, 68, 76, 84
ABC_BLOCKS = 60
IN_COLS = 92 * 128

VMEM_LIMIT = 56 * 1024 * 1024


def _cparams(sem, vmem=VMEM_LIMIT):
    return pltpu.CompilerParams(dimension_semantics=sem, vmem_limit_bytes=vmem)


def _dot(a, b):
    return jnp.dot(a, b, preferred_element_type=F32)


def _dot_nt(a, b):
    return lax.dot_general(a, b, (((1,), (1,)), ((), ())), preferred_element_type=F32)


def _split_bf16(x):
    hi = x.astype(BF16)
    lo = (x - hi.astype(F32)).astype(BF16)
    return hi, lo


def _pack_bf16_pairs(x):
    half = x.shape[1] // 2
    bits = lax.bitcast_convert_type(x.astype(BF16).astype(F32), jnp.uint32)
    return (bits[:, :half] >> 16) | (bits[:, half:] & jnp.uint32(0xFFFF0000))


def _unpack_bf16_pairs(p):
    lo = lax.bitcast_convert_type(p << 16, F32)
    hi = lax.bitcast_convert_type(p & jnp.uint32(0xFFFF0000), F32)
    return jnp.concatenate([lo, hi], axis=-1)


def _dot3(x, w):
    xh, xl = _split_bf16(x)
    wh, wl = _split_bf16(w)
    return _dot(xh, wh) + _dot(xl, wh) + _dot(xh, wl)


def _mod_kernel(c_ref, w_ref, b_ref, o_ref):
    c = c_ref[...]
    x = c * jax.nn.sigmoid(c)
    o_ref[0] = _dot3(x, w_ref[0]) + b_ref[0]


def mod_vectors(cvec, w_mod, b_mod, tn=512):
    depth, D, N = w_mod.shape
    return pl.pallas_call(
        _mod_kernel,
        out_shape=jax.ShapeDtypeStruct((depth, 8, N), F32),
        grid=(depth, N // tn),
        in_specs=[
            pl.BlockSpec((8, D), lambda l, j: (0, 0)),
            pl.BlockSpec((1, D, tn), lambda l, j: (l, 0, j)),
            pl.BlockSpec((1, 1, tn), lambda l, j: (l, 0, j)),
        ],
        out_specs=pl.BlockSpec((1, 8, tn), lambda l, j: (l, 0, j)),
        compiler_params=_cparams(("arbitrary", "arbitrary")),
        name="mod_vectors",
    )(cvec, w_mod, b_mod.reshape(depth, 1, N))


def _modulated(x, gain, shift, scale):
    ms = jnp.mean(x * x, axis=-1, keepdims=True)
    y = x * lax.rsqrt(ms + EPS) * gain
    return y * (1.0 + scale) + shift


def _in_proj_kernel(x_ref, gain_ref, shift_ref, scale_ref, w_ref, o_ref, h_ref):
    @pl.when(pl.program_id(2) == 0)
    def _():
        h_ref[...] = _modulated(x_ref[0], gain_ref[...], shift_ref[0], scale_ref[0]).astype(BF16)

    o_ref[0] = _dot(h_ref[...], w_ref[0]).astype(o_ref.dtype)


def in_proj(x, gain, shift, scale, w, layer, tm, tn=512):
    B, L, D = x.shape
    N = w.shape[2]
    return pl.pallas_call(
        _in_proj_kernel,
        out_shape=jax.ShapeDtypeStruct((B, L, N), BF16),
        grid=(B, L // tm, N // tn),
        in_specs=[
            pl.BlockSpec((1, tm, D), lambda b, i, j: (b, i, 0)),
            pl.BlockSpec((1, D), lambda b, i, j: (0, 0)),
            pl.BlockSpec((1, 1, D), lambda b, i, j: (b, 0, 0)),
            pl.BlockSpec((1, 1, D), lambda b, i, j: (b, 0, 0)),
            pl.BlockSpec((1, D, tn), lambda b, i, j: (layer, 0, j)),
        ],
        out_specs=pl.BlockSpec((1, tm, tn), lambda b, i, j: (b, i, j)),
        scratch_shapes=[pltpu.VMEM((tm, D), BF16)],
        compiler_params=_cparams(("arbitrary", "arbitrary", "arbitrary")),
        name="in_proj",
    )(x, gain, shift, scale, w)


def _rope_tables(L, dim):
    t = jnp.arange(L)
    row = (t // GRID_W).astype(F32)
    col = (t % GRID_W).astype(F32)
    quarter = dim // 4
    inv = ROPE_BASE ** (-jnp.arange(quarter, dtype=F32) / quarter)
    ar = row[:, None] * inv[None, :]
    ac = col[:, None] * inv[None, :]
    ang = jnp.concatenate([ar, ar, ac, ac], axis=-1)
    sign = jnp.where((jnp.arange(dim) % (dim // 2)) < quarter, -1.0, 1.0).astype(F32)
    reps = HEAD_DIM // dim
    return jnp.tile(jnp.cos(ang), (1, reps)), jnp.tile(jnp.sin(ang) * sign[None, :], (1, reps))


def _qk_prep_kernel(p_ref, cosa_ref, sina_ref, cosc_ref, sinc_ref, ga_ref, gb_ref, gc_ref, o_ref):
    t = p_ref.shape[1]
    lane = lax.broadcasted_iota(jnp.int32, (t, HEAD_DIM), 1)

    ri = lax.broadcasted_iota(jnp.int32, (HEAD_DIM, HEAD_DIM), 0)
    ci = lax.broadcasted_iota(jnp.int32, (HEAD_DIM, HEAD_DIM), 1)
    ones_full = jnp.ones((HEAD_DIM, HEAD_DIM), BF16)
    ones_halves = jnp.where((ri < C_SUB_DIM) == (ci < C_SUB_DIM), 1.0, 0.0).astype(BF16)

    def head_sums(x, ones):
        hi, lo = _split_bf16(x * x)
        return _dot(hi, ones) + _dot(lo, ones)

    def norm_full(x):
        return x * lax.rsqrt(head_sums(x, ones_full) * (1.0 / HEAD_DIM) + EPS)

    def norm_halves(x):
        return x * lax.rsqrt(head_sums(x, ones_halves) * (1.0 / C_SUB_DIM) + EPS)

    def rope(x, cos, sin, dim):
        quarter = dim // 4
        first = (lane % (dim // 2)) < quarter
        rot = jnp.where(first, pltpu.roll(x, HEAD_DIM - quarter, 1), pltpu.roll(x, quarter, 1))
        return x * cos + rot * sin

    cosa, sina = cosa_ref[...], sina_ref[...]
    cosc, sinc = cosc_ref[...], sinc_ref[...]
    scale_ab = HEAD_DIM ** -0.5
    scale_c = C_SUB_DIM ** -0.5 * math.log2(math.e)
    for blk in range(ABC_BLOCKS):
        sl = slice(blk * HEAD_DIM, (blk + 1) * HEAD_DIM)
        if A_V <= blk < B_Q or B_V <= blk < C_Q or C_V <= blk:
            o_ref[0, :, sl] = p_ref[0, :, sl]
            continue
        x = p_ref[0, :, sl].astype(F32)
        if blk < A_K:
            y = rope(norm_full(x) * ga_ref[0:1, :], cosa, sina, HEAD_DIM) * scale_ab
        elif blk < A_V:
            y = rope(norm_full(x) * ga_ref[1:2, :], cosa, sina, HEAD_DIM)
        elif blk < B_K:
            y = norm_full(x) * gb_ref[0:1, :] * scale_ab
        elif blk < B_V:
            y = norm_full(x) * gb_ref[1:2, :]
        elif blk < C_K:
            y = rope(norm_halves(x) * gc_ref[0:1, :], cosc, sinc, C_SUB_DIM) * scale_c
        else:
            y = rope(norm_halves(x) * gc_ref[1:2, :], cosc, sinc, C_SUB_DIM)
        o_ref[0, :, sl] = y.astype(o_ref.dtype)


def qk_prep(proj, cosa, sina, cosc, sinc, ga, gb, gc, t=256):
    B, L, _ = proj.shape
    W = ABC_BLOCKS * HEAD_DIM
    tab = pl.BlockSpec((t, HEAD_DIM), lambda b, i: (i, 0))
    par = pl.BlockSpec((2, HEAD_DIM), lambda b, i: (0, 0))
    return pl.pallas_call(
        _qk_prep_kernel,
        out_shape=jax.ShapeDtypeStruct((B, L, W), BF16),
        grid=(B, L // t),
        in_specs=[pl.BlockSpec((1, t, W), lambda b, i: (b, i, 0)), tab, tab, tab, tab, par, par, par],
        out_specs=pl.BlockSpec((1, t, W), lambda b, i: (b, i, 0)),
        compiler_params=_cparams(("arbitrary", "arbitrary")),
        name="qk_prep",
    )(proj, cosa, sina, cosc, sinc, ga, gb, gc)


def _attn_a_kernel(sink_ref, q_ref, k_ref, v_ref, kc_ref, vc_ref, o_ref, *, tq):
    L = k_ref.shape[1]
    kv, i = pl.program_id(1), pl.program_id(2)
    win = tq + 2 * A_WINDOW
    start = pl.multiple_of(jnp.clip(i * tq - A_WINDOW, 0, L - win), A_WINDOW)
    k = k_ref[0, pl.ds(start, win), :]
    v = v_ref[0, pl.ds(start, win), :]
    kc, vc = kc_ref[0], vc_ref[0]
    qpos = i * tq + lax.broadcasted_iota(jnp.int32, (tq, win), 0)
    kpos = start + lax.broadcasted_iota(jnp.int32, (tq, win), 1)
    in_window = jnp.abs(kpos - qpos) <= A_WINDOW
    group = A_HEADS // A_KV_HEADS
    for g in range(group):
        sl = slice(g * HEAD_DIM, (g + 1) * HEAD_DIM)
        q = q_ref[0, :, sl]
        s = jnp.where(in_window, _dot_nt(q, k), NEG_INF)
        sc = _dot_nt(q, kc)
        sink = sink_ref[kv * group + g]
        m = jnp.maximum(jnp.maximum(jnp.max(s, axis=-1, keepdims=True), jnp.max(sc, axis=-1, keepdims=True)), sink)
        p = jnp.exp(s - m)
        pc = jnp.exp(sc - m)
        denom = jnp.sum(p, axis=-1, keepdims=True) + jnp.sum(pc, axis=-1, keepdims=True) + jnp.exp(sink - m)
        o = _dot(p.astype(BF16), v) + _dot(pc.astype(BF16), vc)
        o_ref[0, :, sl] = (o / denom).astype(o_ref.dtype)


def attn_a(qk, qkc, sink, tq=512):
    B, L, _ = qk.shape
    C = qkc.shape[1]
    gw = (A_HEADS // A_KV_HEADS) * HEAD_DIM
    return pl.pallas_call(
        functools.partial(_attn_a_kernel, tq=tq),
        out_shape=jax.ShapeDtypeStruct((B, L, A_HEADS * HEAD_DIM), BF16),
        grid=(B, A_KV_HEADS, L // tq),
        in_specs=[
            pl.BlockSpec(memory_space=pltpu.SMEM),
            pl.BlockSpec((1, tq, gw), lambda b, kv, i: (b, i, kv)),
            pl.BlockSpec((1, L, HEAD_DIM), lambda b, kv, i: (b, 0, A_K + kv)),
            pl.BlockSpec((1, L, HEAD_DIM), lambda b, kv, i: (b, 0, A_V + kv)),
            pl.BlockSpec((1, C, HEAD_DIM), lambda b, kv, i: (b, 0, A_K + kv)),
            pl.BlockSpec((1, C, HEAD_DIM), lambda b, kv, i: (b, 0, A_V + kv)),
        ],
        out_specs=pl.BlockSpec((1, tq, gw), lambda b, kv, i: (b, i, kv)),
        compiler_params=_cparams(("arbitrary", "arbitrary", "arbitrary")),
        name="attn_a",
    )(sink, qk, qk, qk, qkc, qkc)


NA_QROWS = 8
NA_SPLIT = 2


def _na_bias_tables(rpb, R):
    W = GRID_W
    H = rpb.shape[0]
    wq = jnp.arange(W)
    cs = jnp.clip(wq - NA_COLS // 2, 0, W - NA_COLS)
    colmask = (wq[None, :] >= cs[:, None]) & (wq[None, :] < cs[:, None] + NA_COLS)
    rel_c = jnp.clip(wq[None, :] - wq[:, None] + (NA_COLS - 1), 0, 2 * NA_COLS - 2)
    per_row = jnp.where(colmask[None, None], rpb.astype(F32)[:, :, rel_c], NEG_INF)
    per_row = jnp.transpose(per_row, (0, 2, 1, 3))
    masked = jnp.full((H, W, 1, W), NEG_INF, F32)
    nq, nk = NA_QROWS, 2 * NA_QROWS
    tables = []
    for r0, ks in ((0, 0), (nq, nq - NA_ROWS // 2), (R - nq, R - nk)):
        rows = []
        for a in range(nq):
            qr = r0 + a
            rs = min(max(qr - NA_ROWS // 2, 0), R - NA_ROWS)
            blocks = []
            for c in range(nk):
                kr = ks + c
                if rs <= kr < rs + NA_ROWS:
                    d = kr - qr + (NA_ROWS - 1)
                    blocks.append(per_row[:, :, d:d + 1, :])
                else:
                    blocks.append(masked)
            rows.append(jnp.concatenate(blocks, axis=2))
        tables.append(jnp.stack(rows, axis=1).reshape(H, nq * W, nk * W))
    return jnp.stack(tables)


def _attn_b_kernel(q_ref, k_ref, v_ref, kc_ref, vc_ref, bias_ref, o_ref):
    L = k_ref.shape[1]
    R = L // GRID_W
    i = pl.program_id(2)
    nk = 2 * NA_QROWS
    krow = jnp.clip(i * NA_QROWS - NA_ROWS // 2, 0, R - nk)
    start = pl.multiple_of(krow * GRID_W, (NA_ROWS // 2) * GRID_W)
    k = k_ref[0, pl.ds(start, nk * GRID_W), :]
    v = v_ref[0, pl.ds(start, nk * GRID_W), :]
    kc, vc = kc_ref[0], vc_ref[0]
    sub = q_ref.shape[1] // NA_SPLIT
    for part in range(NA_SPLIT):
        rows = slice(part * sub, (part + 1) * sub)
        q = q_ref[0, rows, :]
        s = _dot_nt(q, k) + bias_ref[0, 0, rows, :]
        sc = _dot_nt(q, kc)
        m = jnp.maximum(jnp.max(s, axis=-1, keepdims=True), jnp.max(sc, axis=-1, keepdims=True))
        p = jnp.exp(s - m)
        pc = jnp.exp(sc - m)
        denom = jnp.sum(p, axis=-1, keepdims=True) + jnp.sum(pc, axis=-1, keepdims=True)
        o = _dot(p.astype(BF16), v) + _dot(pc.astype(BF16), vc)
        o_ref[0, rows, :] = (o / denom).astype(o_ref.dtype)


def attn_b(qk, qkc, bias):
    B, L, _ = qk.shape
    C = qkc.shape[1]
    tq = NA_QROWS * GRID_W
    nblk = L // tq

    def bias_idx(h, b, i):
        return (jnp.where(i == 0, 0, jnp.where(i == nblk - 1, 2, 1)), h, 0, 0)

    return pl.pallas_call(
        _attn_b_kernel,
        out_shape=jax.ShapeDtypeStruct((B, L, B_HEADS * HEAD_DIM), BF16),
        grid=(B_HEADS, B, nblk),
        in_specs=[
            pl.BlockSpec((1, tq, HEAD_DIM), lambda h, b, i: (b, i, B_Q + h)),
            pl.BlockSpec((1, L, HEAD_DIM), lambda h, b, i: (b, 0, B_K + h)),
            pl.BlockSpec((1, L, HEAD_DIM), lambda h, b, i: (b, 0, B_V + h)),
            pl.BlockSpec((1, C, HEAD_DIM), lambda h, b, i: (b, 0, B_K + h)),
            pl.BlockSpec((1, C, HEAD_DIM), lambda h, b, i: (b, 0, B_V + h)),
            pl.BlockSpec((1, 1, tq, 2 * tq), bias_idx),
        ],
        out_specs=pl.BlockSpec((1, tq, HEAD_DIM), lambda h, b, i: (b, i, h)),
        compiler_params=_cparams(("arbitrary", "arbitrary", "arbitrary")),
        name="attn_b",
    )(qk, qk, qk, qkc, qkc, bias)


def _diff_lambda(lambda_params, lam_init):
    lp = lambda_params.astype(F32)
    return jnp.exp(jnp.sum(lp[0] * lp[1])) - jnp.exp(jnp.sum(lp[2] * lp[3])) + lam_init


ATTN_C_SAFE_SCORE_BOUND = 40.0


def _attn_c_kernel(lam_ref, q_ref, k_ref, v_ref, kc_ref, vc_ref, g_ref, o_ref, vt_ref, vct_ref, *scratch,
                   tk, coef, online):
    if online:
        m_ref, l_ref, acc_ref = scratch
        m_ref[...] = jnp.full(m_ref.shape, NEG_INF, F32)
    else:
        l_ref, acc_ref = scratch
    nk = vt_ref.shape[0]

    @pl.when(pl.program_id(2) == 0)
    def _():
        for j in range(nk):
            vt_ref[j] = v_ref[0, j * tk:(j + 1) * tk, :].astype(F32).T.astype(BF16)
        vct_ref[...] = vc_ref[0].astype(F32).T.astype(BF16)

    q = q_ref[0]
    lane = lax.broadcasted_iota(jnp.int32, q.shape, 1)
    zero = jnp.zeros_like(q)
    qs = (jnp.where(lane < C_SUB_DIM, q, zero), jnp.where(lane < C_SUB_DIM, zero, q))

    l_ref[...] = jnp.zeros(l_ref.shape, F32)
    acc_ref[...] = jnp.zeros(acc_ref.shape, F32)

    def accumulate(kblk, vtblk):
        for s_idx in range(2):
            s = _dot_nt(kblk, qs[s_idx])
            if online:
                m_old = m_ref[s_idx]
                m_new = jnp.maximum(m_old, jnp.max(s, axis=0, keepdims=True))
                alpha = jnp.exp2(m_old - m_new)
                p = jnp.exp2(s - m_new)
                l_ref[s_idx] = alpha * l_ref[s_idx] + jnp.sum(p, axis=0, keepdims=True)
                acc_ref[s_idx] = alpha * acc_ref[s_idx] + _dot(vtblk, p.astype(BF16))
                m_ref[s_idx] = m_new
            else:
                p = jnp.exp2(s)
                l_ref[s_idx] += jnp.sum(p, axis=0, keepdims=True)
                acc_ref[s_idx] += _dot(vtblk, p.astype(BF16))

    def body(j, carry):
        accumulate(k_ref[0, pl.ds(pl.multiple_of(j * tk, tk), tk), :], vt_ref[j])
        return carry

    lax.fori_loop(0, nk, body, 0, unroll=4)
    accumulate(kc_ref[0], vct_ref[...])

    o = acc_ref[0] / l_ref[0] - lam_ref[0] * (acc_ref[1] / l_ref[1])
    y = o * lax.rsqrt(jnp.mean(o * o, axis=0, keepdims=True) + EPS) * g_ref[...] * coef
    o_ref[0] = y.T.astype(o_ref.dtype)


def attn_c(qk, qkc, lam, subln, coef, online, tq=2048, tk=512):
    B, L, _ = qk.shape
    C = qkc.shape[1]
    H, dv = C_HEADS, HEAD_DIM
    stats = [pltpu.VMEM((2, 1, tq), F32)] * (2 if online else 1)
    return pl.pallas_call(
        functools.partial(_attn_c_kernel, tk=tk, coef=coef, online=online),
        out_shape=jax.ShapeDtypeStruct((B, L, H * dv), BF16),
        grid=(B, H, L // tq),
        in_specs=[
            pl.BlockSpec(memory_space=pltpu.SMEM),
            pl.BlockSpec((1, tq, HEAD_DIM), lambda b, h, i: (b, i, C_Q + h)),
            pl.BlockSpec((1, L, HEAD_DIM), lambda b, h, i: (b, 0, C_K + h)),
            pl.BlockSpec((1, L, HEAD_DIM), lambda b, h, i: (b, 0, C_V + h)),
            pl.BlockSpec((1, C, HEAD_DIM), lambda b, h, i: (b, 0, C_K + h)),
            pl.BlockSpec((1, C, HEAD_DIM), lambda b, h, i: (b, 0, C_V + h)),
            pl.BlockSpec((dv, 1), lambda b, h, i: (0, 0)),
        ],
        out_specs=pl.BlockSpec((1, tq, dv), lambda b, h, i: (b, i, h)),
        scratch_shapes=[pltpu.VMEM((L // tk, dv, tk), BF16), pltpu.VMEM((dv, C), BF16)]
        + stats + [pltpu.VMEM((2, dv, tq), F32)],
        compiler_params=_cparams(("arbitrary", "arbitrary", "arbitrary")),
        name="attn_c_online" if online else "attn_c",
    )(lam, qk, qk, qk, qkc, qkc, subln.reshape(dv, 1).astype(F32))


def _retention_kernel(ld_ref, q_ref, k_ref, v_ref, g_ref, qc_ref, kc_ref, vc_ref, gc_ref, *rest, with_ctx):
    if with_ctx:
        y_ref, yc_ref, of_ref, ob_ref, ocf_ref = rest
    else:
        y_ref, of_ref, ob_ref = rest
        yc_ref = ocf_ref = None
    L, C = q_ref.shape[1], qc_ref.shape[1]
    c = min(RET_CHUNK, L)
    cc = min(RET_CHUNK, C)
    h = pl.program_id(1)
    scale = HEAD_DIM ** -0.5

    def decays(lg, reverse, c):
        row = lax.broadcasted_iota(jnp.int32, (c, c), 0).astype(F32)
        col = lax.broadcasted_iota(jnp.int32, (c, c), 1).astype(F32)
        rowp = lax.broadcasted_iota(jnp.int32, (c, HEAD_DIM), 0).astype(F32)
        if reverse:
            rel, q_exp, k_exp = col - row, c - rowp, rowp
        else:
            rel, q_exp, k_exp = row - col, rowp + 1.0, (c - 1.0) - rowp
        intra = jnp.where(rel >= 0, jnp.exp(lg * jnp.maximum(rel, 0.0)), 0.0)
        return intra, jnp.exp(lg * q_exp), jnp.exp(lg * k_exp), jnp.exp(lg * c)

    def step(S, q, k, v, dec, want_out):
        intra, q_decay, k_decay, chunk_decay = dec
        o = None
        if want_out:
            qs = (q.astype(F32) * scale).astype(BF16)
            s = _dot_nt(qs, k) * intra
            o = _dot(s.astype(BF16), v) + _dot(qs, S.astype(BF16)) * q_decay
        kd_t = (k.astype(F32) * k_decay).T.astype(BF16)
        return S * chunk_decay + _dot(kd_t, v), o

    def gated(o, g):
        g = g.astype(F32)
        y = o * lax.rsqrt(jnp.mean(o * o, axis=-1, keepdims=True) + EPS)
        return y * (g * jax.nn.sigmoid(g))

    def lat(ref, j):
        return ref[0, pl.ds(pl.multiple_of(j * c, c), c), :]

    def ctx(ref, j):
        return ref[0, j * cc:(j + 1) * cc, :]

    dec_f = decays(ld_ref[0, h], False, cc)
    dec_b = decays(ld_ref[1, h], True, cc)
    S_f = jnp.zeros((HEAD_DIM, HEAD_DIM), F32)
    for j in range(C // cc):
        S_f, o = step(S_f, ctx(qc_ref, j), ctx(kc_ref, j), ctx(vc_ref, j), dec_f, with_ctx)
        if with_ctx:
            ocf_ref[j * cc:(j + 1) * cc, :] = o
    S_b = jnp.zeros((HEAD_DIM, HEAD_DIM), F32)
    for j in reversed(range(C // cc)):
        S_b, o = step(S_b, ctx(qc_ref, j), ctx(kc_ref, j), ctx(vc_ref, j), dec_b, with_ctx)
        if with_ctx:
            yc_ref[0, j * cc:(j + 1) * cc, :] = gated(o + ocf_ref[j * cc:(j + 1) * cc, :], ctx(gc_ref, j)).astype(yc_ref.dtype)
    if c != cc:
        dec_f = decays(ld_ref[0, h], False, c)
        dec_b = decays(ld_ref[1, h], True, c)

    n = L // c

    def both(t, carry):
        S_f, S_b = carry
        S_f, o_f = step(S_f, lat(q_ref, t), lat(k_ref, t), lat(v_ref, t), dec_f, True)
        of_ref[pl.ds(pl.multiple_of(t * c, c), c), :] = o_f
        j = n - 1 - t
        S_b, o_b = step(S_b, lat(q_ref, j), lat(k_ref, j), lat(v_ref, j), dec_b, True)
        ob_ref[pl.ds(pl.multiple_of(j * c, c), c), :] = o_b
        return S_f, S_b

    lax.fori_loop(0, n, both, (S_f, S_b), unroll=2)

    def finish(j, carry):
        sl = pl.ds(pl.multiple_of(j * c, c), c)
        y_ref[0, sl, :] = gated(of_ref[sl, :] + ob_ref[sl, :], lat(g_ref, j)).astype(y_ref.dtype)
        return carry

    lax.fori_loop(0, n, finish, 0, unroll=2)


def retention(proj, projc, log_decay, with_ctx):
    B, L, _ = proj.shape
    C = projc.shape[1]

    def lat(blk):
        return pl.BlockSpec((1, L, HEAD_DIM), lambda b, h: (b, 0, blk + h))

    def ctx(blk):
        return pl.BlockSpec((1, C, HEAD_DIM), lambda b, h: (b, 0, blk + h))

    out_shape = [jax.ShapeDtypeStruct((B, L, D_HEADS * HEAD_DIM), BF16)]
    out_specs = [pl.BlockSpec((1, L, HEAD_DIM), lambda b, h: (b, 0, h))]
    scratch = [pltpu.VMEM((L, HEAD_DIM), F32), pltpu.VMEM((L, HEAD_DIM), F32)]
    if with_ctx:
        out_shape.append(jax.ShapeDtypeStruct((B, C, D_HEADS * HEAD_DIM), BF16))
        out_specs.append(pl.BlockSpec((1, C, HEAD_DIM), lambda b, h: (b, 0, h)))
        scratch.append(pltpu.VMEM((C, HEAD_DIM), F32))
    outs = pl.pallas_call(
        functools.partial(_retention_kernel, with_ctx=with_ctx),
        out_shape=out_shape,
        grid=(B, D_HEADS),
        in_specs=[pl.BlockSpec(memory_space=pltpu.SMEM),
                  lat(D_Q), lat(D_K), lat(D_V), lat(D_G), ctx(D_Q), ctx(D_K), ctx(D_V), ctx(D_G)],
        out_specs=out_specs,
        scratch_shapes=scratch,
        compiler_params=_cparams(("arbitrary", "arbitrary")),
        name="retention_ctx" if with_ctx else "retention",
    )(log_decay.astype(F32), proj, proj, proj, proj, projc, projc, projc, projc)
    return (outs[0], outs[1]) if with_ctx else (outs[0], None)


def _ctx_attn_kernel(sink_ref, lam_ref, qk_ref, g_ref, o_ref, *, coef):
    def blk(idx):
        return qk_ref[0, :, idx * HEAD_DIM:(idx + 1) * HEAD_DIM]

    def softmax_out(q, k, v, sink=None, exp=jnp.exp):
        s = _dot_nt(q, k)
        m = jnp.max(s, axis=-1, keepdims=True)
        if sink is not None:
            m = jnp.maximum(m, sink)
        p = exp(s - m)
        denom = jnp.sum(p, axis=-1, keepdims=True)
        if sink is not None:
            denom = denom + jnp.exp(sink - m)
        return _dot(p.astype(BF16), v) / denom

    group = A_HEADS // A_KV_HEADS
    for h in range(A_HEADS):
        o = softmax_out(blk(A_Q + h), blk(A_K + h // group), blk(A_V + h // group), sink_ref[h])
        o_ref[0, :, h * HEAD_DIM:(h + 1) * HEAD_DIM] = o.astype(o_ref.dtype)
    for h in range(B_HEADS):
        o = softmax_out(blk(B_Q + h), blk(B_K + h), blk(B_V + h))
        o_ref[0, :, (A_HEADS + h) * HEAD_DIM:(A_HEADS + h + 1) * HEAD_DIM] = o.astype(o_ref.dtype)
    lane = lax.broadcasted_iota(jnp.int32, (qk_ref.shape[1], HEAD_DIM), 1)
    for h in range(C_HEADS):
        q, k, v = blk(C_Q + h), blk(C_K + h), blk(C_V + h)
        zero = jnp.zeros_like(q)
        o1 = softmax_out(jnp.where(lane < C_SUB_DIM, q, zero), k, v, exp=jnp.exp2)
        o2 = softmax_out(jnp.where(lane < C_SUB_DIM, zero, q), k, v, exp=jnp.exp2)
        o = o1 - lam_ref[0] * o2
        y = o * lax.rsqrt(jnp.mean(o * o, axis=-1, keepdims=True) + EPS) * g_ref[...] * coef
        base = A_HEADS + B_HEADS + h
        o_ref[0, :, base * HEAD_DIM:(base + 1) * HEAD_DIM] = y.astype(o_ref.dtype)


def ctx_attn(qkc, sink, lam, subln, coef):
    B, C, W = qkc.shape
    n_out = (A_HEADS + B_HEADS + C_HEADS) * HEAD_DIM
    return pl.pallas_call(
        functools.partial(_ctx_attn_kernel, coef=coef),
        out_shape=jax.ShapeDtypeStruct((B, C, n_out), BF16),
        grid=(B,),
        in_specs=[
            pl.BlockSpec(memory_space=pltpu.SMEM),
            pl.BlockSpec(memory_space=pltpu.SMEM),
            pl.BlockSpec((1, C, W), lambda b: (b, 0, 0)),
            pl.BlockSpec((1, HEAD_DIM), lambda b: (0, 0)),
        ],
        out_specs=pl.BlockSpec((1, C, n_out), lambda b: (b, 0, 0)),
        compiler_params=_cparams(("arbitrary",)),
        name="ctx_attn",
    )(sink, lam, qkc, subln.reshape(1, HEAD_DIM).astype(F32))


def _out_proj_kernel(*refs, n_parts):
    parts = refs[:n_parts]
    x_ref, gate_ref, w_ref, o_ref = refs[n_parts:]
    mix = jnp.concatenate([p[0] for p in parts], axis=-1)
    o_ref[0] = x_ref[0] + gate_ref[0] * _dot(mix, w_ref[0])


def out_proj(parts, x, gate, w, layer, tm, tn=512):
    B, L, D = x.shape
    part_specs = [pl.BlockSpec((1, tm, p.shape[2]), lambda b, i, j: (b, i, 0)) for p in parts]
    return pl.pallas_call(
        functools.partial(_out_proj_kernel, n_parts=len(parts)),
        out_shape=jax.ShapeDtypeStruct((B, L, D), F32),
        grid=(B, L // tm, D // tn),
        in_specs=part_specs + [
            pl.BlockSpec((1, tm, tn), lambda b, i, j: (b, i, j)),
            pl.BlockSpec((1, 1, tn), lambda b, i, j: (b, 0, j)),
            pl.BlockSpec((1, w.shape[1], tn), lambda b, i, j: (layer, 0, j)),
        ],
        out_specs=pl.BlockSpec((1, tm, tn), lambda b, i, j: (b, i, j)),
        compiler_params=_cparams(("arbitrary", "arbitrary", "arbitrary")),
        name="out_proj",
    )(*parts, x, gate, w)


def _ffn_prep_kernel(*refs, n_lat_tiles):
    if n_lat_tiles is None:
        x_ref, gain_ref, shift_ref, scale_ref, rh_ref, rl_ref, h_ref, id_ref, w_ref = refs
        xc_ref = None
    else:
        x_ref, xc_ref, gain_ref, shift_ref, scale_ref, rh_ref, rl_ref, h_ref, id_ref, w_ref = refs

    def body(src_ref):
        h = _modulated(src_ref[...], gain_ref[...], shift_ref[0], scale_ref[0])
        h_ref[...] = _pack_bf16_pairs(h)

        h_hi, h_lo = _split_bf16(h)
        logits = _dot(h_hi, rh_ref[...]) + _dot(h_lo, rh_ref[...]) + _dot(h_hi, rl_ref[...])
        lane = lax.broadcasted_iota(jnp.int32, logits.shape, 1)
        big = jnp.int32(logits.shape[1])

        def first_argmax(vals):
            top = jnp.max(vals, axis=-1, keepdims=True)
            return top, jnp.min(jnp.where(vals == top, lane, big), axis=-1, keepdims=True)

        is_group = lane < N_GROUPS
        g_top, g_idx = first_argmax(jnp.where(is_group, logits, NEG_INF))
        g_val = 1.0 / jnp.sum(jnp.where(is_group, jnp.exp(logits - g_top), 0.0), axis=-1, keepdims=True)
        lo = N_GROUPS + EXPERTS_PER_GROUP * g_idx
        e_logits = jnp.where((lane >= lo) & (lane < lo + EXPERTS_PER_GROUP), logits, NEG_INF)
        v1, i1 = first_argmax(e_logits)
        v2, i2 = first_argmax(jnp.where(lane == i1, NEG_INF, e_logits))
        e21 = jnp.exp(v2 - v1)
        w1 = g_val / (1.0 + e21)
        w2 = g_val * e21 / (1.0 + e21)
        id_ref[...] = jnp.where(lane == 0, i1 - N_GROUPS, jnp.where(lane == 1, i2 - N_GROUPS, 0))
        w_ref[...] = jnp.where(lane == 0, w1, jnp.where(lane == 1, w2, 0.0))

    if xc_ref is None:
        body(x_ref)
    else:
        pl.when(pl.program_id(0) < n_lat_tiles)(lambda: body(x_ref))
        pl.when(pl.program_id(0) >= n_lat_tiles)(lambda: body(xc_ref))


def ffn_prep(x, xc, gain, shift, scale, r_cat, t=256):
    B, L, D = x.shape
    n_lat = B * L // t
    lat_per_seg = L // t
    if xc is None:
        T = B * L
        xs = [x.reshape(B * L, D)]
        x_specs = [pl.BlockSpec((t, D), lambda i: (i, 0))]
        seg = lambda i: (i // lat_per_seg, 0, 0)
        n_lat_tiles = None
    else:
        C = xc.shape[1]
        T = B * (L + C)
        xs = [x.reshape(B * L, D), xc.reshape(B * C, D)]
        x_specs = [pl.BlockSpec((t, D), lambda i: (jnp.minimum(i, n_lat - 1), 0)),
                   pl.BlockSpec((t, D), lambda i: (jnp.maximum(i - n_lat, 0), 0))]
        seg = lambda i: (jnp.minimum(i // lat_per_seg, B), 0, 0)
        n_lat_tiles = n_lat
    return pl.pallas_call(
        functools.partial(_ffn_prep_kernel, n_lat_tiles=n_lat_tiles),
        out_shape=[jax.ShapeDtypeStruct((T, D // 2), jnp.uint32),
                   jax.ShapeDtypeStruct((T, 128), jnp.int32),
                   jax.ShapeDtypeStruct((T, 128), F32)],
        grid=(T // t,),
        in_specs=x_specs + [
            pl.BlockSpec((1, D), lambda i: (0, 0)),
            pl.BlockSpec((1, 1, D), seg),
            pl.BlockSpec((1, 1, D), seg),
            pl.BlockSpec((D, 128), lambda i: (0, 0)),
            pl.BlockSpec((D, 128), lambda i: (0, 0)),
        ],
        out_specs=[pl.BlockSpec((t, D // 2), lambda i: (i, 0)),
                   pl.BlockSpec((t, 128), lambda i: (i, 0)),
                   pl.BlockSpec((t, 128), lambda i: (i, 0))],
        compiler_params=_cparams(("arbitrary",)),
        name="ffn_prep",
    )(*xs, gain, shift, scale, *_split_bf16(r_cat))


def _route(ids, tm):
    A = ids.shape[0] * TOP_K
    flat_e = ids.reshape(A)
    onehot = (flat_e[:, None] == jnp.arange(N_EXPERTS, dtype=jnp.int32)[None, :]).astype(jnp.int32)
    csum = jnp.cumsum(onehot, axis=0)
    rank = jnp.sum(csum * onehot, axis=1) - 1
    counts = csum[-1]
    pcounts = (counts + tm - 1) // tm * tm
    pends = jnp.cumsum(pcounts)
    pstarts = pends - pcounts
    dest = jnp.sum(onehot * pstarts[None, :], axis=1) + rank
    NB = (A + N_EXPERTS * (tm - 1)) // tm
    row_tok = jnp.zeros((NB * tm,), jnp.int32).at[dest].set(jnp.arange(A, dtype=jnp.int32) // TOP_K)
    block_start = jnp.arange(NB, dtype=jnp.int32) * tm
    block_e = jnp.minimum(jnp.sum((pends[None, :] <= block_start[:, None]).astype(jnp.int32), axis=1), N_EXPERTS - 1)
    return dest.astype(jnp.int32), row_tok, block_e.astype(jnp.int32), (pends[-1:] // tm).astype(jnp.int32)


def _row_copy(src, src_row, dst, dst_row, sem):
    return pltpu.make_async_copy(src.at[pl.ds(src_row, 1), :], dst.at[pl.ds(dst_row, 1), :], sem)


def _expert_kernel(be_ref, nused_ref, tok_ref, h_ref, wg_ref, wu_ref, wd_ref, o_ref, xbuf, sem):
    i = pl.program_id(0)
    tm = o_ref.shape[0]
    n_used = nused_ref[0]

    def row(r, blk, slot):
        return _row_copy(h_ref, tok_ref[blk * tm + r], xbuf.at[slot], r, sem.at[slot])

    def gather(blk, slot):
        def issue(r2, carry):
            for k in range(2):
                row(2 * r2 + k, blk, slot).start(priority=k)
            return carry
        lax.fori_loop(0, tm // 2, issue, 0, unroll=4)

    def drain(slot):
        def wait(r, carry):
            row(r, 0, slot).wait()
            return carry
        lax.fori_loop(0, tm, wait, 0, unroll=8)

    @pl.when(i == 0)
    def _():
        gather(0, 0)

    @pl.when(i + 1 < n_used)
    def _():
        gather(i + 1, (i + 1) % 2)

    @pl.when(i < n_used)
    def _():
        slot = i % 2
        drain(slot)
        x = _unpack_bf16_pairs(xbuf[slot]).astype(BF16)
        a = _dot(x, wg_ref[0, 0])
        u = _dot(x, wu_ref[0, 0])
        hmid = (a * jax.nn.sigmoid(a) * u).astype(BF16)
        o_ref[...] = _pack_bf16_pairs(_dot(hmid, wd_ref[0, 0]))

    @pl.when(i >= n_used)
    def _():
        o_ref[...] = jnp.zeros(o_ref.shape, o_ref.dtype)


def experts(block_e, n_used, row_tok, h, w_gate, w_up, w_down, layer, tm):
    P = row_tok.shape[0]
    _, _, D, De = w_gate.shape
    w_idx = lambda i, be, nu, tok: (layer, be[jnp.minimum(i, nu[0] - 1)], 0, 0)
    return pl.pallas_call(
        _expert_kernel,
        out_shape=jax.ShapeDtypeStruct((P, D // 2), jnp.uint32),
        grid_spec=pltpu.PrefetchScalarGridSpec(
            num_scalar_prefetch=3,
            grid=(P // tm,),
            in_specs=[
                pl.BlockSpec(memory_space=pl.ANY),
                pl.BlockSpec((1, 1, D, De), w_idx),
                pl.BlockSpec((1, 1, D, De), w_idx),
                pl.BlockSpec((1, 1, De, D), w_idx),
            ],
            out_specs=pl.BlockSpec((tm, D // 2), lambda i, be, nu, tok: (i, 0)),
            scratch_shapes=[pltpu.VMEM((2, tm, D // 2), jnp.uint32), pltpu.SemaphoreType.DMA((2,))],
        ),
        compiler_params=_cparams(("arbitrary",)),
        name="moe_experts",
    )(block_e, n_used, row_tok, h, w_gate, w_up, w_down)


def _combine_kernel(dest_ref, x_ref, gate_ref, w_ref, y_ref, o_ref, buf, sem, *, tok_offset):
    t, D = x_ref.shape
    i = pl.program_id(0)
    slot = i % 2

    def gather(step, slot):
        base = (tok_offset + step * t) * TOP_K

        def issue(r, carry):
            for k in range(TOP_K):
                _row_copy(y_ref, dest_ref[base + r * TOP_K + k], buf.at[slot, k], r, sem.at[slot]).start(priority=k)
            return carry
        lax.fori_loop(0, t, issue, 0, unroll=4)

    def drain(r, carry):
        _row_copy(y_ref, 0, buf.at[slot, 0], 0, sem.at[slot]).wait()
        return carry

    @pl.when(i == 0)
    def _():
        gather(0, 0)

    @pl.when(i + 1 < pl.num_programs(0))
    def _():
        gather(i + 1, 1 - slot)

    lax.fori_loop(0, t * TOP_K, drain, 0, unroll=8)
    w = w_ref[...]
    y = w[:, 0:1] * _unpack_bf16_pairs(buf[slot, 0]) + w[:, 1:2] * _unpack_bf16_pairs(buf[slot, 1])
    o_ref[...] = x_ref[...] + gate_ref[0] * y


def combine(dest, x, gate, rw, ys, tok_offset, t=256):
    B, L, D = x.shape
    per_seg = L // t
    woff = tok_offset // t
    out = pl.pallas_call(
        functools.partial(_combine_kernel, tok_offset=tok_offset),
        out_shape=jax.ShapeDtypeStruct((B * L, D), F32),
        grid_spec=pltpu.PrefetchScalarGridSpec(
            num_scalar_prefetch=1,
            grid=(B * L // t,),
            in_specs=[
                pl.BlockSpec((t, D), lambda i, d: (i, 0)),
                pl.BlockSpec((1, 1, D), lambda i, d: (i // per_seg, 0, 0)),
                pl.BlockSpec((t, 128), lambda i, d: (woff + i, 0)),
                pl.BlockSpec(memory_space=pl.ANY),
            ],
            out_specs=pl.BlockSpec((t, D), lambda i, d: (i, 0)),
            scratch_shapes=[pltpu.VMEM((2, TOP_K, t, D // 2), jnp.uint32), pltpu.SemaphoreType.DMA((2,))],
        ),
        compiler_params=_cparams(("arbitrary",)),
        name="moe_combine",
    )(dest, x.reshape(B * L, D), gate, rw, ys)
    return out.reshape(B, L, D)


MOE_TM = 256


def kernel(x, c, ctx, c_ctx, w_mod, b_mod, norm_mix, norm_ffn, w_in, w_out, qk_norm_a, sink_a, qk_norm_b, rpb_b,
           qk_norm_c, lambda_c, subln_c, ret_log_decay, router_group, router_expert, w_gate, w_up, w_down):
    B, L, D = x.shape
    C = ctx.shape[1]
    depth = w_mod.shape[0]

    cvec = jnp.zeros((8, D), F32).at[:B].set(c).at[B].set(c_ctx)
    mod = mod_vectors(cvec, w_mod, b_mod).reshape(depth, 8, 6, D)

    cosa, sina = _rope_tables(L, HEAD_DIM)
    cosc, sinc = _rope_tables(L, C_SUB_DIM)
    one_tab = jnp.ones((C, HEAD_DIM), F32)
    zero_tab = jnp.zeros((C, HEAD_DIM), F32)

    w_in_b, w_out_b = w_in.astype(BF16), w_out.astype(BF16)
    w_gate_b, w_up_b, w_down_b = w_gate.astype(BF16), w_up.astype(BF16), w_down.astype(BF16)

    xc = ctx
    for l in range(depth):
        with_ctx = l < depth - 1
        lat = lambda j: mod[l, :B, j][:, None, :]
        cx = lambda j: jnp.broadcast_to(mod[l, B, j][None, None, :], (B, 1, D))
        gain_mix = norm_mix[l][None, :]

        proj = in_proj(x, gain_mix, lat(0), lat(1), w_in_b, l, tm=512)
        projc = in_proj(xc.reshape(1, B * C, D), gain_mix, cx(0)[:1], cx(1)[:1], w_in_b, l,
                        tm=B * C).reshape(B, C, IN_COLS)
        gain_c = jnp.tile(qk_norm_c[l], (1, HEAD_DIM // C_SUB_DIM))
        qk = qk_prep(proj, cosa, sina, cosc, sinc, qk_norm_a[l], qk_norm_b[l], gain_c)
        qkc = qk_prep(projc, one_tab, zero_tab, one_tab, zero_tab, qk_norm_a[l], qk_norm_b[l], gain_c)

        lam_init = 0.8 - 0.6 * math.exp(-0.3 * l)
        lam = _diff_lambda(lambda_c[l], lam_init).reshape(1)
        out_a = attn_a(qk, qkc, sink_a[l])
        out_b = attn_b(qk, qkc, _na_bias_tables(rpb_b[l], L // GRID_W))
        score_bound = (C_SUB_DIM ** 0.5) * jnp.max(jnp.abs(qk_norm_c[l, 0])) * jnp.max(jnp.abs(qk_norm_c[l, 1]))
        attn_c_l = functools.partial(attn_c, qk, qkc, lam, subln_c[l], 1.0 - lam_init)
        out_c = lax.cond(score_bound <= ATTN_C_SAFE_SCORE_BOUND,
                         lambda: attn_c_l(online=False), lambda: attn_c_l(online=True))
        out_d, out_dc = retention(proj, projc, ret_log_decay[l], with_ctx)

        x = out_proj([out_a, out_b, out_c, out_d], x, lat(2), w_out_b, l, tm=1024)
        if with_ctx:
            out_abc_c = ctx_attn(qkc, sink_a[l], lam, subln_c[l], 1.0 - lam_init)
            flat = lambda t: t.reshape(1, B * C, t.shape[-1])
            xc = out_proj([flat(out_abc_c), flat(out_dc)], flat(xc), cx(2)[:1], w_out_b, l,
                          tm=B * C).reshape(B, C, D)

        r_cat = jnp.zeros((D, 128), F32).at[:, :N_GROUPS].set(router_group[l])
        r_cat = r_cat.at[:, N_GROUPS:N_GROUPS + N_EXPERTS].set(router_expert[l])
        if with_ctx:
            shift = jnp.concatenate([lat(3), cx(3)[:1]], axis=0)
            scale = jnp.concatenate([lat(4), cx(4)[:1]], axis=0)
        else:
            shift, scale = lat(3), lat(4)
        h2, rid, rw = ffn_prep(x, xc if with_ctx else None, norm_ffn[l][None, :], shift, scale, r_cat)
        dest, row_tok, block_e, n_used = _route(rid[:, :TOP_K], MOE_TM)
        ys = experts(block_e, n_used, row_tok, h2, w_gate_b, w_up_b, w_down_b, l, MOE_TM)
        x = combine(dest, x, lat(5), rw, ys, 0)
        if with_ctx:
            xc = combine(dest, xc, cx(5), rw, ys, B * L)
    return x
```

```python
import functools
import math

import jax
import jax.numpy as jnp
from jax import lax
from jax.experimental import pallas as pl
from jax.experimental.pallas import tpu as pltpu

F32 = jnp.float32
BF16 = jnp.bfloat16

HEAD_DIM = 128
GRID_W = 64
A_HEADS, A_KV_HEADS, A_WINDOW = 8, 2, 128
B_HEADS, NA_ROWS, NA_COLS = 8, 8, 16
C_HEADS, C_SUB_DIM = 8, 64
D_HEADS = 8
RET_CHUNK = 256
ROPE_BASE = 10000.0
N_GROUPS, EXPERTS_PER_GROUP, TOP_K = 4, 8, 2
N_EXPERTS = N_GROUPS * EXPERTS_PER_GROUP
EPS = 1e-6
NEG_INF = -1e30

A_Q, A_K, A_V = 0, 8, 10
B_Q, B_K, B_V = 12, 20, 28
C_Q, C_K, C_V = 36, 44, 52
D_Q, D_K, D_V, D_G = 60, 68, 76, 84
ABC_BLOCKS = 60
IN_COLS = 92 * 128

VMEM_LIMIT = 56 * 1024 * 1024


def _cparams(sem, vmem=VMEM_LIMIT):
    return pltpu.CompilerParams(dimension_semantics=sem, vmem_limit_bytes=vmem)


def _dot(a, b):
    return jnp.dot(a, b, preferred_element_type=F32)


def _dot_nt(a, b):
    return lax.dot_general(a, b, (((1,), (1,)), ((), ())), preferred_element_type=F32)


def _split_bf16(x):
    hi = x.astype(BF16)
    lo = (x - hi.astype(F32)).astype(BF16)
    return hi, lo


def _pack_bf16_pairs(x):
    half = x.shape[1] // 2
    bits = lax.bitcast_convert_type(x.astype(BF16).astype(F32), jnp.uint32)
    return (bits[:, :half] >> 16) | (bits[:, half:] & jnp.uint32(0xFFFF0000))


def _unpack_bf16_pairs(p):
    lo = lax.bitcast_convert_type(p << 16, F32)
    hi = lax.bitcast_convert_type(p & jnp.uint32(0xFFFF0000), F32)
    return jnp.concatenate([lo, hi], axis=-1)


def _dot3(x, w):
    xh, xl = _split_bf16(x)
    wh, wl = _split_bf16(w)
    return _dot(xh, wh) + _dot(xl, wh) + _dot(xh, wl)


def _mod_kernel(c_ref, w_ref, b_ref, o_ref):
    c = c_ref[...]
    x = c * jax.nn.sigmoid(c)
    o_ref[0] = _dot3(x, w_ref[0]) + b_ref[0]


def mod_vectors(cvec, w_mod, b_mod, tn=512):
    depth, D, N = w_mod.shape
    return pl.pallas_call(
        _mod_kernel,
        out_shape=jax.ShapeDtypeStruct((depth, 8, N), F32),
        grid=(depth, N // tn),
        in_specs=[
            pl.BlockSpec((8, D), lambda l, j: (0, 0)),
            pl.BlockSpec((1, D, tn), lambda l, j: (l, 0, j)),
            pl.BlockSpec((1, 1, tn), lambda l, j: (l, 0, j)),
        ],
        out_specs=pl.BlockSpec((1, 8, tn), lambda l, j: (l, 0, j)),
        compiler_params=_cparams(("arbitrary", "arbitrary")),
        name="mod_vectors",
    )(cvec, w_mod, b_mod.reshape(depth, 1, N))


def _modulated(x, gain, shift, scale):
    ms = jnp.mean(x * x, axis=-1, keepdims=True)
    y = x * lax.rsqrt(ms + EPS) * gain
    return y * (1.0 + scale) + shift


def _in_proj_kernel(x_ref, gain_ref, shift_ref, scale_ref, w_ref, o_ref, h_ref):
    @pl.when(pl.program_id(2) == 0)
    def _():
        h_ref[...] = _modulated(x_ref[0], gain_ref[...], shift_ref[0], scale_ref[0]).astype(BF16)

    o_ref[0] = _dot(h_ref[...], w_ref[0]).astype(o_ref.dtype)


def in_proj(x, gain, shift, scale, w, layer, tm, tn=512):
    B, L, D = x.shape
    N = w.shape[2]
    return pl.pallas_call(
        _in_proj_kernel,
        out_shape=jax.ShapeDtypeStruct((B, L, N), BF16),
        grid=(B, L // tm, N // tn),
        in_specs=[
            pl.BlockSpec((1, tm, D), lambda b, i, j: (b, i, 0)),
            pl.BlockSpec((1, D), lambda b, i, j: (0, 0)),
            pl.BlockSpec((1, 1, D), lambda b, i, j: (b, 0, 0)),
            pl.BlockSpec((1, 1, D), lambda b, i, j: (b, 0, 0)),
            pl.BlockSpec((1, D, tn), lambda b, i, j: (layer, 0, j)),
        ],
        out_specs=pl.BlockSpec((1, tm, tn), lambda b, i, j: (b, i, j)),
        scratch_shapes=[pltpu.VMEM((tm, D), BF16)],
        compiler_params=_cparams(("arbitrary", "arbitrary", "arbitrary")),
        name="in_proj",
    )(x, gain, shift, scale, w)


def _rope_tables(L, dim):
    t = jnp.arange(L)
    row = (t // GRID_W).astype(F32)
    col = (t % GRID_W).astype(F32)
    quarter = dim // 4
    inv = ROPE_BASE ** (-jnp.arange(quarter, dtype=F32) / quarter)
    ar = row[:, None] * inv[None, :]
    ac = col[:, None] * inv[None, :]
    ang = jnp.concatenate([ar, ar, ac, ac], axis=-1)
    sign = jnp.where((jnp.arange(dim) % (dim // 2)) < quarter, -1.0, 1.0).astype(F32)
    reps = HEAD_DIM // dim
    return jnp.tile(jnp.cos(ang), (1, reps)), jnp.tile(jnp.sin(ang) * sign[None, :], (1, reps))


def _qk_prep_kernel(p_ref, cosa_ref, sina_ref, cosc_ref, sinc_ref, ga_ref, gb_ref, gc_ref, o_ref):
    t = p_ref.shape[1]
    lane = lax.broadcasted_iota(jnp.int32, (t, HEAD_DIM), 1)

    ri = lax.broadcasted_iota(jnp.int32, (HEAD_DIM, HEAD_DIM), 0)
    ci = lax.broadcasted_iota(jnp.int32, (HEAD_DIM, HEAD_DIM), 1)
    ones_full = jnp.ones((HEAD_DIM, HEAD_DIM), BF16)
    ones_halves = jnp.where((ri < C_SUB_DIM) == (ci < C_SUB_DIM), 1.0, 0.0).astype(BF16)

    def head_sums(x, ones):
        hi, lo = _split_bf16(x * x)
        return _dot(hi, ones) + _dot(lo, ones)

    def norm_full(x):
        return x * lax.rsqrt(head_sums(x, ones_full) * (1.0 / HEAD_DIM) + EPS)

    def norm_halves(x):
        return x * lax.rsqrt(head_sums(x, ones_halves) * (1.0 / C_SUB_DIM) + EPS)

    def rope(x, cos, sin, dim):
        quarter = dim // 4
        first = (lane % (dim // 2)) < quarter
        rot = jnp.where(first, pltpu.roll(x, HEAD_DIM - quarter, 1), pltpu.roll(x, quarter, 1))
        return x * cos + rot * sin

    cosa, sina = cosa_ref[...], sina_ref[...]
    cosc, sinc = cosc_ref[...], sinc_ref[...]
    scale_ab = HEAD_DIM ** -0.5
    scale_c = C_SUB_DIM ** -0.5 * math.log2(math.e)
    for blk in range(ABC_BLOCKS):
        sl = slice(blk * HEAD_DIM, (blk + 1) * HEAD_DIM)
        if A_V <= blk < B_Q or B_V <= blk < C_Q or C_V <= blk:
            o_ref[0, :, sl] = p_ref[0, :, sl]
            continue
        x = p_ref[0, :, sl].astype(F32)
        if blk < A_K:
            y = rope(norm_full(x) * ga_ref[0:1, :], cosa, sina, HEAD_DIM) * scale_ab
        elif blk < A_V:
            y = rope(norm_full(x) * ga_ref[1:2, :], cosa, sina, HEAD_DIM)
        elif blk < B_K:
            y = norm_full(x) * gb_ref[0:1, :] * scale_ab
        elif blk < B_V:
            y = norm_full(x) * gb_ref[1:2, :]
        elif blk < C_K:
            y = rope(norm_halves(x) * gc_ref[0:1, :], cosc, sinc, C_SUB_DIM) * scale_c
        else:
            y = rope(norm_halves(x) * gc_ref[1:2, :], cosc, sinc, C_SUB_DIM)
        o_ref[0, :, sl] = y.astype(o_ref.dtype)


def qk_prep(proj, cosa, sina, cosc, sinc, ga, gb, gc, t=256):
    B, L, _ = proj.shape
    W = ABC_BLOCKS * HEAD_DIM
    tab = pl.BlockSpec((t, HEAD_DIM), lambda b, i: (i, 0))
    par = pl.BlockSpec((2, HEAD_DIM), lambda b, i: (0, 0))
    return pl.pallas_call(
        _qk_prep_kernel,
        out_shape=jax.ShapeDtypeStruct((B, L, W), BF16),
        grid=(B, L // t),
        in_specs=[pl.BlockSpec((1, t, W), lambda b, i: (b, i, 0)), tab, tab, tab, tab, par, par, par],
        out_specs=pl.BlockSpec((1, t, W), lambda b, i: (b, i, 0)),
        compiler_params=_cparams(("arbitrary", "arbitrary")),
        name="qk_prep",
    )(proj, cosa, sina, cosc, sinc, ga, gb, gc)


def _attn_a_kernel(sink_ref, q_ref, k_ref, v_ref, kc_ref, vc_ref, o_ref, *, tq):
    L = k_ref.shape[1]
    kv, i = pl.program_id(1), pl.program_id(2)
    win = tq + 2 * A_WINDOW
    start = pl.multiple_of(jnp.clip(i * tq - A_WINDOW, 0, L - win), A_WINDOW)
    k = k_ref[0, pl.ds(start, win), :]
    v = v_ref[0, pl.ds(start, win), :]
    kc, vc = kc_ref[0], vc_ref[0]
    qpos = i * tq + lax.broadcasted_iota(jnp.int32, (tq, win), 0)
    kpos = start + lax.broadcasted_iota(jnp.int32, (tq, win), 1)
    in_window = jnp.abs(kpos - qpos) <= A_WINDOW
    group = A_HEADS // A_KV_HEADS
    for g in range(group):
        sl = slice(g * HEAD_DIM, (g + 1) * HEAD_DIM)
        q = q_ref[0, :, sl]
        s = jnp.where(in_window, _dot_nt(q, k), NEG_INF)
        sc = _dot_nt(q, kc)
        sink = sink_ref[kv * group + g]
        m = jnp.maximum(jnp.maximum(jnp.max(s, axis=-1, keepdims=True), jnp.max(sc, axis=-1, keepdims=True)), sink)
        p = jnp.exp(s - m)
        pc = jnp.exp(sc - m)
        denom = jnp.sum(p, axis=-1, keepdims=True) + jnp.sum(pc, axis=-1, keepdims=True) + jnp.exp(sink - m)
        o = _dot(p.astype(BF16), v) + _dot(pc.astype(BF16), vc)
        o_ref[0, :, sl] = (o / denom).astype(o_ref.dtype)


def attn_a(qk, qkc, sink, tq=512):
    B, L, _ = qk.shape
    C = qkc.shape[1]
    gw = (A_HEADS // A_KV_HEADS) * HEAD_DIM
    return pl.pallas_call(
        functools.partial(_attn_a_kernel, tq=tq),
        out_shape=jax.ShapeDtypeStruct((B, L, A_HEADS * HEAD_DIM), BF16),
        grid=(B, A_KV_HEADS, L // tq),
        in_specs=[
            pl.BlockSpec(memory_space=pltpu.SMEM),
            pl.BlockSpec((1, tq, gw), lambda b, kv, i: (b, i, kv)),
            pl.BlockSpec((1, L, HEAD_DIM), lambda b, kv, i: (b, 0, A_K + kv)),
            pl.BlockSpec((1, L, HEAD_DIM), lambda b, kv, i: (b, 0, A_V + kv)),
            pl.BlockSpec((1, C, HEAD_DIM), lambda b, kv, i: (b, 0, A_K + kv)),
            pl.BlockSpec((1, C, HEAD_DIM), lambda b, kv, i: (b, 0, A_V + kv)),
        ],
        out_specs=pl.BlockSpec((1, tq, gw), lambda b, kv, i: (b, i, kv)),
        compiler_params=_cparams(("arbitrary", "arbitrary", "arbitrary")),
        name="attn_a",
    )(sink, qk, qk, qk, qkc, qkc)


NA_QROWS = 8
NA_SPLIT = 2


def _na_bias_tables(rpb, R):
    W = GRID_W
    H = rpb.shape[0]
    wq = jnp.arange(W)
    cs = jnp.clip(wq - NA_COLS // 2, 0, W - NA_COLS)
    colmask = (wq[None, :] >= cs[:, None]) & (wq[None, :] < cs[:, None] + NA_COLS)
    rel_c = jnp.clip(wq[None, :] - wq[:, None] + (NA_COLS - 1), 0, 2 * NA_COLS - 2)
    per_row = jnp.where(colmask[None, None], rpb.astype(F32)[:, :, rel_c], NEG_INF)
    per_row = jnp.transpose(per_row, (0, 2, 1, 3))
    masked = jnp.full((H, W, 1, W), NEG_INF, F32)
    nq, nk = NA_QROWS, 2 * NA_QROWS
    tables = []
    for r0, ks in ((0, 0), (nq, nq - NA_ROWS // 2), (R - nq, R - nk)):
        rows = []
        for a in range(nq):
            qr = r0 + a
            rs = min(max(qr - NA_ROWS // 2, 0), R - NA_ROWS)
            blocks = []
            for c in range(nk):
                kr = ks + c
                if rs <= kr < rs + NA_ROWS:
                    d = kr - qr + (NA_ROWS - 1)
                    blocks.append(per_row[:, :, d:d + 1, :])
                else:
                    blocks.append(masked)
            rows.append(jnp.concatenate(blocks, axis=2))
        tables.append(jnp.stack(rows, axis=1).reshape(H, nq * W, nk * W))
    return jnp.stack(tables)


def _attn_b_kernel(q_ref, k_ref, v_ref, kc_ref, vc_ref, bias_ref, o_ref):
    L = k_ref.shape[1]
    R = L // GRID_W
    i = pl.program_id(2)
    nk = 2 * NA_QROWS
    krow = jnp.clip(i * NA_QROWS - NA_ROWS // 2, 0, R - nk)
    start = pl.multiple_of(krow * GRID_W, (NA_ROWS // 2) * GRID_W)
    k = k_ref[0, pl.ds(start, nk * GRID_W), :]
    v = v_ref[0, pl.ds(start, nk * GRID_W), :]
    kc, vc = kc_ref[0], vc_ref[0]
    sub = q_ref.shape[1] // NA_SPLIT
    for part in range(NA_SPLIT):
        rows = slice(part * sub, (part + 1) * sub)
        q = q_ref[0, rows, :]
        s = _dot_nt(q, k) + bias_ref[0, 0, rows, :]
        sc = _dot_nt(q, kc)
        m = jnp.maximum(jnp.max(s, axis=-1, keepdims=True), jnp.max(sc, axis=-1, keepdims=True))
        p = jnp.exp(s - m)
        pc = jnp.exp(sc - m)
        denom = jnp.sum(p, axis=-1, keepdims=True) + jnp.sum(pc, axis=-1, keepdims=True)
        o = _dot(p.astype(BF16), v) + _dot(pc.astype(BF16), vc)
        o_ref[0, rows, :] = (o / denom).astype(o_ref.dtype)


def attn_b(qk, qkc, bias):
    B, L, _ = qk.shape
    C = qkc.shape[1]
    tq = NA_QROWS * GRID_W
    nblk = L // tq

    def bias_idx(h, b, i):
        return (jnp.where(i == 0, 0, jnp.where(i == nblk - 1, 2, 1)), h, 0, 0)

    return pl.pallas_call(
        _attn_b_kernel,
        out_shape=jax.ShapeDtypeStruct((B, L, B_HEADS * HEAD_DIM), BF16),
        grid=(B_HEADS, B, nblk),
        in_specs=[
            pl.BlockSpec((1, tq, HEAD_DIM), lambda h, b, i: (b, i, B_Q + h)),
            pl.BlockSpec((1, L, HEAD_DIM), lambda h, b, i: (b, 0, B_K + h)),
            pl.BlockSpec((1, L, HEAD_DIM), lambda h, b, i: (b, 0, B_V + h)),
            pl.BlockSpec((1, C, HEAD_DIM), lambda h, b, i: (b, 0, B_K + h)),
            pl.BlockSpec((1, C, HEAD_DIM), lambda h, b, i: (b, 0, B_V + h)),
            pl.BlockSpec((1, 1, tq, 2 * tq), bias_idx),
        ],
        out_specs=pl.BlockSpec((1, tq, HEAD_DIM), lambda h, b, i: (b, i, h)),
        compiler_params=_cparams(("arbitrary", "arbitrary", "arbitrary")),
        name="attn_b",
    )(qk, qk, qk, qkc, qkc, bias)


def _diff_lambda(lambda_params, lam_init):
    lp = lambda_params.astype(F32)
    return jnp.exp(jnp.sum(lp[0] * lp[1])) - jnp.exp(jnp.sum(lp[2] * lp[3])) + lam_init


ATTN_C_SAFE_SCORE_BOUND = 40.0


def _attn_c_kernel(lam_ref, q_ref, k_ref, v_ref, kc_ref, vc_ref, g_ref, o_ref, vt_ref, vct_ref, *scratch,
                   tk, coef, online):
    if online:
        m_ref, l_ref, acc_ref = scratch
        m_ref[...] = jnp.full(m_ref.shape, NEG_INF, F32)
    else:
        l_ref, acc_ref = scratch
    nk = vt_ref.shape[0]

    @pl.when(pl.program_id(2) == 0)
    def _():
        for j in range(nk):
            vt_ref[j] = v_ref[0, j * tk:(j + 1) * tk, :].astype(F32).T.astype(BF16)
        vct_ref[...] = vc_ref[0].astype(F32).T.astype(BF16)

    q = q_ref[0]
    lane = lax.broadcasted_iota(jnp.int32, q.shape, 1)
    zero = jnp.zeros_like(q)
    qs = (jnp.where(lane < C_SUB_DIM, q, zero), jnp.where(lane < C_SUB_DIM, zero, q))

    l_ref[...] = jnp.zeros(l_ref.shape, F32)
    acc_ref[...] = jnp.zeros(acc_ref.shape, F32)

    def accumulate(kblk, vtblk):
        for s_idx in range(2):
            s = _dot_nt(kblk, qs[s_idx])
            if online:
                m_old = m_ref[s_idx]
                m_new = jnp.maximum(m_old, jnp.max(s, axis=0, keepdims=True))
                alpha = jnp.exp2(m_old - m_new)
                p = jnp.exp2(s - m_new)
                l_ref[s_idx] = alpha * l_ref[s_idx] + jnp.sum(p, axis=0, keepdims=True)
                acc_ref[s_idx] = alpha * acc_ref[s_idx] + _dot(vtblk, p.astype(BF16))
                m_ref[s_idx] = m_new
            else:
                p = jnp.exp2(s)
                l_ref[s_idx] += jnp.sum(p, axis=0, keepdims=True)
                acc_ref[s_idx] += _dot(vtblk, p.astype(BF16))

    def body(j, carry):
        accumulate(k_ref[0, pl.ds(pl.multiple_of(j * tk, tk), tk), :], vt_ref[j])
        return carry

    lax.fori_loop(0, nk, body, 0, unroll=4)
    accumulate(kc_ref[0], vct_ref[...])

    o = acc_ref[0] / l_ref[0] - lam_ref[0] * (acc_ref[1] / l_ref[1])
    y = o * lax.rsqrt(jnp.mean(o * o, axis=0, keepdims=True) + EPS) * g_ref[...] * coef
    o_ref[0] = y.T.astype(o_ref.dtype)


def attn_c(qk, qkc, lam, subln, coef, online, tq=2048, tk=512):
    B, L, _ = qk.shape
    C = qkc.shape[1]
    H, dv = C_HEADS, HEAD_DIM
    stats = [pltpu.VMEM((2, 1, tq), F32)] * (2 if online else 1)
    return pl.pallas_call(
        functools.partial(_attn_c_kernel, tk=tk, coef=coef, online=online),
        out_shape=jax.ShapeDtypeStruct((B, L, H * dv), BF16),
        grid=(B, H, L // tq),
        in_specs=[
            pl.BlockSpec(memory_space=pltpu.SMEM),
            pl.BlockSpec((1, tq, HEAD_DIM), lambda b, h, i: (b, i, C_Q + h)),
            pl.BlockSpec((1, L, HEAD_DIM), lambda b, h, i: (b, 0, C_K + h)),
            pl.BlockSpec((1, L, HEAD_DIM), lambda b, h, i: (b, 0, C_V + h)),
            pl.BlockSpec((1, C, HEAD_DIM), lambda b, h, i: (b, 0, C_K + h)),
            pl.BlockSpec((1, C, HEAD_DIM), lambda b, h, i: (b, 0, C_V + h)),
            pl.BlockSpec((dv, 1), lambda b, h, i: (0, 0)),
        ],
        out_specs=pl.BlockSpec((1, tq, dv), lambda b, h, i: (b, i, h)),
        scratch_shapes=[pltpu.VMEM((L // tk, dv, tk), BF16), pltpu.VMEM((dv, C), BF16)]
        + stats + [pltpu.VMEM((2, dv, tq), F32)],
        compiler_params=_cparams(("arbitrary", "arbitrary", "arbitrary")),
        name="attn_c_online" if online else "attn_c",
    )(lam, qk, qk, qk, qkc, qkc, subln.reshape(dv, 1).astype(F32))


def _retention_kernel(ld_ref, q_ref, k_ref, v_ref, g_ref, qc_ref, kc_ref, vc_ref, gc_ref, *rest, with_ctx):
    if with_ctx:
        y_ref, yc_ref, of_ref, ob_ref, ocf_ref = rest
    else:
        y_ref, of_ref, ob_ref = rest
        yc_ref = ocf_ref = None
    L, C = q_ref.shape[1], qc_ref.shape[1]
    c = min(RET_CHUNK, L)
    cc = min(RET_CHUNK, C)
    h = pl.program_id(1)
    scale = HEAD_DIM ** -0.5

    def decays(lg, reverse, c):
        row = lax.broadcasted_iota(jnp.int32, (c, c), 0).astype(F32)
        col = lax.broadcasted_iota(jnp.int32, (c, c), 1).astype(F32)
        rowp = lax.broadcasted_iota(jnp.int32, (c, HEAD_DIM), 0).astype(F32)
        if reverse:
            rel, q_exp, k_exp = col - row, c - rowp, rowp
        else:
            rel, q_exp, k_exp = row - col, rowp + 1.0, (c - 1.0) - rowp
        intra = jnp.where(rel >= 0, jnp.exp(lg * jnp.maximum(rel, 0.0)), 0.0)
        return intra, jnp.exp(lg * q_exp), jnp.exp(lg * k_exp), jnp.exp(lg * c)

    def step(S, q, k, v, dec, want_out):
        intra, q_decay, k_decay, chunk_decay = dec
        o = None
        if want_out:
            qs = (q.astype(F32) * scale).astype(BF16)
            s = _dot_nt(qs, k) * intra
            o = _dot(s.astype(BF16), v) + _dot(qs, S.astype(BF16)) * q_decay
        kd_t = (k.astype(F32) * k_decay).T.astype(BF16)
        return S * chunk_decay + _dot(kd_t, v), o

    def gated(o, g):
        g = g.astype(F32)
        y = o * lax.rsqrt(jnp.mean(o * o, axis=-1, keepdims=True) + EPS)
        return y * (g * jax.nn.sigmoid(g))

    def lat(ref, j):
        return ref[0, pl.ds(pl.multiple_of(j * c, c), c), :]

    def ctx(ref, j):
        return ref[0, j * cc:(j + 1) * cc, :]

    dec_f = decays(ld_ref[0, h], False, cc)
    dec_b = decays(ld_ref[1, h], True, cc)
    S_f = jnp.zeros((HEAD_DIM, HEAD_DIM), F32)
    for j in range(C // cc):
        S_f, o = step(S_f, ctx(qc_ref, j), ctx(kc_ref, j), ctx(vc_ref, j), dec_f, with_ctx)
        if with_ctx:
            ocf_ref[j * cc:(j + 1) * cc, :] = o
    S_b = jnp.zeros((HEAD_DIM, HEAD_DIM), F32)
    for j in reversed(range(C // cc)):
        S_b, o = step(S_b, ctx(qc_ref, j), ctx(kc_ref, j), ctx(vc_ref, j), dec_b, with_ctx)
        if with_ctx:
            yc_ref[0, j * cc:(j + 1) * cc, :] = gated(o + ocf_ref[j * cc:(j + 1) * cc, :], ctx(gc_ref, j)).astype(yc_ref.dtype)
    if c != cc:
        dec_f = decays(ld_ref[0, h], False, c)
        dec_b = decays(ld_ref[1, h], True, c)

    n = L // c

    def both(t, carry):
        S_f, S_b = carry
        S_f, o_f = step(S_f, lat(q_ref, t), lat(k_ref, t), lat(v_ref, t), dec_f, True)
        of_ref[pl.ds(pl.multiple_of(t * c, c), c), :] = o_f
        j = n - 1 - t
        S_b, o_b = step(S_b, lat(q_ref, j), lat(k_ref, j), lat(v_ref, j), dec_b, True)
        ob_ref[pl.ds(pl.multiple_of(j * c, c), c), :] = o_b
        return S_f, S_b

    lax.fori_loop(0, n, both, (S_f, S_b), unroll=2)

    def finish(j, carry):
        sl = pl.ds(pl.multiple_of(j * c, c), c)
        y_ref[0, sl, :] = gated(of_ref[sl, :] + ob_ref[sl, :], lat(g_ref, j)).astype(y_ref.dtype)
        return carry

    lax.fori_loop(0, n, finish, 0, unroll=2)


def retention(proj, projc, log_decay, with_ctx):
    B, L, _ = proj.shape
    C = projc.shape[1]

    def lat(blk):
        return pl.BlockSpec((1, L, HEAD_DIM), lambda b, h: (b, 0, blk + h))

    def ctx(blk):
        return pl.BlockSpec((1, C, HEAD_DIM), lambda b, h: (b, 0, blk + h))

    out_shape = [jax.ShapeDtypeStruct((B, L, D_HEADS * HEAD_DIM), BF16)]
    out_specs = [pl.BlockSpec((1, L, HEAD_DIM), lambda b, h: (b, 0, h))]
    scratch = [pltpu.VMEM((L, HEAD_DIM), F32), pltpu.VMEM((L, HEAD_DIM), F32)]
    if with_ctx:
        out_shape.append(jax.ShapeDtypeStruct((B, C, D_HEADS * HEAD_DIM), BF16))
        out_specs.append(pl.BlockSpec((1, C, HEAD_DIM), lambda b, h: (b, 0, h)))
        scratch.append(pltpu.VMEM((C, HEAD_DIM), F32))
    outs = pl.pallas_call(
        functools.partial(_retention_kernel, with_ctx=with_ctx),
        out_shape=out_shape,
        grid=(B, D_HEADS),
        in_specs=[pl.BlockSpec(memory_space=pltpu.SMEM),
                  lat(D_Q), lat(D_K), lat(D_V), lat(D_G), ctx(D_Q), ctx(D_K), ctx(D_V), ctx(D_G)],
        out_specs=out_specs,
        scratch_shapes=scratch,
        compiler_params=_cparams(("arbitrary", "arbitrary")),
        name="retention_ctx" if with_ctx else "retention",
    )(log_decay.astype(F32), proj, proj, proj, proj, projc, projc, projc, projc)
    return (outs[0], outs[1]) if with_ctx else (outs[0], None)


def _ctx_attn_kernel(sink_ref, lam_ref, qk_ref, g_ref, o_ref, *, coef):
    def blk(idx):
        return qk_ref[0, :, idx * HEAD_DIM:(idx + 1) * HEAD_DIM]

    def softmax_out(q, k, v, sink=None, exp=jnp.exp):
        s = _dot_nt(q, k)
        m = jnp.max(s, axis=-1, keepdims=True)
        if sink is not None:
            m = jnp.maximum(m, sink)
        p = exp(s - m)
        denom = jnp.sum(p, axis=-1, keepdims=True)
        if sink is not None:
            denom = denom + jnp.exp(sink - m)
        return _dot(p.astype(BF16), v) / denom

    group = A_HEADS // A_KV_HEADS
    for h in range(A_HEADS):
        o = softmax_out(blk(A_Q + h), blk(A_K + h // group), blk(A_V + h // group), sink_ref[h])
        o_ref[0, :, h * HEAD_DIM:(h + 1) * HEAD_DIM] = o.astype(o_ref.dtype)
    for h in range(B_HEADS):
        o = softmax_out(blk(B_Q + h), blk(B_K + h), blk(B_V + h))
        o_ref[0, :, (A_HEADS + h) * HEAD_DIM:(A_HEADS + h + 1) * HEAD_DIM] = o.astype(o_ref.dtype)
    lane = lax.broadcasted_iota(jnp.int32, (qk_ref.shape[1], HEAD_DIM), 1)
    for h in range(C_HEADS):
        q, k, v = blk(C_Q + h), blk(C_K + h), blk(C_V + h)
        zero = jnp.zeros_like(q)
        o1 = softmax_out(jnp.where(lane < C_SUB_DIM, q, zero), k, v, exp=jnp.exp2)
        o2 = softmax_out(jnp.where(lane < C_SUB_DIM, zero, q), k, v, exp=jnp.exp2)
        o = o1 - lam_ref[0] * o2
        y = o * lax.rsqrt(jnp.mean(o * o, axis=-1, keepdims=True) + EPS) * g_ref[...] * coef
        base = A_HEADS + B_HEADS + h
        o_ref[0, :, base * HEAD_DIM:(base + 1) * HEAD_DIM] = y.astype(o_ref.dtype)


def ctx_attn(qkc, sink, lam, subln, coef):
    B, C, W = qkc.shape
    n_out = (A_HEADS + B_HEADS + C_HEADS) * HEAD_DIM
    return pl.pallas_call(
        functools.partial(_ctx_attn_kernel, coef=coef),
        out_shape=jax.ShapeDtypeStruct((B, C, n_out), BF16),
        grid=(B,),
        in_specs=[
            pl.BlockSpec(memory_space=pltpu.SMEM),
            pl.BlockSpec(memory_space=pltpu.SMEM),
            pl.BlockSpec((1, C, W), lambda b: (b, 0, 0)),
            pl.BlockSpec((1, HEAD_DIM), lambda b: (0, 0)),
        ],
        out_specs=pl.BlockSpec((1, C, n_out), lambda b: (b, 0, 0)),
        compiler_params=_cparams(("arbitrary",)),
        name="ctx_attn",
    )(sink, lam, qkc, subln.reshape(1, HEAD_DIM).astype(F32))


def _out_proj_kernel(*refs, n_parts):
    parts = refs[:n_parts]
    x_ref, gate_ref, w_ref, o_ref = refs[n_parts:]
    mix = jnp.concatenate([p[0] for p in parts], axis=-1)
    o_ref[0] = x_ref[0] + gate_ref[0] * _dot(mix, w_ref[0])


def out_proj(parts, x, gate, w, layer, tm, tn=512):
    B, L, D = x.shape
    part_specs = [pl.BlockSpec((1, tm, p.shape[2]), lambda b, i, j: (b, i, 0)) for p in parts]
    return pl.pallas_call(
        functools.partial(_out_proj_kernel, n_parts=len(parts)),
        out_shape=jax.ShapeDtypeStruct((B, L, D), F32),
        grid=(B, L // tm, D // tn),
        in_specs=part_specs + [
            pl.BlockSpec((1, tm, tn), lambda b, i, j: (b, i, j)),
            pl.BlockSpec((1, 1, tn), lambda b, i, j: (b, 0, j)),
            pl.BlockSpec((1, w.shape[1], tn), lambda b, i, j: (layer, 0, j)),
        ],
        out_specs=pl.BlockSpec((1, tm, tn), lambda b, i, j: (b, i, j)),
        compiler_params=_cparams(("arbitrary", "arbitrary", "arbitrary")),
        name="out_proj",
    )(*parts, x, gate, w)


def _ffn_prep_kernel(*refs, n_lat_tiles):
    if n_lat_tiles is None:
        x_ref, gain_ref, shift_ref, scale_ref, rh_ref, rl_ref, h_ref, id_ref, w_ref = refs
        xc_ref = None
    else:
        x_ref, xc_ref, gain_ref, shift_ref, scale_ref, rh_ref, rl_ref, h_ref, id_ref, w_ref = refs

    def body(src_ref):
        h = _modulated(src_ref[...], gain_ref[...], shift_ref[0], scale_ref[0])
        h_ref[...] = _pack_bf16_pairs(h)

        h_hi, h_lo = _split_bf16(h)
        logits = _dot(h_hi, rh_ref[...]) + _dot(h_lo, rh_ref[...]) + _dot(h_hi, rl_ref[...])
        lane = lax.broadcasted_iota(jnp.int32, logits.shape, 1)
        big = jnp.int32(logits.shape[1])

        def first_argmax(vals):
            top = jnp.max(vals, axis=-1, keepdims=True)
            return top, jnp.min(jnp.where(vals == top, lane, big), axis=-1, keepdims=True)

        is_group = lane < N_GROUPS
        g_top, g_idx = first_argmax(jnp.where(is_group, logits, NEG_INF))
        g_val = 1.0 / jnp.sum(jnp.where(is_group, jnp.exp(logits - g_top), 0.0), axis=-1, keepdims=True)
        lo = N_GROUPS + EXPERTS_PER_GROUP * g_idx
        e_logits = jnp.where((lane >= lo) & (lane < lo + EXPERTS_PER_GROUP), logits, NEG_INF)
        v1, i1 = first_argmax(e_logits)
        v2, i2 = first_argmax(jnp.where(lane == i1, NEG_INF, e_logits))
        e21 = jnp.exp(v2 - v1)
        w1 = g_val / (1.0 + e21)
        w2 = g_val * e21 / (1.0 + e21)
        id_ref[...] = jnp.where(lane == 0, i1 - N_GROUPS, jnp.where(lane == 1, i2 - N_GROUPS, 0))
        w_ref[...] = jnp.where(lane == 0, w1, jnp.where(lane == 1, w2, 0.0))

    if xc_ref is None:
        body(x_ref)
    else:
        pl.when(pl.program_id(0) < n_lat_tiles)(lambda: body(x_ref))
        pl.when(pl.program_id(0) >= n_lat_tiles)(lambda: body(xc_ref))


def ffn_prep(x, xc, gain, shift, scale, r_cat, t=256):
    B, L, D = x.shape
    n_lat = B * L // t
    lat_per_seg = L // t
    if xc is None:
        T = B * L
        xs = [x.reshape(B * L, D)]
        x_specs = [pl.BlockSpec((t, D), lambda i: (i, 0))]
        seg = lambda i: (i // lat_per_seg, 0, 0)
        n_lat_tiles = None
    else:
        C = xc.shape[1]
        T = B * (L + C)
        xs = [x.reshape(B * L, D), xc.reshape(B * C, D)]
        x_specs = [pl.BlockSpec((t, D), lambda i: (jnp.minimum(i, n_lat - 1), 0)),
                   pl.BlockSpec((t, D), lambda i: (jnp.maximum(i - n_lat, 0), 0))]
        seg = lambda i: (jnp.minimum(i // lat_per_seg, B), 0, 0)
        n_lat_tiles = n_lat
    return pl.pallas_call(
        functools.partial(_ffn_prep_kernel, n_lat_tiles=n_lat_tiles),
        out_shape=[jax.ShapeDtypeStruct((T, D // 2), jnp.uint32),
                   jax.ShapeDtypeStruct((T, 128), jnp.int32),
                   jax.ShapeDtypeStruct((T, 128), F32)],
        grid=(T // t,),
        in_specs=x_specs + [
            pl.BlockSpec((1, D), lambda i: (0, 0)),
            pl.BlockSpec((1, 1, D), seg),
            pl.BlockSpec((1, 1, D), seg),
            pl.BlockSpec((D, 128), lambda i: (0, 0)),
            pl.BlockSpec((D, 128), lambda i: (0, 0)),
        ],
        out_specs=[pl.BlockSpec((t, D // 2), lambda i: (i, 0)),
                   pl.BlockSpec((t, 128), lambda i: (i, 0)),
                   pl.BlockSpec((t, 128), lambda i: (i, 0))],
        compiler_params=_cparams(("arbitrary",)),
        name="ffn_prep",
    )(*xs, gain, shift, scale, *_split_bf16(r_cat))


def _route(ids, tm):
    A = ids.shape[0] * TOP_K
    flat_e = ids.reshape(A)
    onehot = (flat_e[:, None] == jnp.arange(N_EXPERTS, dtype=jnp.int32)[None, :]).astype(jnp.int32)
    csum = jnp.cumsum(onehot, axis=0)
    rank = jnp.sum(csum * onehot, axis=1) - 1
    counts = csum[-1]
    pcounts = (counts + tm - 1) // tm * tm
    pends = jnp.cumsum(pcounts)
    pstarts = pends - pcounts
    dest = jnp.sum(onehot * pstarts[None, :], axis=1) + rank
    NB = (A + N_EXPERTS * (tm - 1)) // tm
    row_tok = jnp.zeros((NB * tm,), jnp.int32).at[dest].set(jnp.arange(A, dtype=jnp.int32) // TOP_K)
    block_start = jnp.arange(NB, dtype=jnp.int32) * tm
    block_e = jnp.minimum(jnp.sum((pends[None, :] <= block_start[:, None]).astype(jnp.int32), axis=1), N_EXPERTS - 1)
    return dest.astype(jnp.int32), row_tok, block_e.astype(jnp.int32), (pends[-1:] // tm).astype(jnp.int32)


def _row_copy(src, src_row, dst, dst_row, sem):
    return pltpu.make_async_copy(src.at[pl.ds(src_row, 1), :], dst.at[pl.ds(dst_row, 1), :], sem)


def _expert_kernel(be_ref, nused_ref, tok_ref, h_ref, wg_ref, wu_ref, wd_ref, o_ref, xbuf, sem):
    i = pl.program_id(0)
    tm = o_ref.shape[0]
    n_used = nused_ref[0]

    def row(r, blk, slot):
        return _row_copy(h_ref, tok_ref[blk * tm + r], xbuf.at[slot], r, sem.at[slot])

    def gather(blk, slot):
        def issue(r2, carry):
            for k in range(2):
                row(2 * r2 + k, blk, slot).start(priority=1)
            return carry
        lax.fori_loop(0, tm // 2, issue, 0, unroll=4)

    def drain(slot):
        def wait(r, carry):
            row(r, 0, slot).wait()
            return carry
        lax.fori_loop(0, tm, wait, 0, unroll=8)

    @pl.when(i == 0)
    def _():
        gather(0, 0)

    @pl.when(i + 1 < n_used)
    def _():
        gather(i + 1, (i + 1) % 2)

    @pl.when(i < n_used)
    def _():
        slot = i % 2
        drain(slot)
        x = _unpack_bf16_pairs(xbuf[slot]).astype(BF16)
        a = _dot(x, wg_ref[0, 0])
        u = _dot(x, wu_ref[0, 0])
        hmid = (a * jax.nn.sigmoid(a) * u).astype(BF16)
        o_ref[...] = _pack_bf16_pairs(_dot(hmid, wd_ref[0, 0]))

    @pl.when(i >= n_used)
    def _():
        o_ref[...] = jnp.zeros(o_ref.shape, o_ref.dtype)


def experts(block_e, n_used, row_tok, h, w_gate, w_up, w_down, layer, tm):
    P = row_tok.shape[0]
    _, _, D, De = w_gate.shape
    w_idx = lambda i, be, nu, tok: (layer, be[jnp.minimum(i, nu[0] - 1)], 0, 0)
    return pl.pallas_call(
        _expert_kernel,
        out_shape=jax.ShapeDtypeStruct((P, D // 2), jnp.uint32),
        grid_spec=pltpu.PrefetchScalarGridSpec(
            num_scalar_prefetch=3,
            grid=(P // tm,),
            in_specs=[
                pl.BlockSpec(memory_space=pl.ANY),
                pl.BlockSpec((1, 1, D, De), w_idx),
                pl.BlockSpec((1, 1, D, De), w_idx),
                pl.BlockSpec((1, 1, De, D), w_idx),
            ],
            out_specs=pl.BlockSpec((tm, D // 2), lambda i, be, nu, tok: (i, 0)),
            scratch_shapes=[pltpu.VMEM((2, tm, D // 2), jnp.uint32), pltpu.SemaphoreType.DMA((2,))],
        ),
        compiler_params=_cparams(("arbitrary",)),
        name="moe_experts",
    )(block_e, n_used, row_tok, h, w_gate, w_up, w_down)


def _combine_kernel(dest_ref, x_ref, gate_ref, w_ref, y_ref, o_ref, buf, sem, *, tok_offset):
    t, D = x_ref.shape
    i = pl.program_id(0)
    slot = i % 2

    def gather(step, slot):
        base = (tok_offset + step * t) * TOP_K

        def issue(r, carry):
            for k in range(TOP_K):
                _row_copy(y_ref, dest_ref[base + r * TOP_K + k], buf.at[slot, k], r, sem.at[slot]).start(priority=k)
            return carry
        lax.fori_loop(0, t, issue, 0, unroll=4)

    def drain(r, carry):
        _row_copy(y_ref, 0, buf.at[slot, 0], 0, sem.at[slot]).wait()
        return carry

    @pl.when(i == 0)
    def _():
        gather(0, 0)

    @pl.when(i + 1 < pl.num_programs(0))
    def _():
        gather(i + 1, 1 - slot)

    lax.fori_loop(0, t * TOP_K, drain, 0, unroll=8)
    w = w_ref[...]
    y = w[:, 0:1] * _unpack_bf16_pairs(buf[slot, 0]) + w[:, 1:2] * _unpack_bf16_pairs(buf[slot, 1])
    o_ref[...] = x_ref[...] + gate_ref[0] * y


def combine(dest, x, gate, rw, ys, tok_offset, t=256):
    B, L, D = x.shape
    per_seg = L // t
    woff = tok_offset // t
    out = pl.pallas_call(
        functools.partial(_combine_kernel, tok_offset=tok_offset),
        out_shape=jax.ShapeDtypeStruct((B * L, D), F32),
        grid_spec=pltpu.PrefetchScalarGridSpec(
            num_scalar_prefetch=1,
            grid=(B * L // t,),
            in_specs=[
                pl.BlockSpec((t, D), lambda i, d: (i, 0)),
                pl.BlockSpec((1, 1, D), lambda i, d: (i // per_seg, 0, 0)),
                pl.BlockSpec((t, 128), lambda i, d: (woff + i, 0)),
                pl.BlockSpec(memory_space=pl.ANY),
            ],
            out_specs=pl.BlockSpec((t, D), lambda i, d: (i, 0)),
            scratch_shapes=[pltpu.VMEM((2, TOP_K, t, D // 2), jnp.uint32), pltpu.SemaphoreType.DMA((2,))],
        ),
        compiler_params=_cparams(("arbitrary",)),
        name="moe_combine",
    )(dest, x.reshape(B * L, D), gate, rw, ys)
    return out.reshape(B, L, D)


MOE_TM = 256


def kernel(x, c, ctx, c_ctx, w_mod, b_mod, norm_mix, norm_ffn, w_in, w_out, qk_norm_a, sink_a, qk_norm_b, rpb_b,
           qk_norm_c, lambda_c, subln_c, ret_log_decay, router_group, router_expert, w_gate, w_up, w_down):
    B, L, D = x.shape
    C = ctx.shape[1]
    depth = w_mod.shape[0]

    cvec = jnp.zeros((8, D), F32).at[:B].set(c).at[B].set(c_ctx)
    mod = mod_vectors(cvec, w_mod, b_mod).reshape(depth, 8, 6, D)

    cosa, sina = _rope_tables(L, HEAD_DIM)
    cosc, sinc = _rope_tables(L, C_SUB_DIM)
    one_tab = jnp.ones((C, HEAD_DIM), F32)
    zero_tab = jnp.zeros((C, HEAD_DIM), F32)

    w_in_b, w_out_b = w_in.astype(BF16), w_out.astype(BF16)
    w_gate_b, w_up_b, w_down_b = w_gate.astype(BF16), w_up.astype(BF16), w_down.astype(BF16)

    xc = ctx
    for l in range(depth):
        with_ctx = l < depth - 1
        lat = lambda j: mod[l, :B, j][:, None, :]
        cx = lambda j: jnp.broadcast_to(mod[l, B, j][None, None, :], (B, 1, D))
        gain_mix = norm_mix[l][None, :]

        proj = in_proj(x, gain_mix, lat(0), lat(1), w_in_b, l, tm=512)
        projc = in_proj(xc.reshape(1, B * C, D), gain_mix, cx(0)[:1], cx(1)[:1], w_in_b, l,
                        tm=B * C).reshape(B, C, IN_COLS)
        gain_c = jnp.tile(qk_norm_c[l], (1, HEAD_DIM // C_SUB_DIM))
        qk = qk_prep(proj, cosa, sina, cosc, sinc, qk_norm_a[l], qk_norm_b[l], gain_c)
        qkc = qk_prep(projc, one_tab, zero_tab, one_tab, zero_tab, qk_norm_a[l], qk_norm_b[l], gain_c)

        lam_init = 0.8 - 0.6 * math.exp(-0.3 * l)
        lam = _diff_lambda(lambda_c[l], lam_init).reshape(1)
        out_a = attn_a(qk, qkc, sink_a[l])
        out_b = attn_b(qk, qkc, _na_bias_tables(rpb_b[l], L // GRID_W))
        score_bound = (C_SUB_DIM ** 0.5) * jnp.max(jnp.abs(qk_norm_c[l, 0])) * jnp.max(jnp.abs(qk_norm_c[l, 1]))
        attn_c_l = functools.partial(attn_c, qk, qkc, lam, subln_c[l], 1.0 - lam_init)
        out_c = lax.cond(score_bound <= ATTN_C_SAFE_SCORE_BOUND,
                         lambda: attn_c_l(online=False), lambda: attn_c_l(online=True))
        out_d, out_dc = retention(proj, projc, ret_log_decay[l], with_ctx)

        x = out_proj([out_a, out_b, out_c, out_d], x, lat(2), w_out_b, l, tm=1024)
        if with_ctx:
            out_abc_c = ctx_attn(qkc, sink_a[l], lam, subln_c[l], 1.0 - lam_init)
            flat = lambda t: t.reshape(1, B * C, t.shape[-1])
            xc = out_proj([flat(out_abc_c), flat(out_dc)], flat(xc), cx(2)[:1], w_out_b, l,
                          tm=B * C).reshape(B, C, D)

        r_cat = jnp.zeros((D, 128), F32).at[:, :N_GROUPS].set(router_group[l])
        r_cat = r_cat.at[:, N_GROUPS:N_GROUPS + N_EXPERTS].set(router_expert[l])
        if with_ctx:
            shift = jnp.concatenate([lat(3), cx(3)[:1]], axis=0)
            scale = jnp.concatenate([lat(4), cx(4)[:1]], axis=0)
        else:
            shift, scale = lat(3), lat(4)
        h2, rid, rw = ffn_prep(x, xc if with_ctx else None, norm_ffn[l][None, :], shift, scale, r_cat)
        dest, row_tok, block_e, n_used = _route(rid[:, :TOP_K], MOE_TM)
        ys = experts(block_e, n_used, row_tok, h2, w_gate_b, w_up_b, w_down_b, l, MOE_TM)
        x = combine(dest, x, lat(5), rw, ys, 0)
        if with_ctx:
            xc = combine(dest, xc, cx(5), rw, ys, B * L)
    return x
```

```python
import functools
import math

import jax
import jax.numpy as jnp
from jax import lax
from jax.experimental import pallas as pl
from jax.experimental.pallas import tpu as pltpu

F32 = jnp.float32
BF16 = jnp.bfloat16

HEAD_DIM = 128
GRID_W = 64
A_HEADS, A_KV_HEADS, A_WINDOW = 8, 2, 128
B_HEADS, NA_ROWS, NA_COLS = 8, 8, 16
C_HEADS, C_SUB_DIM = 8, 64
D_HEADS = 8
RET_CHUNK = 256
ROPE_BASE = 10000.0
N_GROUPS, EXPERTS_PER_GROUP, TOP_K = 4, 8, 2
N_EXPERTS = N_GROUPS * EXPERTS_PER_GROUP
EPS = 1e-6
NEG_INF = -1e30

A_Q, A_K, A_V = 0, 8, 10
B_Q, B_K, B_V = 12, 20, 28
C_Q, C_K, C_V = 36, 44, 52
D_Q, D_K, D_V, D_G = 60, 68, 76, 84
ABC_BLOCKS = 60
IN_COLS = 92 * 128

VMEM_LIMIT = 56 * 1024 * 1024


def _cparams(sem, vmem=VMEM_LIMIT):
    return pltpu.CompilerParams(dimension_semantics=sem, vmem_limit_bytes=vmem)


def _dot(a, b):
    return jnp.dot(a, b, preferred_element_type=F32)


def _dot_nt(a, b):
    return lax.dot_general(a, b, (((1,), (1,)), ((), ())), preferred_element_type=F32)


def _split_bf16(x):
    hi = x.astype(BF16)
    lo = (x - hi.astype(F32)).astype(BF16)
    return hi, lo


def _pack_bf16_pairs(x):
    half = x.shape[1] // 2
    bits = lax.bitcast_convert_type(x.astype(BF16).astype(F32), jnp.uint32)
    return (bits[:, :half] >> 16) | (bits[:, half:] & jnp.uint32(0xFFFF0000))


def _unpack_bf16_pairs(p):
    lo = lax.bitcast_convert_type(p << 16, F32)
    hi = lax.bitcast_convert_type(p & jnp.uint32(0xFFFF0000), F32)
    return jnp.concatenate([lo, hi], axis=-1)


def _dot3(x, w):
    xh, xl = _split_bf16(x)
    wh, wl = _split_bf16(w)
    return _dot(xh, wh) + _dot(xl, wh) + _dot(xh, wl)


def _mod_kernel(c_ref, w_ref, b_ref, o_ref):
    c = c_ref[...]
    x = c * jax.nn.sigmoid(c)
    o_ref[0] = _dot3(x, w_ref[0]) + b_ref[0]


def mod_vectors(cvec, w_mod, b_mod, tn=512):
    depth, D, N = w_mod.shape
    return pl.pallas_call(
        _mod_kernel,
        out_shape=jax.ShapeDtypeStruct((depth, 8, N), F32),
        grid=(depth, N // tn),
        in_specs=[
            pl.BlockSpec((8, D), lambda l, j: (0, 0)),
            pl.BlockSpec((1, D, tn), lambda l, j: (l, 0, j)),
            pl.BlockSpec((1, 1, tn), lambda l, j: (l, 0, j)),
        ],
        out_specs=pl.BlockSpec((1, 8, tn), lambda l, j: (l, 0, j)),
        compiler_params=_cparams(("arbitrary", "arbitrary")),
        name="mod_vectors",
    )(cvec, w_mod, b_mod.reshape(depth, 1, N))


def _modulated(x, gain, shift, scale):
    ms = jnp.mean(x * x, axis=-1, keepdims=True)
    y = x * lax.rsqrt(ms + EPS) * gain
    return y * (1.0 + scale) + shift


def _in_proj_kernel(x_ref, gain_ref, shift_ref, scale_ref, w_ref, o_ref, h_ref):
    @pl.when(pl.program_id(2) == 0)
    def _():
        h_ref[...] = _modulated(x_ref[0], gain_ref[...], shift_ref[0], scale_ref[0]).astype(BF16)

    o_ref[0] = _dot(h_ref[...], w_ref[0]).astype(o_ref.dtype)


def in_proj(x, gain, shift, scale, w, layer, tm, tn=512):
    B, L, D = x.shape
    N = w.shape[2]
    return pl.pallas_call(
        _in_proj_kernel,
        out_shape=jax.ShapeDtypeStruct((B, L, N), BF16),
        grid=(B, L // tm, N // tn),
        in_specs=[
            pl.BlockSpec((1, tm, D), lambda b, i, j: (b, i, 0)),
            pl.BlockSpec((1, D), lambda b, i, j: (0, 0)),
            pl.BlockSpec((1, 1, D), lambda b, i, j: (b, 0, 0)),
            pl.BlockSpec((1, 1, D), lambda b, i, j: (b, 0, 0)),
            pl.BlockSpec((1, D, tn), lambda b, i, j: (layer, 0, j)),
        ],
        out_specs=pl.BlockSpec((1, tm, tn), lambda b, i, j: (b, i, j)),
        scratch_shapes=[pltpu.VMEM((tm, D), BF16)],
        compiler_params=_cparams(("arbitrary", "arbitrary", "arbitrary")),
        name="in_proj",
    )(x, gain, shift, scale, w)


def _rope_tables(L, dim):
    t = jnp.arange(L)
    row = (t // GRID_W).astype(F32)
    col = (t % GRID_W).astype(F32)
    quarter = dim // 4
    inv = ROPE_BASE ** (-jnp.arange(quarter, dtype=F32) / quarter)
    ar = row[:, None] * inv[None, :]
    ac = col[:, None] * inv[None, :]
    ang = jnp.concatenate([ar, ar, ac, ac], axis=-1)
    sign = jnp.where((jnp.arange(dim) % (dim // 2)) < quarter, -1.0, 1.0).astype(F32)
    reps = HEAD_DIM // dim
    return jnp.tile(jnp.cos(ang), (1, reps)), jnp.tile(jnp.sin(ang) * sign[None, :], (1, reps))


def _qk_prep_kernel(p_ref, cosa_ref, sina_ref, cosc_ref, sinc_ref, ga_ref, gb_ref, gc_ref, o_ref):
    t = p_ref.shape[1]
    lane = lax.broadcasted_iota(jnp.int32, (t, HEAD_DIM), 1)

    ri = lax.broadcasted_iota(jnp.int32, (HEAD_DIM, HEAD_DIM), 0)
    ci = lax.broadcasted_iota(jnp.int32, (HEAD_DIM, HEAD_DIM), 1)
    ones_full = jnp.ones((HEAD_DIM, HEAD_DIM), BF16)
    ones_halves = jnp.where((ri < C_SUB_DIM) == (ci < C_SUB_DIM), 1.0, 0.0).astype(BF16)

    def head_sums(x, ones):
        hi, lo = _split_bf16(x * x)
        return _dot(hi, ones) + _dot(lo, ones)

    def norm_full(x):
        return x * lax.rsqrt(head_sums(x, ones_full) * (1.0 / HEAD_DIM) + EPS)

    def norm_halves(x):
        return x * lax.rsqrt(head_sums(x, ones_halves) * (1.0 / C_SUB_DIM) + EPS)

    def rope(x, cos, sin, dim):
        quarter = dim // 4
        first = (lane % (dim // 2)) < quarter
        rot = jnp.where(first, pltpu.roll(x, HEAD_DIM - quarter, 1), pltpu.roll(x, quarter, 1))
        return x * cos + rot * sin

    cosa, sina = cosa_ref[...], sina_ref[...]
    cosc, sinc = cosc_ref[...], sinc_ref[...]
    scale_ab = HEAD_DIM ** -0.5
    scale_c = C_SUB_DIM ** -0.5 * math.log2(math.e)
    for blk in range(ABC_BLOCKS):
        sl = slice(blk * HEAD_DIM, (blk + 1) * HEAD_DIM)
        if A_V <= blk < B_Q or B_V <= blk < C_Q or C_V <= blk:
            o_ref[0, :, sl] = p_ref[0, :, sl]
            continue
        x = p_ref[0, :, sl].astype(F32)
        if blk < A_K:
            y = rope(norm_full(x) * ga_ref[0:1, :], cosa, sina, HEAD_DIM) * scale_ab
        elif blk < A_V:
            y = rope(norm_full(x) * ga_ref[1:2, :], cosa, sina, HEAD_DIM)
        elif blk < B_K:
            y = norm_full(x) * gb_ref[0:1, :] * scale_ab
        elif blk < B_V:
            y = norm_full(x) * gb_ref[1:2, :]
        elif blk < C_K:
            y = rope(norm_halves(x) * gc_ref[0:1, :], cosc, sinc, C_SUB_DIM) * scale_c
        else:
            y = rope(norm_halves(x) * gc_ref[1:2, :], cosc, sinc, C_SUB_DIM)
        o_ref[0, :, sl] = y.astype(o_ref.dtype)


def qk_prep(proj, cosa, sina, cosc, sinc, ga, gb, gc, t=256):
    B, L, _ = proj.shape
    W = ABC_BLOCKS * HEAD_DIM
    tab = pl.BlockSpec((t, HEAD_DIM), lambda b, i: (i, 0))
    par = pl.BlockSpec((2, HEAD_DIM), lambda b, i: (0, 0))
    return pl.pallas_call(
        _qk_prep_kernel,
        out_shape=jax.ShapeDtypeStruct((B, L, W), BF16),
        grid=(B, L // t),
        in_specs=[pl.BlockSpec((1, t, W), lambda b, i: (b, i, 0)), tab, tab, tab, tab, par, par, par],
        out_specs=pl.BlockSpec((1, t, W), lambda b, i: (b, i, 0)),
        compiler_params=_cparams(("arbitrary", "arbitrary")),
        name="qk_prep",
    )(proj, cosa, sina, cosc, sinc, ga, gb, gc)


def _attn_a_kernel(sink_ref, q_ref, k_ref, v_ref, kc_ref, vc_ref, o_ref, *, tq):
    L = k_ref.shape[1]
    kv, i = pl.program_id(1), pl.program_id(2)
    win = tq + 2 * A_WINDOW
    start = pl.multiple_of(jnp.clip(i * tq - A_WINDOW, 0, L - win), A_WINDOW)
    k = k_ref[0, pl.ds(start, win), :]
    v = v_ref[0, pl.ds(start, win), :]
    kc, vc = kc_ref[0], vc_ref[0]
    qpos = i * tq + lax.broadcasted_iota(jnp.int32, (tq, win), 0)
    kpos = start + lax.broadcasted_iota(jnp.int32, (tq, win), 1)
    in_window = jnp.abs(kpos - qpos) <= A_WINDOW
    group = A_HEADS // A_KV_HEADS
    for g in range(group):
        sl = slice(g * HEAD_DIM, (g + 1) * HEAD_DIM)
        q = q_ref[0, :, sl]
        s = jnp.where(in_window, _dot_nt(q, k), NEG_INF)
        sc = _dot_nt(q, kc)
        sink = sink_ref[kv * group + g]
        m = jnp.maximum(jnp.maximum(jnp.max(s, axis=-1, keepdims=True), jnp.max(sc, axis=-1, keepdims=True)), sink)
        p = jnp.exp(s - m)
        pc = jnp.exp(sc - m)
        denom = jnp.sum(p, axis=-1, keepdims=True) + jnp.sum(pc, axis=-1, keepdims=True) + jnp.exp(sink - m)
        o = _dot(p.astype(BF16), v) + _dot(pc.astype(BF16), vc)
        o_ref[0, :, sl] = (o / denom).astype(o_ref.dtype)


def attn_a(qk, qkc, sink, tq=512):
    B, L, _ = qk.shape
    C = qkc.shape[1]
    gw = (A_HEADS // A_KV_HEADS) * HEAD_DIM
    return pl.pallas_call(
        functools.partial(_attn_a_kernel, tq=tq),
        out_shape=jax.ShapeDtypeStruct((B, L, A_HEADS * HEAD_DIM), BF16),
        grid=(B, A_KV_HEADS, L // tq),
        in_specs=[
            pl.BlockSpec(memory_space=pltpu.SMEM),
            pl.BlockSpec((1, tq, gw), lambda b, kv, i: (b, i, kv)),
            pl.BlockSpec((1, L, HEAD_DIM), lambda b, kv, i: (b, 0, A_K + kv)),
            pl.BlockSpec((1, L, HEAD_DIM), lambda b, kv, i: (b, 0, A_V + kv)),
            pl.BlockSpec((1, C, HEAD_DIM), lambda b, kv, i: (b, 0, A_K + kv)),
            pl.BlockSpec((1, C, HEAD_DIM), lambda b, kv, i: (b, 0, A_V + kv)),
        ],
        out_specs=pl.BlockSpec((1, tq, gw), lambda b, kv, i: (b, i, kv)),
        compiler_params=_cparams(("arbitrary", "arbitrary", "arbitrary")),
        name="attn_a",
    )(sink, qk, qk, qk, qkc, qkc)


NA_QROWS = 8
NA_SPLIT = 2


def _na_bias_tables(rpb, R):
    W = GRID_W
    H = rpb.shape[0]
    wq = jnp.arange(W)
    cs = jnp.clip(wq - NA_COLS // 2, 0, W - NA_COLS)
    colmask = (wq[None, :] >= cs[:, None]) & (wq[None, :] < cs[:, None] + NA_COLS)
    rel_c = jnp.clip(wq[None, :] - wq[:, None] + (NA_COLS - 1), 0, 2 * NA_COLS - 2)
    per_row = jnp.where(colmask[None, None], rpb.astype(F32)[:, :, rel_c], NEG_INF)
    per_row = jnp.transpose(per_row, (0, 2, 1, 3))
    masked = jnp.full((H, W, 1, W), NEG_INF, F32)
    nq, nk = NA_QROWS, 2 * NA_QROWS
    tables = []
    for r0, ks in ((0, 0), (nq, nq - NA_ROWS // 2), (R - nq, R - nk)):
        rows = []
        for a in range(nq):
            qr = r0 + a
            rs = min(max(qr - NA_ROWS // 2, 0), R - NA_ROWS)
            blocks = []
            for c in range(nk):
                kr = ks + c
                if rs <= kr < rs + NA_ROWS:
                    d = kr - qr + (NA_ROWS - 1)
                    blocks.append(per_row[:, :, d:d + 1, :])
                else:
                    blocks.append(masked)
            rows.append(jnp.concatenate(blocks, axis=2))
        tables.append(jnp.stack(rows, axis=1).reshape(H, nq * W, nk * W))
    return jnp.stack(tables)


def _attn_b_kernel(q_ref, k_ref, v_ref, kc_ref, vc_ref, bias_ref, o_ref):
    L = k_ref.shape[1]
    R = L // GRID_W
    i = pl.program_id(2)
    nk = 2 * NA_QROWS
    krow = jnp.clip(i * NA_QROWS - NA_ROWS // 2, 0, R - nk)
    start = pl.multiple_of(krow * GRID_W, (NA_ROWS // 2) * GRID_W)
    k = k_ref[0, pl.ds(start, nk * GRID_W), :]
    v = v_ref[0, pl.ds(start, nk * GRID_W), :]
    kc, vc = kc_ref[0], vc_ref[0]
    sub = q_ref.shape[1] // NA_SPLIT
    for part in range(NA_SPLIT):
        rows = slice(part * sub, (part + 1) * sub)
        q = q_ref[0, rows, :]
        s = _dot_nt(q, k) + bias_ref[0, 0, rows, :]
        sc = _dot_nt(q, kc)
        m = jnp.maximum(jnp.max(s, axis=-1, keepdims=True), jnp.max(sc, axis=-1, keepdims=True))
        p = jnp.exp(s - m)
        pc = jnp.exp(sc - m)
        denom = jnp.sum(p, axis=-1, keepdims=True) + jnp.sum(pc, axis=-1, keepdims=True)
        o = _dot(p.astype(BF16), v) + _dot(pc.astype(BF16), vc)
        o_ref[0, rows, :] = (o / denom).astype(o_ref.dtype)


def attn_b(qk, qkc, bias):
    B, L, _ = qk.shape
    C = qkc.shape[1]
    tq = NA_QROWS * GRID_W
    nblk = L // tq

    def bias_idx(h, b, i):
        return (jnp.where(i == 0, 0, jnp.where(i == nblk - 1, 2, 1)), h, 0, 0)

    return pl.pallas_call(
        _attn_b_kernel,
        out_shape=jax.ShapeDtypeStruct((B, L, B_HEADS * HEAD_DIM), BF16),
        grid=(B_HEADS, B, nblk),
        in_specs=[
            pl.BlockSpec((1, tq, HEAD_DIM), lambda h, b, i: (b, i, B_Q + h)),
            pl.BlockSpec((1, L, HEAD_DIM), lambda h, b, i: (b, 0, B_K + h)),
            pl.BlockSpec((1, L, HEAD_DIM), lambda h, b, i: (b, 0, B_V + h)),
            pl.BlockSpec((1, C, HEAD_DIM), lambda h, b, i: (b, 0, B_K + h)),
            pl.BlockSpec((1, C, HEAD_DIM), lambda h, b, i: (b, 0, B_V + h)),
            pl.BlockSpec((1, 1, tq, 2 * tq), bias_idx),
        ],
        out_specs=pl.BlockSpec((1, tq, HEAD_DIM), lambda h, b, i: (b, i, h)),
        compiler_params=_cparams(("arbitrary", "arbitrary", "arbitrary")),
        name="attn_b",
    )(qk, qk, qk, qkc, qkc, bias)


def _diff_lambda(lambda_params, lam_init):
    lp = lambda_params.astype(F32)
    return jnp.exp(jnp.sum(lp[0] * lp[1])) - jnp.exp(jnp.sum(lp[2] * lp[3])) + lam_init


ATTN_C_SAFE_SCORE_BOUND = 40.0


def _attn_c_kernel(lam_ref, q_ref, k_ref, v_ref, kc_ref, vc_ref, g_ref, o_ref, vt_ref, vct_ref, *scratch,
                   tk, coef, online):
    if online:
        m_ref, l_ref, acc_ref = scratch
        m_ref[...] = jnp.full(m_ref.shape, NEG_INF, F32)
    else:
        l_ref, acc_ref = scratch
    nk = vt_ref.shape[0]

    @pl.when(pl.program_id(2) == 0)
    def _():
        for j in range(nk):
            vt_ref[j] = v_ref[0, j * tk:(j + 1) * tk, :].astype(F32).T.astype(BF16)
        vct_ref[...] = vc_ref[0].astype(F32).T.astype(BF16)

    q = q_ref[0]
    lane = lax.broadcasted_iota(jnp.int32, q.shape, 1)
    zero = jnp.zeros_like(q)
    qs = (jnp.where(lane < C_SUB_DIM, q, zero), jnp.where(lane < C_SUB_DIM, zero, q))

    l_ref[...] = jnp.zeros(l_ref.shape, F32)
    acc_ref[...] = jnp.zeros(acc_ref.shape, F32)

    def accumulate(kblk, vtblk):
        for s_idx in range(2):
            s = _dot_nt(kblk, qs[s_idx])
            if online:
                m_old = m_ref[s_idx]
                m_new = jnp.maximum(m_old, jnp.max(s, axis=0, keepdims=True))
                alpha = jnp.exp2(m_old - m_new)
                p = jnp.exp2(s - m_new)
                l_ref[s_idx] = alpha * l_ref[s_idx] + jnp.sum(p, axis=0, keepdims=True)
                acc_ref[s_idx] = alpha * acc_ref[s_idx] + _dot(vtblk, p.astype(BF16))
                m_ref[s_idx] = m_new
            else:
                p = jnp.exp2(s)
                l_ref[s_idx] += jnp.sum(p, axis=0, keepdims=True)
                acc_ref[s_idx] += _dot(vtblk, p.astype(BF16))

    def body(j, carry):
        accumulate(k_ref[0, pl.ds(pl.multiple_of(j * tk, tk), tk), :], vt_ref[j])
        return carry

    lax.fori_loop(0, nk, body, 0, unroll=4)
    accumulate(kc_ref[0], vct_ref[...])

    o = acc_ref[0] / l_ref[0] - lam_ref[0] * (acc_ref[1] / l_ref[1])
    y = o * lax.rsqrt(jnp.mean(o * o, axis=0, keepdims=True) + EPS) * g_ref[...] * coef
    o_ref[0] = y.T.astype(o_ref.dtype)


def attn_c(qk, qkc, lam, subln, coef, online, tq=2048, tk=512):
    B, L, _ = qk.shape
    C = qkc.shape[1]
    H, dv = C_HEADS, HEAD_DIM
    stats = [pltpu.VMEM((2, 1, tq), F32)] * (2 if online else 1)
    return pl.pallas_call(
        functools.partial(_attn_c_kernel, tk=tk, coef=coef, online=online),
        out_shape=jax.ShapeDtypeStruct((B, L, H * dv), BF16),
        grid=(B, H, L // tq),
        in_specs=[
            pl.BlockSpec(memory_space=pltpu.SMEM),
            pl.BlockSpec((1, tq, HEAD_DIM), lambda b, h, i: (b, i, C_Q + h)),
            pl.BlockSpec((1, L, HEAD_DIM), lambda b, h, i: (b, 0, C_K + h)),
            pl.BlockSpec((1, L, HEAD_DIM), lambda b, h, i: (b, 0, C_V + h)),
            pl.BlockSpec((1, C, HEAD_DIM), lambda b, h, i: (b, 0, C_K + h)),
            pl.BlockSpec((1, C, HEAD_DIM), lambda b, h, i: (b, 0, C_V + h)),
            pl.BlockSpec((dv, 1), lambda b, h, i: (0, 0)),
        ],
        out_specs=pl.BlockSpec((1, tq, dv), lambda b, h, i: (b, i, h)),
        scratch_shapes=[pltpu.VMEM((L // tk, dv, tk), BF16), pltpu.VMEM((dv, C), BF16)]
        + stats + [pltpu.VMEM((2, dv, tq), F32)],
        compiler_params=_cparams(("arbitrary", "arbitrary", "arbitrary")),
        name="attn_c_online" if online else "attn_c",
    )(lam, qk, qk, qk, qkc, qkc, subln.reshape(dv, 1).astype(F32))


def _retention_kernel(ld_ref, q_ref, k_ref, v_ref, g_ref, qc_ref, kc_ref, vc_ref, gc_ref, *rest, with_ctx):
    if with_ctx:
        y_ref, yc_ref, of_ref, ob_ref, ocf_ref = rest
    else:
        y_ref, of_ref, ob_ref = rest
        yc_ref = ocf_ref = None
    L, C = q_ref.shape[1], qc_ref.shape[1]
    c = min(RET_CHUNK, L)
    cc = min(RET_CHUNK, C)
    h = pl.program_id(1)
    scale = HEAD_DIM ** -0.5

    def decays(lg, reverse, c):
        row = lax.broadcasted_iota(jnp.int32, (c, c), 0).astype(F32)
        col = lax.broadcasted_iota(jnp.int32, (c, c), 1).astype(F32)
        rowp = lax.broadcasted_iota(jnp.int32, (c, HEAD_DIM), 0).astype(F32)
        if reverse:
            rel, q_exp, k_exp = col - row, c - rowp, rowp
        else:
            rel, q_exp, k_exp = row - col, rowp + 1.0, (c - 1.0) - rowp
        intra = jnp.where(rel >= 0, jnp.exp(lg * jnp.maximum(rel, 0.0)), 0.0)
        return intra, jnp.exp(lg * q_exp), jnp.exp(lg * k_exp), jnp.exp(lg * c)

    def step(S, q, k, v, dec, want_out):
        intra, q_decay, k_decay, chunk_decay = dec
        o = None
        if want_out:
            qs = (q.astype(F32) * scale).astype(BF16)
            s = _dot_nt(qs, k) * intra
            o = _dot(s.astype(BF16), v) + _dot(qs, S.astype(BF16)) * q_decay
        kd_t = (k.astype(F32) * k_decay).T.astype(BF16)
        return S * chunk_decay + _dot(kd_t, v), o

    def gated(o, g):
        g = g.astype(F32)
        y = o * lax.rsqrt(jnp.mean(o * o, axis=-1, keepdims=True) + EPS)
        return y * (g * jax.nn.sigmoid(g))

    def lat(ref, j):
        return ref[0, pl.ds(pl.multiple_of(j * c, c), c), :]

    def ctx(ref, j):
        return ref[0, j * cc:(j + 1) * cc, :]

    dec_f = decays(ld_ref[0, h], False, cc)
    dec_b = decays(ld_ref[1, h], True, cc)
    S_f = jnp.zeros((HEAD_DIM, HEAD_DIM), F32)
    for j in range(C // cc):
        S_f, o = step(S_f, ctx(qc_ref, j), ctx(kc_ref, j), ctx(vc_ref, j), dec_f, with_ctx)
        if with_ctx:
            ocf_ref[j * cc:(j + 1) * cc, :] = o
    S_b = jnp.zeros((HEAD_DIM, HEAD_DIM), F32)
    for j in reversed(range(C // cc)):
        S_b, o = step(S_b, ctx(qc_ref, j), ctx(kc_ref, j), ctx(vc_ref, j), dec_b, with_ctx)
        if with_ctx:
            yc_ref[0, j * cc:(j + 1) * cc, :] = gated(o + ocf_ref[j * cc:(j + 1) * cc, :], ctx(gc_ref, j)).astype(yc_ref.dtype)
    if c != cc:
        dec_f = decays(ld_ref[0, h], False, c)
        dec_b = decays(ld_ref[1, h], True, c)

    n = L // c

    def both(t, carry):
        S_f, S_b = carry
        S_f, o_f = step(S_f, lat(q_ref, t), lat(k_ref, t), lat(v_ref, t), dec_f, True)
        of_ref[pl.ds(pl.multiple_of(t * c, c), c), :] = o_f
        j = n - 1 - t
        S_b, o_b = step(S_b, lat(q_ref, j), lat(k_ref, j), lat(v_ref, j), dec_b, True)
        ob_ref[pl.ds(pl.multiple_of(j * c, c), c), :] = o_b
        return S_f, S_b

    lax.fori_loop(0, n, both, (S_f, S_b), unroll=2)

    def finish(j, carry):
        sl = pl.ds(pl.multiple_of(j * c, c), c)
        y_ref[0, sl, :] = gated(of_ref[sl, :] + ob_ref[sl, :], lat(g_ref, j)).astype(y_ref.dtype)
        return carry

    lax.fori_loop(0, n, finish, 0, unroll=2)


def retention(proj, projc, log_decay, with_ctx):
    B, L, _ = proj.shape
    C = projc.shape[1]

    def lat(blk):
        return pl.BlockSpec((1, L, HEAD_DIM), lambda b, h: (b, 0, blk + h))

    def ctx(blk):
        return pl.BlockSpec((1, C, HEAD_DIM), lambda b, h: (b, 0, blk + h))

    out_shape = [jax.ShapeDtypeStruct((B, L, D_HEADS * HEAD_DIM), BF16)]
    out_specs = [pl.BlockSpec((1, L, HEAD_DIM), lambda b, h: (b, 0, h))]
    scratch = [pltpu.VMEM((L, HEAD_DIM), F32), pltpu.VMEM((L, HEAD_DIM), F32)]
    if with_ctx:
        out_shape.append(jax.ShapeDtypeStruct((B, C, D_HEADS * HEAD_DIM), BF16))
        out_specs.append(pl.BlockSpec((1, C, HEAD_DIM), lambda b, h: (b, 0, h)))
        scratch.append(pltpu.VMEM((C, HEAD_DIM), F32))
    outs = pl.pallas_call(
        functools.partial(_retention_kernel, with_ctx=with_ctx),
        out_shape=out_shape,
        grid=(B, D_HEADS),
        in_specs=[pl.BlockSpec(memory_space=pltpu.SMEM),
                  lat(D_Q), lat(D_K), lat(D_V), lat(D_G), ctx(D_Q), ctx(D_K), ctx(D_V), ctx(D_G)],
        out_specs=out_specs,
        scratch_shapes=scratch,
        compiler_params=_cparams(("arbitrary", "arbitrary")),
        name="retention_ctx" if with_ctx else "retention",
    )(log_decay.astype(F32), proj, proj, proj, proj, projc, projc, projc, projc)
    return (outs[0], outs[1]) if with_ctx else (outs[0], None)


def _ctx_attn_kernel(sink_ref, lam_ref, qk_ref, g_ref, o_ref, *, coef):
    def blk(idx):
        return qk_ref[0, :, idx * HEAD_DIM:(idx + 1) * HEAD_DIM]

    def softmax_out(q, k, v, sink=None, exp=jnp.exp):
        s = _dot_nt(q, k)
        m = jnp.max(s, axis=-1, keepdims=True)
        if sink is not None:
            m = jnp.maximum(m, sink)
        p = exp(s - m)
        denom = jnp.sum(p, axis=-1, keepdims=True)
        if sink is not None:
            denom = denom + jnp.exp(sink - m)
        return _dot(p.astype(BF16), v) / denom

    group = A_HEADS // A_KV_HEADS
    for h in range(A_HEADS):
        o = softmax_out(blk(A_Q + h), blk(A_K + h // group), blk(A_V + h // group), sink_ref[h])
        o_ref[0, :, h * HEAD_DIM:(h + 1) * HEAD_DIM] = o.astype(o_ref.dtype)
    for h in range(B_HEADS):
        o = softmax_out(blk(B_Q + h), blk(B_K + h), blk(B_V + h))
        o_ref[0, :, (A_HEADS + h) * HEAD_DIM:(A_HEADS + h + 1) * HEAD_DIM] = o.astype(o_ref.dtype)
    lane = lax.broadcasted_iota(jnp.int32, (qk_ref.shape[1], HEAD_DIM), 1)
    for h in range(C_HEADS):
        q, k, v = blk(C_Q + h), blk(C_K + h), blk(C_V + h)
        zero = jnp.zeros_like(q)
        o1 = softmax_out(jnp.where(lane < C_SUB_DIM, q, zero), k, v, exp=jnp.exp2)
        o2 = softmax_out(jnp.where(lane < C_SUB_DIM, zero, q), k, v, exp=jnp.exp2)
        o = o1 - lam_ref[0] * o2
        y = o * lax.rsqrt(jnp.mean(o * o, axis=-1, keepdims=True) + EPS) * g_ref[...] * coef
        base = A_HEADS + B_HEADS + h
        o_ref[0, :, base * HEAD_DIM:(base + 1) * HEAD_DIM] = y.astype(o_ref.dtype)


def ctx_attn(qkc, sink, lam, subln, coef):
    B, C, W = qkc.shape
    n_out = (A_HEADS + B_HEADS + C_HEADS) * HEAD_DIM
    return pl.pallas_call(
        functools.partial(_ctx_attn_kernel, coef=coef),
        out_shape=jax.ShapeDtypeStruct((B, C, n_out), BF16),
        grid=(B,),
        in_specs=[
            pl.BlockSpec(memory_space=pltpu.SMEM),
            pl.BlockSpec(memory_space=pltpu.SMEM),
            pl.BlockSpec((1, C, W), lambda b: (b, 0, 0)),
            pl.BlockSpec((1, HEAD_DIM), lambda b: (0, 0)),
        ],
        out_specs=pl.BlockSpec((1, C, n_out), lambda b: (b, 0, 0)),
        compiler_params=_cparams(("arbitrary",)),
        name="ctx_attn",
    )(sink, lam, qkc, subln.reshape(1, HEAD_DIM).astype(F32))


def _out_proj_kernel(*refs, n_parts):
    parts = refs[:n_parts]
    x_ref, gate_ref, w_ref, o_ref = refs[n_parts:]
    mix = jnp.concatenate([p[0] for p in parts], axis=-1)
    o_ref[0] = x_ref[0] + gate_ref[0] * _dot(mix, w_ref[0])


def out_proj(parts, x, gate, w, layer, tm, tn=512):
    B, L, D = x.shape
    part_specs = [pl.BlockSpec((1, tm, p.shape[2]), lambda b, i, j: (b, i, 0)) for p in parts]
    return pl.pallas_call(
        functools.partial(_out_proj_kernel, n_parts=len(parts)),
        out_shape=jax.ShapeDtypeStruct((B, L, D), F32),
        grid=(B, L // tm, D // tn),
        in_specs=part_specs + [
            pl.BlockSpec((1, tm, tn), lambda b, i, j: (b, i, j)),
            pl.BlockSpec((1, 1, tn), lambda b, i, j: (b, 0, j)),
            pl.BlockSpec((1, w.shape[1], tn), lambda b, i, j: (layer, 0, j)),
        ],
        out_specs=pl.BlockSpec((1, tm, tn), lambda b, i, j: (b, i, j)),
        compiler_params=_cparams(("arbitrary", "arbitrary", "arbitrary")),
        name="out_proj",
    )(*parts, x, gate, w)


def _ffn_prep_kernel(*refs, n_lat_tiles):
    if n_lat_tiles is None:
        x_ref, gain_ref, shift_ref, scale_ref, rh_ref, rl_ref, h_ref, id_ref, w_ref = refs
        xc_ref = None
    else:
        x_ref, xc_ref, gain_ref, shift_ref, scale_ref, rh_ref, rl_ref, h_ref, id_ref, w_ref = refs

    def body(src_ref):
        h = _modulated(src_ref[...], gain_ref[...], shift_ref[0], scale_ref[0])
        h_ref[...] = _pack_bf16_pairs(h)

        h_hi, h_lo = _split_bf16(h)
        logits = _dot(h_hi, rh_ref[...]) + _dot(h_lo, rh_ref[...]) + _dot(h_hi, rl_ref[...])
        lane = lax.broadcasted_iota(jnp.int32, logits.shape, 1)
        big = jnp.int32(logits.shape[1])

        def first_argmax(vals):
            top = jnp.max(vals, axis=-1, keepdims=True)
            return top, jnp.min(jnp.where(vals == top, lane, big), axis=-1, keepdims=True)

        is_group = lane < N_GROUPS
        g_top, g_idx = first_argmax(jnp.where(is_group, logits, NEG_INF))
        g_val = 1.0 / jnp.sum(jnp.where(is_group, jnp.exp(logits - g_top), 0.0), axis=-1, keepdims=True)
        lo = N_GROUPS + EXPERTS_PER_GROUP * g_idx
        e_logits = jnp.where((lane >= lo) & (lane < lo + EXPERTS_PER_GROUP), logits, NEG_INF)
        v1, i1 = first_argmax(e_logits)
        v2, i2 = first_argmax(jnp.where(lane == i1, NEG_INF, e_logits))
        e21 = jnp.exp(v2 - v1)
        w1 = g_val / (1.0 + e21)
        w2 = g_val * e21 / (1.0 + e21)
        id_ref[...] = jnp.where(lane == 0, i1 - N_GROUPS, jnp.where(lane == 1, i2 - N_GROUPS, 0))
        w_ref[...] = jnp.where(lane == 0, w1, jnp.where(lane == 1, w2, 0.0))

    if xc_ref is None:
        body(x_ref)
    else:
        pl.when(pl.program_id(0) < n_lat_tiles)(lambda: body(x_ref))
        pl.when(pl.program_id(0) >= n_lat_tiles)(lambda: body(xc_ref))


def ffn_prep(x, xc, gain, shift, scale, r_cat, t=256):
    B, L, D = x.shape
    n_lat = B * L // t
    lat_per_seg = L // t
    if xc is None:
        T = B * L
        xs = [x.reshape(B * L, D)]
        x_specs = [pl.BlockSpec((t, D), lambda i: (i, 0))]
        seg = lambda i: (i // lat_per_seg, 0, 0)
        n_lat_tiles = None
    else:
        C = xc.shape[1]
        T = B * (L + C)
        xs = [x.reshape(B * L, D), xc.reshape(B * C, D)]
        x_specs = [pl.BlockSpec((t, D), lambda i: (jnp.minimum(i, n_lat - 1), 0)),
                   pl.BlockSpec((t, D), lambda i: (jnp.maximum(i - n_lat, 0), 0))]
        seg = lambda i: (jnp.minimum(i // lat_per_seg, B), 0, 0)
        n_lat_tiles = n_lat
    return pl.pallas_call(
        functools.partial(_ffn_prep_kernel, n_lat_tiles=n_lat_tiles),
        out_shape=[jax.ShapeDtypeStruct((T, D // 2), jnp.uint32),
                   jax.ShapeDtypeStruct((T, 128), jnp.int32),
                   jax.ShapeDtypeStruct((T, 128), F32)],
        grid=(T // t,),
        in_specs=x_specs + [
            pl.BlockSpec((1, D), lambda i: (0, 0)),
            pl.BlockSpec((1, 1, D), seg),
            pl.BlockSpec((1, 1, D), seg),
            pl.BlockSpec((D, 128), lambda i: (0, 0)),
            pl.BlockSpec((D, 128), lambda i: (0, 0)),
        ],
        out_specs=[pl.BlockSpec((t, D // 2), lambda i: (i, 0)),
                   pl.BlockSpec((t, 128), lambda i: (i, 0)),
                   pl.BlockSpec((t, 128), lambda i: (i, 0))],
        compiler_params=_cparams(("arbitrary",)),
        name="ffn_prep",
    )(*xs, gain, shift, scale, *_split_bf16(r_cat))


def _route(ids, tm):
    A = ids.shape[0] * TOP_K
    flat_e = ids.reshape(A)
    onehot = (flat_e[:, None] == jnp.arange(N_EXPERTS, dtype=jnp.int32)[None, :]).astype(jnp.int32)
    csum = jnp.cumsum(onehot, axis=0)
    rank = jnp.sum(csum * onehot, axis=1) - 1
    counts = csum[-1]
    pcounts = (counts + tm - 1) // tm * tm
    pends = jnp.cumsum(pcounts)
    pstarts = pends - pcounts
    dest = jnp.sum(onehot * pstarts[None, :], axis=1) + rank
    NB = (A + N_EXPERTS * (tm - 1)) // tm
    row_tok = jnp.zeros((NB * tm,), jnp.int32).at[dest].set(jnp.arange(A, dtype=jnp.int32) // TOP_K)
    block_start = jnp.arange(NB, dtype=jnp.int32) * tm
    block_e = jnp.minimum(jnp.sum((pends[None, :] <= block_start[:, None]).astype(jnp.int32), axis=1), N_EXPERTS - 1)
    nonempty = (pcounts > 0).astype(jnp.int32)
    expert_list = jnp.argsort(1 - nonempty, stable=True).astype(jnp.int32)
    block_ord = (jnp.cumsum(nonempty) - 1)[block_e].astype(jnp.int32)
    block_first = jnp.concatenate([jnp.ones((1,), jnp.int32), (block_e[1:] != block_e[:-1]).astype(jnp.int32)])
    sched = (expert_list, jnp.sum(nonempty).reshape(1).astype(jnp.int32), block_ord, block_first)
    return dest.astype(jnp.int32), row_tok, sched, (pends[-1:] // tm).astype(jnp.int32)


def _row_copy(src, src_row, dst, dst_row, sem):
    return pltpu.make_async_copy(src.at[pl.ds(src_row, 1), :], dst.at[pl.ds(dst_row, 1), :], sem)


W_RING = 3


def _expert_kernel(elist_ref, nexp_ref, ord_ref, first_ref, nused_ref, tok_ref, h_ref, wg_hbm, wu_hbm, wd_hbm,
                   o_ref, xbuf, sem, wg_buf, wu_buf, wd_buf, wsem, *, layer):
    i = pl.program_id(0)
    tm = o_ref.shape[0]
    n_used = nused_ref[0]
    n_exp = nexp_ref[0]

    def weight_copies(k):
        e, wslot = elist_ref[k], k % W_RING
        return [pltpu.make_async_copy(hbm.at[layer, e], buf.at[wslot], wsem.at[wslot])
                for hbm, buf in ((wg_hbm, wg_buf), (wu_hbm, wu_buf), (wd_hbm, wd_buf))]

    def start_weights(k):
        for c in weight_copies(k):
            c.start()

    def row(r, blk, slot):
        return _row_copy(h_ref, tok_ref[blk * tm + r], xbuf.at[slot], r, sem.at[slot])

    def gather(blk, slot):
        def issue(r2, carry):
            for k in range(2):
                row(2 * r2 + k, blk, slot).start(priority=k)
            return carry
        lax.fori_loop(0, tm // 2, issue, 0, unroll=4)

    def drain(slot):
        def wait(r, carry):
            row(r, 0, slot).wait()
            return carry
        lax.fori_loop(0, tm, wait, 0, unroll=8)

    @pl.when(i == 0)
    def _():
        gather(0, 0)
        start_weights(0)

        @pl.when(n_exp > 1)
        def _():
            start_weights(1)

    @pl.when(i + 1 < n_used)
    def _():
        gather(i + 1, (i + 1) % 2)

    @pl.when(i < n_used)
    def _():
        k = ord_ref[i]

        @pl.when(first_ref[i] == 1)
        def _():
            @pl.when(k + 2 < n_exp)
            def _():
                start_weights(k + 2)
            for c in weight_copies(k):
                c.wait()

        slot = i % 2
        wslot = k % W_RING
        drain(slot)
        x = _unpack_bf16_pairs(xbuf[slot]).astype(BF16)
        a = _dot(x, wg_buf[wslot])
        u = _dot(x, wu_buf[wslot])
        hmid = (a * jax.nn.sigmoid(a) * u).astype(BF16)
        o_ref[...] = _pack_bf16_pairs(_dot(hmid, wd_buf[wslot]))

    @pl.when(i >= n_used)
    def _():
        o_ref[...] = jnp.zeros(o_ref.shape, o_ref.dtype)


def experts(sched, n_used, row_tok, h, w_gate, w_up, w_down, layer, tm):
    P = row_tok.shape[0]
    _, _, D, De = w_gate.shape
    hbm = pl.BlockSpec(memory_space=pl.ANY)
    return pl.pallas_call(
        functools.partial(_expert_kernel, layer=layer),
        out_shape=jax.ShapeDtypeStruct((P, D // 2), jnp.uint32),
        grid_spec=pltpu.PrefetchScalarGridSpec(
            num_scalar_prefetch=6,
            grid=(P // tm,),
            in_specs=[hbm, hbm, hbm, hbm],
            out_specs=pl.BlockSpec((tm, D // 2), lambda i, *_: (i, 0)),
            scratch_shapes=[pltpu.VMEM((2, tm, D // 2), jnp.uint32), pltpu.SemaphoreType.DMA((2,)),
                            pltpu.VMEM((W_RING, D, De), BF16), pltpu.VMEM((W_RING, D, De), BF16),
                            pltpu.VMEM((W_RING, De, D), BF16), pltpu.SemaphoreType.DMA((W_RING,))],
        ),
        compiler_params=_cparams(("arbitrary",)),
        name="moe_experts",
    )(*sched, n_used, row_tok, h, w_gate, w_up, w_down)


def _combine_kernel(dest_ref, x_ref, gate_ref, w_ref, y_ref, o_ref, buf, sem, *, tok_offset):
    t, D = x_ref.shape
    i = pl.program_id(0)
    slot = i % 2

    def gather(step, slot):
        base = (tok_offset + step * t) * TOP_K

        def issue(r, carry):
            for k in range(TOP_K):
                _row_copy(y_ref, dest_ref[base + r * TOP_K + k], buf.at[slot, k], r, sem.at[slot]).start(priority=k)
            return carry
        lax.fori_loop(0, t, issue, 0, unroll=4)

    def drain(r, carry):
        _row_copy(y_ref, 0, buf.at[slot, 0], 0, sem.at[slot]).wait()
        return carry

    @pl.when(i == 0)
    def _():
        gather(0, 0)

    @pl.when(i + 1 < pl.num_programs(0))
    def _():
        gather(i + 1, 1 - slot)

    lax.fori_loop(0, t * TOP_K, drain, 0, unroll=8)
    w = w_ref[...]
    y = w[:, 0:1] * _unpack_bf16_pairs(buf[slot, 0]) + w[:, 1:2] * _unpack_bf16_pairs(buf[slot, 1])
    o_ref[...] = x_ref[...] + gate_ref[0] * y


def combine(dest, x, gate, rw, ys, tok_offset, t=256):
    B, L, D = x.shape
    per_seg = L // t
    woff = tok_offset // t
    out = pl.pallas_call(
        functools.partial(_combine_kernel, tok_offset=tok_offset),
        out_shape=jax.ShapeDtypeStruct((B * L, D), F32),
        grid_spec=pltpu.PrefetchScalarGridSpec(
            num_scalar_prefetch=1,
            grid=(B * L // t,),
            in_specs=[
                pl.BlockSpec((t, D), lambda i, d: (i, 0)),
                pl.BlockSpec((1, 1, D), lambda i, d: (i // per_seg, 0, 0)),
                pl.BlockSpec((t, 128), lambda i, d: (woff + i, 0)),
                pl.BlockSpec(memory_space=pl.ANY),
            ],
            out_specs=pl.BlockSpec((t, D), lambda i, d: (i, 0)),
            scratch_shapes=[pltpu.VMEM((2, TOP_K, t, D // 2), jnp.uint32), pltpu.SemaphoreType.DMA((2,))],
        ),
        compiler_params=_cparams(("arbitrary",)),
        name="moe_combine",
    )(dest, x.reshape(B * L, D), gate, rw, ys)
    return out.reshape(B, L, D)


MOE_TM = 256


def kernel(x, c, ctx, c_ctx, w_mod, b_mod, norm_mix, norm_ffn, w_in, w_out, qk_norm_a, sink_a, qk_norm_b, rpb_b,
           qk_norm_c, lambda_c, subln_c, ret_log_decay, router_group, router_expert, w_gate, w_up, w_down):
    B, L, D = x.shape
    C = ctx.shape[1]
    depth = w_mod.shape[0]

    cvec = jnp.zeros((8, D), F32).at[:B].set(c).at[B].set(c_ctx)
    mod = mod_vectors(cvec, w_mod, b_mod).reshape(depth, 8, 6, D)

    cosa, sina = _rope_tables(L, HEAD_DIM)
    cosc, sinc = _rope_tables(L, C_SUB_DIM)
    one_tab = jnp.ones((C, HEAD_DIM), F32)
    zero_tab = jnp.zeros((C, HEAD_DIM), F32)

    w_in_b, w_out_b = w_in.astype(BF16), w_out.astype(BF16)
    w_gate_b, w_up_b, w_down_b = w_gate.astype(BF16), w_up.astype(BF16), w_down.astype(BF16)

    xc = ctx
    for l in range(depth):
        with_ctx = l < depth - 1
        lat = lambda j: mod[l, :B, j][:, None, :]
        cx = lambda j: jnp.broadcast_to(mod[l, B, j][None, None, :], (B, 1, D))
        gain_mix = norm_mix[l][None, :]

        proj = in_proj(x, gain_mix, lat(0), lat(1), w_in_b, l, tm=512)
        projc = in_proj(xc.reshape(1, B * C, D), gain_mix, cx(0)[:1], cx(1)[:1], w_in_b, l,
                        tm=B * C).reshape(B, C, IN_COLS)
        gain_c = jnp.tile(qk_norm_c[l], (1, HEAD_DIM // C_SUB_DIM))
        qk = qk_prep(proj, cosa, sina, cosc, sinc, qk_norm_a[l], qk_norm_b[l], gain_c)
        qkc = qk_prep(projc, one_tab, zero_tab, one_tab, zero_tab, qk_norm_a[l], qk_norm_b[l], gain_c)

        lam_init = 0.8 - 0.6 * math.exp(-0.3 * l)
        lam = _diff_lambda(lambda_c[l], lam_init).reshape(1)
        out_a = attn_a(qk, qkc, sink_a[l])
        out_b = attn_b(qk, qkc, _na_bias_tables(rpb_b[l], L // GRID_W))
        score_bound = (C_SUB_DIM ** 0.5) * jnp.max(jnp.abs(qk_norm_c[l, 0])) * jnp.max(jnp.abs(qk_norm_c[l, 1]))
        attn_c_l = functools.partial(attn_c, qk, qkc, lam, subln_c[l], 1.0 - lam_init)
        out_c = lax.cond(score_bound <= ATTN_C_SAFE_SCORE_BOUND,
                         lambda: attn_c_l(online=False), lambda: attn_c_l(online=True))
        out_d, out_dc = retention(proj, projc, ret_log_decay[l], with_ctx)

        x = out_proj([out_a, out_b, out_c, out_d], x, lat(2), w_out_b, l, tm=1024)
        if with_ctx:
            out_abc_c = ctx_attn(qkc, sink_a[l], lam, subln_c[l], 1.0 - lam_init)
            flat = lambda t: t.reshape(1, B * C, t.shape[-1])
            xc = out_proj([flat(out_abc_c), flat(out_dc)], flat(xc), cx(2)[:1], w_out_b, l,
                          tm=B * C).reshape(B, C, D)

        r_cat = jnp.zeros((D, 128), F32).at[:, :N_GROUPS].set(router_group[l])
        r_cat = r_cat.at[:, N_GROUPS:N_GROUPS + N_EXPERTS].set(router_expert[l])
        if with_ctx:
            shift = jnp.concatenate([lat(3), cx(3)[:1]], axis=0)
            scale = jnp.concatenate([lat(4), cx(4)[:1]], axis=0)
        else:
            shift, scale = lat(3), lat(4)
        h2, rid, rw = ffn_prep(x, xc if with_ctx else None, norm_ffn[l][None, :], shift, scale, r_cat)
        dest, row_tok, sched, n_used = _route(rid[:, :TOP_K], MOE_TM)
        ys = experts(sched, n_used, row_tok, h2, w_gate_b, w_up_b, w_down_b, l, MOE_TM)
        x = combine(dest, x, lat(5), rw, ys, 0)
        if with_ctx:
            xc = combine(dest, xc, cx(5), rw, ys, B * L)
    return x
```
